```python
import math
import jax, jax.numpy as jnp
from jax import lax
import numpy as np

D_MODEL = 1024
BATCH = 16
SEQ = 2048
DEPTH = 1
DEC_BATCH = 32
DEC_SEQ = 2048
PAST_LEN = 128

D_MIX = D_MODEL
HEAD_DIM = 64
N_Q_HEADS = 8
N_KV_HEADS = 2
D_ATT = N_Q_HEADS * HEAD_DIM
D_KV = N_KV_HEADS * HEAD_DIM
D_LRU = D_MIX - D_ATT
N_LRU_BLOCKS = 8
LRU_BLOCK = D_LRU // N_LRU_BLOCKS
CONV_W = 4
CONV_PAD_LEFT = 2
LRU_C = 8.0
D_IN = D_ATT + 2 * D_KV + 2 * D_LRU
SPLITS = (D_ATT, D_ATT + D_KV, D_ATT + 2 * D_KV, D_ATT + 2 * D_KV + D_LRU)
GRID_W = 64
ROPE_THETA = 10000.0
ROPE_FREQS = HEAD_DIM // 4
Q_BLOCK = 128
N_EXPERTS = 64
TOP_K = 6
N_GROUPS = 8
TOPK_GROUPS = 4
D_EXPERT = 256
D_SHARED = 256
ROUTED_SCALE = 2.5
MOE_BLOCK = 128
N_MOD = 6
EPS = 1e-6
DN_ALPHA = (2.0 * DEPTH) ** 0.25
DN_BETA = (8.0 * DEPTH) ** -0.25

kernel_name = 'hybrid_rglru_gqa_moe_encoder'


def ln_plain(x):
    xf = x.astype(jnp.float32)
    mu = jnp.mean(xf, axis=-1, keepdims=True)
    var = jnp.mean(jnp.square(xf - mu), axis=-1, keepdims=True)
    return ((xf - mu) * lax.rsqrt(var + EPS)).astype(x.dtype)


def ln_affine(x, g, b):
    xf = x.astype(jnp.float32)
    mu = jnp.mean(xf, axis=-1, keepdims=True)
    var = jnp.mean(jnp.square(xf - mu), axis=-1, keepdims=True)
    y = (xf - mu) * lax.rsqrt(var + EPS) * g.astype(jnp.float32) + b.astype(jnp.float32)
    return y.astype(x.dtype)


def rms_norm(x, g):
    xf = x.astype(jnp.float32)
    y = xf * lax.rsqrt(jnp.mean(jnp.square(xf), axis=-1, keepdims=True) + EPS) * g.astype(jnp.float32)
    return y.astype(x.dtype)


def axial_rope_tables(S):
    rows = S // GRID_W
    row_idx = jnp.repeat(jnp.arange(rows, dtype=jnp.float32), GRID_W)
    col_idx = jnp.tile(jnp.arange(GRID_W, dtype=jnp.float32), rows)
    inv_freq = ROPE_THETA ** (-jnp.arange(ROPE_FREQS, dtype=jnp.float32) / ROPE_FREQS)
    ang = jnp.stack([row_idx[:, None] * inv_freq, col_idx[:, None] * inv_freq], axis=1)
    return jnp.cos(ang), jnp.sin(ang)


def apply_rope(x, cos, sin):
    B, S, H, _ = x.shape
    xr = x.astype(jnp.float32).reshape(B, S, H, 2, 2, ROPE_FREQS)
    x1 = xr[..., 0, :]
    x2 = xr[..., 1, :]
    c = cos[None, :, None]
    s = sin[None, :, None]
    out = jnp.stack([x1 * c - x2 * s, x2 * c + x1 * s], axis=-2)
    return out.reshape(B, S, H, HEAD_DIM).astype(x.dtype)


def block_attention(q, k, v):
    B, S = q.shape[0], q.shape[1]
    nq = S // Q_BLOCK
    G = N_Q_HEADS // N_KV_HEADS
    qb = q.reshape(B, nq, Q_BLOCK, N_KV_HEADS, G, HEAD_DIM).transpose(1, 0, 3, 4, 2, 5)
    kt = k.transpose(0, 2, 1, 3)
    vt = v.transpose(0, 2, 1, 3)
    scale = HEAD_DIM ** -0.5

    def one_block(qblk):
        s = jnp.einsum('bkgqd,bksd->bkgqs', qblk, kt, preferred_element_type=jnp.float32) * scale
        p = jax.nn.softmax(s, axis=-1).astype(vt.dtype)
        return jnp.einsum('bkgqs,bksd->bkgqd', p, vt)

    o = lax.map(one_block, qb)
    return o.transpose(1, 0, 4, 2, 3, 5).reshape(B, S, D_ATT)


def centred_conv(x, w, b):
    S = x.shape[1]
    xp = jnp.pad(x, ((0, 0), (CONV_PAD_LEFT, CONV_W - 1 - CONV_PAD_LEFT), (0, 0)))
    out = xp[:, 0:S] * w[0]
    for j in range(1, CONV_W):
        out = out + xp[:, j:j + S] * w[j]
    return out + b


def _lin_combine(left, right):
    a_l, b_l = left
    a_r, b_r = right
    return a_l * a_r, a_r * b_l + b_r


def rglru_direction(xc, wa, ba, wx, bx, lam, reverse):
    B, S, _ = xc.shape
    xb = xc.reshape(B, S, N_LRU_BLOCKS, LRU_BLOCK)
    r = jax.nn.sigmoid((jnp.einsum('bsnc,ncd->bsnd', xb, wa).reshape(B, S, D_LRU) + ba).astype(jnp.float32))
    i = jax.nn.sigmoid((jnp.einsum('bsnc,ncd->bsnd', xb, wx).reshape(B, S, D_LRU) + bx).astype(jnp.float32))
    log_a = -LRU_C * r * jax.nn.softplus(-lam.astype(jnp.float32))
    a = jnp.exp(log_a)
    gain = jnp.sqrt(-jnp.expm1(2.0 * log_a))
    bterm = gain * i * xc.astype(jnp.float32)
    _, h = lax.associative_scan(_lin_combine, (a, bterm), axis=1, reverse=reverse)
    return h


def mixer(u, p):
    B, S, _ = u.shape
    proj = u @ p['w_in']
    q, k, v, xl, gl = jnp.split(proj, SPLITS, axis=-1)
    q = rms_norm(q.reshape(B, S, N_Q_HEADS, HEAD_DIM), p['q_norm_g'])
    k = rms_norm(k.reshape(B, S, N_KV_HEADS, HEAD_DIM), p['k_norm_g'])
    v = v.reshape(B, S, N_KV_HEADS, HEAD_DIM)
    cos, sin = axial_rope_tables(S)
    att = block_attention(apply_rope(q, cos, sin), apply_rope(k, cos, sin), v)
    xc = centred_conv(xl, p['conv_w'], p['conv_b'])
    h = (rglru_direction(xc, p['lru_wa'][0], p['lru_ba'][0], p['lru_wx'][0], p['lru_bx'][0], p['lru_lambda'][0], False)
         + rglru_direction(xc, p['lru_wa'][1], p['lru_ba'][1], p['lru_wx'][1], p['lru_bx'][1], p['lru_lambda'][1], True))
    lru = jax.nn.gelu(gl) * h.astype(u.dtype)
    merged = jnp.concatenate([rms_norm(att, p['attn_out_g']), rms_norm(lru, p['lru_out_g'])], axis=-1)
    return merged @ p['w_out']


def swiglu(x, w_in, w_out):
    g, up = jnp.split(x @ w_in, 2, axis=-1)
    return (jax.nn.silu(g) * up) @ w_out


def routed_experts(xt, idx, w, w_exp_in, w_exp_out):
    T = xt.shape[0]
    TK = T * TOP_K
    flat_e = idx.reshape(TK)
    flat_tok = jnp.repeat(jnp.arange(T, dtype=jnp.int32), TOP_K)
    flat_w = w.reshape(TK)
    order = jnp.argsort(flat_e)
    se = flat_e[order]
    stok = flat_tok[order]
    sw = flat_w[order]
    counts = jnp.bincount(flat_e, length=N_EXPERTS)
    padded = (counts + MOE_BLOCK - 1) // MOE_BLOCK * MOE_BLOCK
    pad_end = jnp.cumsum(padded)
    pad_start = pad_end - padded
    start = jnp.cumsum(counts) - counts
    dest = pad_start[se] + (jnp.arange(TK, dtype=jnp.int32) - start[se])
    P = ((TK + MOE_BLOCK - 1) // MOE_BLOCK + N_EXPERTS) * MOE_BLOCK
    nblk = P // MOE_BLOCK
    buf_tok = jnp.zeros((P,), jnp.int32).at[dest].set(stok)
    buf_w = jnp.zeros((P,), jnp.float32).at[dest].set(sw)
    blk_start = jnp.arange(nblk, dtype=jnp.int32) * MOE_BLOCK
    blk_e = jnp.minimum(jnp.sum(pad_end[None, :] <= blk_start[:, None], axis=1), N_EXPERTS - 1)

    def body(acc, blk):
        tok, gw, e = blk
        yb = swiglu(xt[tok], w_exp_in[e], w_exp_out[e]) * gw[:, None].astype(xt.dtype)
        return acc.at[tok].add(yb), None

    out, _ = lax.scan(body, jnp.zeros_like(xt),
                      (buf_tok.reshape(nblk, MOE_BLOCK), buf_w.reshape(nblk, MOE_BLOCK), blk_e))
    return out


def moe(u, p):
    B, S, D = u.shape
    T = B * S
    xt = u.reshape(T, D)
    scores = jax.nn.sigmoid((xt @ p['w_router']).astype(jnp.float32))
    choice = scores + p['router_bias'].astype(jnp.float32)
    grp = choice.reshape(T, N_GROUPS, N_EXPERTS // N_GROUPS)
    grp_score = jnp.sum(lax.top_k(grp, 2)[0], axis=-1)
    _, top_grp = lax.top_k(grp_score, TOPK_GROUPS)
    grp_mask = jnp.any(top_grp[:, :, None] == jnp.arange(N_GROUPS)[None, None, :], axis=1)
    expert_mask = jnp.repeat(grp_mask, N_EXPERTS // N_GROUPS, axis=1)
    _, idx = lax.top_k(jnp.where(expert_mask, choice, -jnp.inf), TOP_K)
    w = jnp.take_along_axis(scores, idx, axis=1)
    w = w / jnp.sum(w, axis=-1, keepdims=True) * ROUTED_SCALE
    routed = routed_experts(xt, idx, w, p['w_exp_in'], p['w_exp_out'])
    shared = swiglu(xt, p['w_sh_in'], p['w_sh_out'])
    return (routed + shared).reshape(B, S, D)


def encoder_layer(x, c, p):
    mod = jax.nn.silu(c) @ p['w_ada'] + p['b_ada']
    shift1, scale1, gate1, shift2, scale2, gate2 = [m[:, None, :] for m in jnp.split(mod, N_MOD, axis=-1)]
    u = ln_plain(x) * (1.0 + scale1) + shift1
    x = ln_affine(DN_ALPHA * x + gate1 * mixer(u, p), p['ln1_g'], p['ln1_b'])
    u = ln_plain(x) * (1.0 + scale2) + shift2
    x = ln_affine(DN_ALPHA * x + gate2 * moe(u, p), p['ln2_g'], p['ln2_b'])
    return x


def trunk(x, c, params):
    for l in range(DEPTH):
        p = {name: arr[l] for name, arr in params.items()}
        x = encoder_layer(x, c, p)
    return x


def setup_inputs(seed: int = 0) -> dict:
    key = jax.random.key(seed)
    ks = jax.random.split(key, 32)
    L = DEPTH

    def nrm(k, shape, scale):
        return jax.random.normal(k, shape, jnp.float32) * scale

    col = jnp.arange(D_IN)
    v_cols = (col >= D_ATT + D_KV) & (col < D_ATT + 2 * D_KV)
    w_in = nrm(ks[6], (L, D_MODEL, D_IN), D_MODEL ** -0.5) * jnp.where(v_cols, DN_BETA, 1.0)
    a0 = jax.random.uniform(ks[16], (L, 2, D_LRU), jnp.float32, 0.9, 0.999)
    return {
        'x_prompt': nrm(ks[0], (BATCH, SEQ, D_MODEL), 1.0),
        'x_sample': nrm(ks[1], (DEC_BATCH, DEC_SEQ, D_MODEL), 1.0),
        'c_prompt': nrm(ks[2], (BATCH, D_MODEL), 1.0),
        'c_sample': nrm(ks[3], (DEC_BATCH, D_MODEL), 1.0),
        'w_ada': nrm(ks[4], (L, D_MODEL, N_MOD * D_MODEL), D_MODEL ** -0.5),
        'b_ada': nrm(ks[5], (L, N_MOD * D_MODEL), 0.02),
        'w_in': w_in,
        'q_norm_g': 1.0 + nrm(ks[7], (L, HEAD_DIM), 0.1),
        'k_norm_g': 1.0 + nrm(ks[8], (L, HEAD_DIM), 0.1),
        'conv_w': nrm(ks[9], (L, CONV_W, D_LRU), CONV_W ** -0.5),
        'conv_b': nrm(ks[10], (L, D_LRU), 0.02),
        'lru_wa': nrm(ks[11], (L, 2, N_LRU_BLOCKS, LRU_BLOCK, LRU_BLOCK), LRU_BLOCK ** -0.5),
        'lru_ba': nrm(ks[12], (L, 2, D_LRU), 0.02),
        'lru_wx': nrm(ks[13], (L, 2, N_LRU_BLOCKS, LRU_BLOCK, LRU_BLOCK), LRU_BLOCK ** -0.5),
        'lru_bx': nrm(ks[14], (L, 2, D_LRU), 0.02),
        'lru_lambda': jnp.log(a0) - jnp.log1p(-a0),
        'attn_out_g': 1.0 + nrm(ks[17], (L, D_ATT), 0.1),
        'lru_out_g': 1.0 + nrm(ks[18], (L, D_LRU), 0.1),
        'w_out': nrm(ks[19], (L, D_MIX, D_MODEL), D_MIX ** -0.5 * DN_BETA),
        'ln1_g': 1.0 + nrm(ks[20], (L, D_MODEL), 0.1),
        'ln1_b': nrm(ks[21], (L, D_MODEL), 0.02),
        'w_router': nrm(ks[22], (L, D_MODEL, N_EXPERTS), D_MODEL ** -0.5),
        'router_bias': nrm(ks[23], (L, N_EXPERTS), 0.01),
        'w_exp_in': nrm(ks[24], (L, N_EXPERTS, D_MODEL, 2 * D_EXPERT), D_MODEL ** -0.5 * DN_BETA),
        'w_exp_out': nrm(ks[25], (L, N_EXPERTS, D_EXPERT, D_MODEL), D_EXPERT ** -0.5 * DN_BETA),
        'w_sh_in': nrm(ks[26], (L, D_MODEL, 2 * D_SHARED), D_MODEL ** -0.5 * DN_BETA),
        'w_sh_out': nrm(ks[27], (L, D_SHARED, D_MODEL), D_SHARED ** -0.5 * DN_BETA),
        'ln2_g': 1.0 + nrm(ks[28], (L, D_MODEL), 0.1),
        'ln2_b': nrm(ks[29], (L, D_MODEL), 0.02),
    }


def reference(x_prompt, x_sample, c_prompt, c_sample, w_ada, b_ada, w_in, q_norm_g, k_norm_g,
              conv_w, conv_b, lru_wa, lru_ba, lru_wx, lru_bx, lru_lambda, attn_out_g, lru_out_g,
              w_out, ln1_g, ln1_b, w_router, router_bias, w_exp_in, w_exp_out, w_sh_in, w_sh_out,
              ln2_g, ln2_b):
    params = {
        'w_ada': w_ada, 'b_ada': b_ada, 'w_in': w_in, 'q_norm_g': q_norm_g, 'k_norm_g': k_norm_g,
        'conv_w': conv_w, 'conv_b': conv_b, 'lru_wa': lru_wa, 'lru_ba': lru_ba, 'lru_wx': lru_wx,
        'lru_bx': lru_bx, 'lru_lambda': lru_lambda, 'attn_out_g': attn_out_g, 'lru_out_g': lru_out_g,
        'w_out': w_out, 'ln1_g': ln1_g, 'ln1_b': ln1_b, 'w_router': w_router, 'router_bias': router_bias,
        'w_exp_in': w_exp_in, 'w_exp_out': w_exp_out, 'w_sh_in': w_sh_in, 'w_sh_out': w_sh_out,
        'ln2_g': ln2_g, 'ln2_b': ln2_b,
    }
    y_prompt = trunk(x_prompt, c_prompt, params)
    y_sample = trunk(x_sample, c_sample, params)
    return (y_prompt, y_sample)
```

```python
import functools

import jax
import jax.numpy as jnp
from jax import lax
from jax.experimental import pallas as pl
from jax.experimental.pallas import tpu as pltpu

F32 = jnp.float32
BF16 = jnp.bfloat16
HIGHEST = lax.Precision.HIGHEST

HEAD_DIM = 64
N_Q_HEADS = 8
N_KV_HEADS = 2
Q_PER_KV = N_Q_HEADS // N_KV_HEADS
D_ATT = N_Q_HEADS * HEAD_DIM
D_KV = N_KV_HEADS * HEAD_DIM
N_LRU_BLOCKS = 8
CONV_W = 4
LRU_C = 8.0
GRID_W = 64
ROPE_THETA = 10000.0
ROPE_FREQS = HEAD_DIM // 4
N_EXPERTS = 64
TOP_K = 6
N_GROUPS = 8
GROUP_SIZE = N_EXPERTS // N_GROUPS
TOPK_GROUPS = 4
ROUTED_SCALE = 2.5
N_MOD = 6
EPS = 1e-6
DEPTH = 1
DN_ALPHA = (2.0 * DEPTH) ** 0.25

LANES = 128
SUBLANES = 8
VMEM_LIMIT_BYTES = 56 * 1024 * 1024


def _tile(n, pref):
    if n <= pref:
        return n
    for t in range(pref, 0, -1):
        if n % t == 0 and t % SUBLANES == 0:
            return t
    return n


def _params(*sem):
    return pltpu.CompilerParams(dimension_semantics=sem, vmem_limit_bytes=VMEM_LIMIT_BYTES)


def _layer_norm(x):
    mu = jnp.mean(x, axis=-1, keepdims=True)
    xc = x - mu
    var = jnp.mean(xc * xc, axis=-1, keepdims=True)
    return xc * lax.rsqrt(var + EPS)


def _rms(x):
    return x * lax.rsqrt(jnp.mean(x * x, axis=-1, keepdims=True) + EPS)


def _ada_kernel(c_ref, w_ref, b_ref, o_ref):
    c = c_ref[...]
    sc = c * jax.nn.sigmoid(c)
    o_ref[...] = jnp.dot(sc, w_ref[...], precision=HIGHEST, preferred_element_type=F32) + b_ref[...]


def _ada(c, w_ada, b_ada):
    B, D = c.shape
    N = w_ada.shape[1]
    tn = _tile(N, 1024)
    return pl.pallas_call(
        _ada_kernel,
        grid=(N // tn,),
        in_specs=[
            pl.BlockSpec((B, D), lambda j: (0, 0)),
            pl.BlockSpec((D, tn), lambda j: (0, j)),
            pl.BlockSpec((1, tn), lambda j: (0, j)),
        ],
        out_specs=pl.BlockSpec((B, tn), lambda j: (0, j)),
        out_shape=jax.ShapeDtypeStruct((B, N), F32),
        compiler_params=_params("arbitrary"),
        name="ada",
    )(c, w_ada, b_ada.reshape(1, N))


def _group_sumsq(t, ones_blk):
    sq = t * t
    hi = sq.astype(BF16)
    lo = (sq - hi.astype(F32)).astype(BF16)
    return (jnp.dot(hi, ones_blk, preferred_element_type=F32)
            + jnp.dot(lo, ones_blk, preferred_element_type=F32))


def _rope(t, cos, sin_signed, first_half):
    fwd = pltpu.roll(t, LANES - ROPE_FREQS, axis=1)
    bwd = pltpu.roll(t, ROPE_FREQS, axis=1)
    return t * cos + jnp.where(first_half, fwd, bwd) * sin_signed


def _in_proj_kernel(x_ref, mod_ref, w_ref, gq_ref, gk_ref, cos_ref, sin_ref, ones_ref,
                    q_ref, kt_ref, v_ref, xl_ref, gl_ref):
    x = x_ref[...]
    shift1 = mod_ref[0:1, :]
    scale1 = mod_ref[1:2, :]
    u = _layer_norm(x) * (1.0 + scale1) + shift1
    proj = jnp.dot(u.astype(BF16), w_ref[...], preferred_element_type=F32)

    cos = cos_ref[...]
    sin_signed = sin_ref[...]
    lane = lax.broadcasted_iota(jnp.int32, cos.shape, 1)
    first_half = (lane % (2 * ROPE_FREQS)) < ROPE_FREQS
    ones2 = ones_ref[...]
    ones1 = ones2[0:LANES, 0:LANES]

    def norm_rope(t, g, ones_blk):
        ss = _group_sumsq(t, ones_blk)
        tn = t * lax.rsqrt(ss * (1.0 / HEAD_DIM) + EPS) * g
        chunks = [_rope(tn[:, c * LANES:(c + 1) * LANES], cos, sin_signed, first_half)
                  for c in range(t.shape[1] // LANES)]
        return chunks

    w2 = 2 * LANES
    for half in range(D_ATT // w2):
        t = proj[:, half * w2:(half + 1) * w2]
        chunks = norm_rope(t, gq_ref[:, half * w2:(half + 1) * w2], ones2)
        for c, ch in enumerate(chunks):
            lo = half * w2 + c * LANES
            q_ref[:, lo:lo + LANES] = ch.astype(BF16)
    k = proj[:, D_ATT:D_ATT + D_KV]
    (kr,) = norm_rope(k, gk_ref[...], ones1)
    kt_ref[...] = kr.T.astype(BF16)
    v_ref[...] = proj[:, D_ATT + D_KV:D_ATT + 2 * D_KV].astype(BF16)
    o = D_ATT + 2 * D_KV
    d_lru = xl_ref.shape[-1]
    xl_ref[...] = proj[:, o:o + d_lru]
    gl_ref[...] = proj[:, o + d_lru:o + 2 * d_lru]


def _in_proj(x, mod, w_in_bf, gq, gk, cos_t, sin_t, ones_blk, d_lru):
    B, S, D = x.shape
    tm = _tile(S, 512)
    d_in = w_in_bf.shape[1]
    full = lambda shape: pl.BlockSpec(shape, lambda b, i: tuple(0 for _ in shape))
    return pl.pallas_call(
        _in_proj_kernel,
        grid=(B, S // tm),
        in_specs=[
            pl.BlockSpec((None, tm, D), lambda b, i: (b, i, 0)),
            pl.BlockSpec((None, N_MOD, D), lambda b, i: (b, 0, 0)),
            full((D, d_in)),
            full((1, D_ATT)),
            full((1, D_KV)),
            pl.BlockSpec((tm, LANES), lambda b, i: (i, 0)),
            pl.BlockSpec((tm, LANES), lambda b, i: (i, 0)),
            full((2 * LANES, 2 * LANES)),
        ],
        out_specs=[
            pl.BlockSpec((None, tm, D_ATT), lambda b, i: (b, i, 0)),
            pl.BlockSpec((None, D_KV, tm), lambda b, i: (b, 0, i)),
            pl.BlockSpec((None, tm, D_KV), lambda b, i: (b, i, 0)),
            pl.BlockSpec((None, tm, d_lru), lambda b, i: (b, i, 0)),
            pl.BlockSpec((None, tm, d_lru), lambda b, i: (b, i, 0)),
        ],
        out_shape=[
            jax.ShapeDtypeStruct((B, S, D_ATT), BF16),
            jax.ShapeDtypeStruct((B, D_KV, S), BF16),
            jax.ShapeDtypeStruct((B, S, D_KV), BF16),
            jax.ShapeDtypeStruct((B, S, d_lru), F32),
            jax.ShapeDtypeStruct((B, S, d_lru), F32),
        ],
        compiler_params=_params("parallel", "parallel"),
        name="in_proj",
    )(x, mod, w_in_bf, gq, gk, cos_t, sin_t, ones_blk)


def _attn_kernel(q_ref, kt_ref, v_ref, o_ref):
    for j in range(N_KV_HEADS):
        kt = kt_ref[j * HEAD_DIM:(j + 1) * HEAD_DIM, :]
        vj = v_ref[:, j * HEAD_DIM:(j + 1) * HEAD_DIM]
        for g in range(Q_PER_KV):
            h = j * Q_PER_KV + g
            qh = q_ref[:, h * HEAD_DIM:(h + 1) * HEAD_DIM]
            s = jnp.dot(qh, kt, preferred_element_type=F32)
            m = jnp.max(s, axis=-1, keepdims=True)
            p = jnp.exp(s - m)
            l = jnp.sum(p, axis=-1, keepdims=True)
            o = jnp.dot(p.astype(BF16), vj, preferred_element_type=F32)
            o_ref[:, h * HEAD_DIM:(h + 1) * HEAD_DIM] = (o / l).astype(BF16)


def _attention(q, kt, v):
    B, S, _ = q.shape
    tq = _tile(S, 256)
    return pl.pallas_call(
        _attn_kernel,
        grid=(B, S // tq),
        in_specs=[
            pl.BlockSpec((None, tq, D_ATT), lambda b, i: (b, i, 0)),
            pl.BlockSpec((None, D_KV, S), lambda b, i: (b, 0, 0)),
            pl.BlockSpec((None, S, D_KV), lambda b, i: (b, 0, 0)),
        ],
        out_specs=pl.BlockSpec((None, tq, D_ATT), lambda b, i: (b, i, 0)),
        out_shape=jax.ShapeDtypeStruct((B, S, D_ATT), BF16),
        compiler_params=_params("parallel", "parallel"),
        name="attn",
    )(q, kt, v)


def _group_scan(a, b, reverse):
    sub = lax.broadcasted_iota(jnp.int32, a.shape, 1)
    for k in (1, 2, 4):
        if reverse:
            shift, valid = SUBLANES - k, sub < SUBLANES - k
        else:
            shift, valid = k, sub >= k
        a_n = jnp.where(valid, pltpu.roll(a, shift, axis=1), 1.0)
        b_n = jnp.where(valid, pltpu.roll(b, shift, axis=1), 0.0)
        b = a * b_n + b
        a = a * a_n
    return a, b


def _lru_kernel(xl_ref, gl_ref, cw_ref, cb_ref, w_ref, bias_ref, lam_ref, o_ref,
                xc_ref, af_ref, bf_ref, ab_ref, bb_ref, *, chunk):
    S, C = xl_ref.shape
    G = S // SUBLANES
    xl = xl_ref[...]
    row = lax.broadcasted_iota(jnp.int32, (S, 1), 0)
    xm2 = jnp.where(row >= 2, pltpu.roll(xl, 2, axis=0), 0.0)
    xm1 = jnp.where(row >= 1, pltpu.roll(xl, 1, axis=0), 0.0)
    xp1 = jnp.where(row < S - 1, pltpu.roll(xl, S - 1, axis=0), 0.0)
    xc_ref[...] = (xm2 * cw_ref[0:1, :] + xm1 * cw_ref[1:2, :] + xl * cw_ref[2:3, :]
                   + xp1 * cw_ref[3:4, :] + cb_ref[...])

    lam = lam_ref[...]
    nlam = -lam
    softplus = jnp.maximum(nlam, 0.0) + jnp.log(1.0 + jnp.exp(-jnp.abs(nlam)))
    cg = chunk // SUBLANES

    def gates(ci, carry):
        r0 = pl.multiple_of(ci * chunk, chunk)
        g0 = pl.multiple_of(ci * cg, cg)
        xc = xc_ref[pl.ds(r0, chunk), :]
        g = jnp.dot(xc.astype(BF16), w_ref[...], preferred_element_type=F32) + bias_ref[...]
        for d, (a_ref, b_ref) in enumerate(((af_ref, bf_ref), (ab_ref, bb_ref))):
            r = jax.nn.sigmoid(g[:, (2 * d) * C:(2 * d + 1) * C])
            i = jax.nn.sigmoid(g[:, (2 * d + 1) * C:(2 * d + 2) * C])
            log_a = (-LRU_C) * r * softplus[d:d + 1, :]
            a = jnp.exp(log_a)
            gain = jnp.sqrt(1.0 - a * a)
            bt = gain * i * xc
            a3, b3 = _group_scan(a.reshape(cg, SUBLANES, C), bt.reshape(cg, SUBLANES, C), d == 1)
            a_ref[pl.ds(g0, cg), :, :] = a3
            b_ref[pl.ds(g0, cg), :, :] = b3
        return carry

    lax.fori_loop(0, S // chunk, gates, 0)

    def step(gi, carry):
        hf, hb = carry
        hf_new = af_ref[gi] * hf + bf_ref[gi]
        bf_ref[gi] = hf_new
        gb = G - 1 - gi
        hb_new = ab_ref[gb] * hb + bb_ref[gb]
        bb_ref[gb] = hb_new
        return hf_new[SUBLANES - 1:SUBLANES, :], hb_new[0:1, :]

    zero = jnp.zeros((1, C), F32)
    lax.fori_loop(0, G, step, (zero, zero), unroll=4)

    def emit(ci, carry):
        r0 = pl.multiple_of(ci * chunk, chunk)
        g0 = pl.multiple_of(ci * cg, cg)
        h = (bf_ref[pl.ds(g0, cg), :, :] + bb_ref[pl.ds(g0, cg), :, :]).reshape(chunk, C)
        gl = gl_ref[pl.ds(r0, chunk), :]
        o_ref[pl.ds(r0, chunk), :] = (jax.nn.gelu(gl) * h).astype(BF16)
        return carry

    lax.fori_loop(0, S // chunk, emit, 0)


def _lru(xl, gl, conv_w, conv_b, w_gate, b_gate, lam):
    B, S, d_lru = xl.shape
    nh, C, _ = w_gate.shape
    chunk = _tile(S, 512)
    G = S // SUBLANES
    col = lambda b, h: (b, 0, h)
    return pl.pallas_call(
        functools.partial(_lru_kernel, chunk=chunk),
        grid=(B, nh),
        in_specs=[
            pl.BlockSpec((None, S, C), col),
            pl.BlockSpec((None, S, C), col),
            pl.BlockSpec((CONV_W, C), lambda b, h: (0, h)),
            pl.BlockSpec((1, C), lambda b, h: (0, h)),
            pl.BlockSpec((None, C, 4 * C), lambda b, h: (h, 0, 0)),
            pl.BlockSpec((None, 1, 4 * C), lambda b, h: (h, 0, 0)),
            pl.BlockSpec((2, C), lambda b, h: (0, h)),
        ],
        out_specs=pl.BlockSpec((None, S, C), col),
        out_shape=jax.ShapeDtypeStruct((B, S, d_lru), BF16),
        scratch_shapes=[pltpu.VMEM((S, C), F32)] + [pltpu.VMEM((G, SUBLANES, C), F32)] * 4,
        compiler_params=_params("parallel", "parallel"),
        name="lru",
    )(xl, gl, conv_w, conv_b, w_gate, b_gate, lam)


def _out_proj_kernel(att_ref, lru_ref, x_ref, mod_ref, ga_ref, gr_ref, w_ref, g1_ref, b1_ref, wr_ref,
                     x1_ref, u2_ref, lg_ref):
    d_att = att_ref.shape[-1]
    an = _rms(att_ref[...].astype(F32)) * ga_ref[...]
    rn = _rms(lru_ref[...].astype(F32)) * gr_ref[...]
    mix = (jnp.dot(an.astype(BF16), w_ref[0:d_att, :], preferred_element_type=F32)
           + jnp.dot(rn.astype(BF16), w_ref[d_att:, :], preferred_element_type=F32))
    gate1 = mod_ref[2:3, :]
    shift2 = mod_ref[3:4, :]
    scale2 = mod_ref[4:5, :]
    x1 = _layer_norm(DN_ALPHA * x_ref[...] + gate1 * mix) * g1_ref[...] + b1_ref[...]
    x1_ref[...] = x1
    u2 = _layer_norm(x1) * (1.0 + scale2) + shift2
    u2_ref[...] = u2.astype(BF16)
    lg_ref[...] = lax.dot_general(wr_ref[...], u2, (((1,), (1,)), ((), ())),
                                  precision=HIGHEST, preferred_element_type=F32)


def _out_proj(att, lru, x, mod, ga, gr, w_out_bf, g1, b1, wr_t):
    B, S, D = x.shape
    tm = _tile(S, 512)
    d_att, d_lru = att.shape[-1], lru.shape[-1]
    full = lambda shape: pl.BlockSpec(shape, lambda b, i: tuple(0 for _ in shape))
    row = lambda w: pl.BlockSpec((None, tm, w), lambda b, i: (b, i, 0))
    return pl.pallas_call(
        _out_proj_kernel,
        grid=(B, S // tm),
        in_specs=[
            row(d_att), row(d_lru), row(D),
            pl.BlockSpec((None, N_MOD, D), lambda b, i: (b, 0, 0)),
            full((1, d_att)), full((1, d_lru)), full((d_att + d_lru, D)),
            full((1, D)), full((1, D)), full((N_EXPERTS, D)),
        ],
        out_specs=[row(D), row(D), pl.BlockSpec((None, N_EXPERTS, tm), lambda b, i: (b, 0, i))],
        out_shape=[
            jax.ShapeDtypeStruct((B, S, D), F32),
            jax.ShapeDtypeStruct((B, S, D), BF16),
            jax.ShapeDtypeStruct((B, N_EXPERTS, S), F32),
        ],
        compiler_params=_params("parallel", "parallel"),
        name="out_proj",
    )(att, lru, x, mod, ga, gr, w_out_bf, g1, b1, wr_t)


def _route_kernel(lg_ref, bias_ref, cw_ref):
    E, T = lg_ref.shape
    s = jax.nn.sigmoid(lg_ref[...])
    choice = s + bias_ref[...]
    neg = -jnp.inf

    c3 = choice.reshape(N_GROUPS, GROUP_SIZE, T)
    mem = lax.broadcasted_iota(jnp.int32, c3.shape, 1)
    m1 = jnp.max(c3, axis=1, keepdims=True)
    first = jnp.min(jnp.where(c3 == m1, mem, GROUP_SIZE), axis=1, keepdims=True)
    m2 = jnp.max(jnp.where(mem == first, neg, c3), axis=1, keepdims=True)
    gs = (m1 + m2).reshape(N_GROUPS, T)

    gi = lax.broadcasted_iota(jnp.int32, gs.shape, 0)
    rank = jnp.zeros(gs.shape, jnp.int32)
    for g2 in range(N_GROUPS):
        other = gs[g2:g2 + 1, :]
        beats = (other > gs) | ((other == gs) & (gi > g2))
        rank = rank + beats.astype(jnp.int32)
    gmask = (rank < TOPK_GROUPS).astype(F32).reshape(N_GROUPS, 1, T)
    emask = jnp.broadcast_to(gmask, (N_GROUPS, GROUP_SIZE, T)).reshape(E, T) > 0.5
    cur = jnp.where(emask, choice, neg)

    ei = lax.broadcasted_iota(jnp.int32, (E, T), 0)
    sel = jnp.zeros((E, T), F32)
    for _ in range(TOP_K):
        m = jnp.max(cur, axis=0, keepdims=True)
        first_e = jnp.min(jnp.where(cur == m, ei, E), axis=0, keepdims=True)
        hit = ei == first_e
        sel = jnp.where(hit, 1.0, sel)
        cur = jnp.where(hit, neg, cur)

    w = s * sel
    cw = w / jnp.sum(w, axis=0, keepdims=True) * ROUTED_SCALE
    cw_pad = jnp.concatenate([cw, jnp.zeros((LANES - E, T), F32)], axis=0)
    cw_ref[...] = cw_pad.T


def _route(logits_t, router_bias):
    B, E, S = logits_t.shape
    tr = _tile(S, 512)
    return pl.pallas_call(
        _route_kernel,
        grid=(B, S // tr),
        in_specs=[
            pl.BlockSpec((None, E, tr), lambda b, i: (b, 0, i)),
            pl.BlockSpec((E, 1), lambda b, i: (0, 0)),
        ],
        out_specs=pl.BlockSpec((None, tr, LANES), lambda b, i: (b, i, 0)),
        out_shape=jax.ShapeDtypeStruct((B, S, LANES), F32),
        compiler_params=_params("parallel", "parallel"),
        name="route",
    )(logits_t, router_bias.reshape(E, 1))


def _swiglu_hidden(u, w_in):
    gu = jnp.dot(u, w_in, preferred_element_type=F32)
    d = gu.shape[-1] // 2
    g, up = gu[:, :d], gu[:, d:]
    return g * jax.nn.sigmoid(g) * up


def _moe_kernel(u_ref, x1_ref, cw_ref, mod_ref, wsi_ref, wso_ref, wei_ref, weo_ref, g2_ref, b2_ref,
                o_ref, acc_ref):
    e = pl.program_id(1)
    u = u_ref[...]

    @pl.when(e == 0)
    def _():
        hs = _swiglu_hidden(u, wsi_ref[...])
        acc_ref[...] = jnp.dot(hs.astype(BF16), wso_ref[...], preferred_element_type=F32)

    cw = cw_ref[...]
    lane = lax.broadcasted_iota(jnp.int32, cw.shape, 1)
    wcol = jnp.sum(jnp.where(lane == e, cw, 0.0), axis=1, keepdims=True)
    h = _swiglu_hidden(u, wei_ref[...]) * wcol
    acc_ref[...] += jnp.dot(h.astype(BF16), weo_ref[...], preferred_element_type=F32)

    @pl.when(e == pl.num_programs(1) - 1)
    def _():
        gate2 = mod_ref[5:6, :]
        y = DN_ALPHA * x1_ref[...] + gate2 * acc_ref[...]
        o_ref[...] = _layer_norm(y) * g2_ref[...] + b2_ref[...]


def _moe(u2, x1, cw_t, mod, wsi, wso, wei, weo, g2, b2, S):
    T, D = u2.shape
    tm = _tile(S, 1024)
    per_seq = S // tm
    E, _, d2 = wei.shape
    de = weo.shape[1]
    full = lambda shape: pl.BlockSpec(shape, lambda i, e: tuple(0 for _ in shape))
    return pl.pallas_call(
        _moe_kernel,
        grid=(T // tm, E),
        in_specs=[
            pl.BlockSpec((tm, D), lambda i, e: (i, 0)),
            pl.BlockSpec((tm, D), lambda i, e: (i, 0)),
            pl.BlockSpec((tm, LANES), lambda i, e: (i, 0)),
            pl.BlockSpec((None, N_MOD, D), lambda i, e: (i // per_seq, 0, 0)),
            full(wsi.shape), full(wso.shape),
            pl.BlockSpec((None, D, d2), lambda i, e: (e, 0, 0)),
            pl.BlockSpec((None, de, D), lambda i, e: (e, 0, 0)),
            full((1, D)), full((1, D)),
        ],
        out_specs=pl.BlockSpec((tm, D), lambda i, e: (i, 0)),
        out_shape=jax.ShapeDtypeStruct((T, D), F32),
        scratch_shapes=[pltpu.VMEM((tm, D), F32)],
        compiler_params=_params("parallel", "arbitrary"),
        name="moe",
    )(u2, x1, cw_t, mod, wsi, wso, wei, weo, g2, b2)


def _rope_tables(S):
    rows = S // GRID_W
    row_idx = jnp.repeat(jnp.arange(rows, dtype=F32), GRID_W)
    col_idx = jnp.tile(jnp.arange(GRID_W, dtype=F32), rows)
    inv_freq = ROPE_THETA ** (-jnp.arange(ROPE_FREQS, dtype=F32) / ROPE_FREQS)
    ang = jnp.stack([row_idx[:, None] * inv_freq, col_idx[:, None] * inv_freq], axis=1)
    cos, sin = jnp.cos(ang), jnp.sin(ang)
    cos_h = jnp.stack([cos, cos], axis=2).reshape(S, HEAD_DIM)
    sin_h = jnp.stack([-sin, sin], axis=2).reshape(S, HEAD_DIM)
    reps = LANES // HEAD_DIM
    return jnp.tile(cos_h, (1, reps)), jnp.tile(sin_h, (1, reps))


def _gate_weights(lru_wa, lru_ba, lru_wx, lru_bx, C):
    nb, blk, _ = lru_wa.shape[1:]
    d_lru = nb * blk
    per = C // blk

    def dense(w):
        w = w.reshape(d_lru // C, per, blk, blk)
        eye = jnp.eye(per, dtype=w.dtype)
        return jnp.einsum("hpcd,pq->hpcqd", w, eye).reshape(d_lru // C, C, C)

    w = jnp.concatenate([dense(lru_wa[0]), dense(lru_wx[0]), dense(lru_wa[1]), dense(lru_wx[1])], axis=-1)
    halves = lambda v: v.reshape(d_lru // C, 1, C)
    b = jnp.concatenate([halves(lru_ba[0]), halves(lru_bx[0]), halves(lru_ba[1]), halves(lru_bx[1])], axis=-1)
    return w.astype(BF16), b


def _encoder(x, c, P):
    B, S, D = x.shape
    mod = _ada(c, P["w_ada"], P["b_ada"]).reshape(B, N_MOD, D)
    cos_t, sin_t = _rope_tables(S)
    q, kt, v, xl, gl = _in_proj(x, mod, P["w_in"], P["gq"], P["gk"], cos_t, sin_t, P["ones_blk"], P["d_lru"])
    att = _attention(q, kt, v)
    lru = _lru(xl, gl, P["conv_w"], P["conv_b"], P["w_gate"], P["b_gate"], P["lam"])
    x1, u2, logits_t = _out_proj(att, lru, x, mod, P["ga"], P["gr"], P["w_out"], P["ln1_g"], P["ln1_b"], P["wr_t"])
    cw_t = _route(logits_t, P["router_bias"])
    y = _moe(u2.reshape(B * S, D), x1.reshape(B * S, D), cw_t.reshape(B * S, LANES), mod,
             P["w_sh_in"], P["w_sh_out"], P["w_exp_in"], P["w_exp_out"], P["ln2_g"], P["ln2_b"], S)
    return y.reshape(B, S, D)


def kernel(x_prompt, x_sample, c_prompt, c_sample, w_ada, b_ada, w_in, q_norm_g, k_norm_g, conv_w, conv_b, lru_wa, lru_ba, lru_wx, lru_bx, lru_lambda, attn_out_g, lru_out_g, w_out, ln1_g, ln1_b, w_router, router_bias, w_exp_in, w_exp_out, w_sh_in, w_sh_out, ln2_g, ln2_b):
    l = 0
    D = x_prompt.shape[-1]
    d_lru = lru_out_g.shape[-1]
    lru_half = 2 * LANES
    w_gate, b_gate = _gate_weights(lru_wa[l], lru_ba[l], lru_wx[l], lru_bx[l], lru_half)
    idx = jnp.arange(2 * LANES)
    ones_blk = (idx[:, None] // HEAD_DIM == idx[None, :] // HEAD_DIM).astype(BF16)
    P = {
        "w_ada": w_ada[l], "b_ada": b_ada[l],
        "w_in": w_in[l].astype(BF16),
        "gq": (jnp.tile(q_norm_g[l], N_Q_HEADS) * HEAD_DIM ** -0.5).reshape(1, D_ATT),
        "gk": jnp.tile(k_norm_g[l], N_KV_HEADS).reshape(1, D_KV),
        "ones_blk": ones_blk, "d_lru": d_lru,
        "conv_w": conv_w[l], "conv_b": conv_b[l].reshape(1, d_lru),
        "w_gate": w_gate, "b_gate": b_gate, "lam": lru_lambda[l],
        "ga": attn_out_g[l].reshape(1, D_ATT), "gr": lru_out_g[l].reshape(1, d_lru),
        "w_out": w_out[l].astype(BF16),
        "ln1_g": ln1_g[l].reshape(1, D), "ln1_b": ln1_b[l].reshape(1, D),
        "wr_t": w_router[l].T, "router_bias": router_bias[l],
        "w_exp_in": w_exp_in[l].astype(BF16), "w_exp_out": w_exp_out[l].astype(BF16),
        "w_sh_in": w_sh_in[l].astype(BF16), "w_sh_out": w_sh_out[l].astype(BF16),
        "ln2_g": ln2_g[l].reshape(1, D), "ln2_b": ln2_b[l].reshape(1, D),
    }
    return (_encoder(x_prompt, c_prompt, P), _encoder(x_sample, c_sample, P))
```

```python
import functools

import jax
import jax.numpy as jnp
from jax import lax
from jax.experimental import pallas as pl
from jax.experimental.pallas import tpu as pltpu
from jax.experimental.pallas import tpu_sc as plsc

F32 = jnp.float32
BF16 = jnp.bfloat16
U32 = jnp.uint32
HIGHEST = lax.Precision.HIGHEST

HEAD_DIM = 64
N_Q_HEADS = 8
N_KV_HEADS = 2
Q_PER_KV = N_Q_HEADS // N_KV_HEADS
D_ATT = N_Q_HEADS * HEAD_DIM
D_KV = N_KV_HEADS * HEAD_DIM
N_LRU_BLOCKS = 8
CONV_W = 4
LRU_C = 8.0
GRID_W = 64
ROPE_THETA = 10000.0
ROPE_FREQS = HEAD_DIM // 4
N_EXPERTS = 64
TOP_K = 6
N_GROUPS = 8
GROUP_SIZE = N_EXPERTS // N_GROUPS
TOPK_GROUPS = 4
ROUTED_SCALE = 2.5
N_MOD = 6
EPS = 1e-6
DEPTH = 1
DN_ALPHA = (2.0 * DEPTH) ** 0.25

LANES = 128
SUBLANES = 8
VMEM_LIMIT_BYTES = 56 * 1024 * 1024

ROW_SPLIT = 2
SC_WINDOW = 128
ROUTE_TILE = 512
EXPERT_BLOCK = 512


def _tile(n, pref):
    if n <= pref:
        return n
    for t in range(pref, 0, -1):
        if n % t == 0 and t % SUBLANES == 0:
            return t
    return n


def _params(*sem):
    return pltpu.CompilerParams(dimension_semantics=sem, vmem_limit_bytes=VMEM_LIMIT_BYTES)


def _layer_norm(x):
    mu = jnp.mean(x, axis=-1, keepdims=True)
    xc = x - mu
    var = jnp.mean(xc * xc, axis=-1, keepdims=True)
    return xc * lax.rsqrt(var + EPS)


def _rms(x):
    return x * lax.rsqrt(jnp.mean(x * x, axis=-1, keepdims=True) + EPS)


def _pack_pair(a, b):
    ah = lax.bitcast_convert_type(a.astype(BF16).astype(F32), U32)
    bh = lax.bitcast_convert_type(b.astype(BF16).astype(F32), U32)
    return ah | (bh >> 16)


def _unpack_pair(w):
    a = lax.bitcast_convert_type(w & jnp.uint32(0xFFFF0000), F32)
    b = lax.bitcast_convert_type(w << 16, F32)
    return a, b


def _store_packed(ref, v):
    q = v.shape[-1] // (2 * ROW_SPLIT)
    for h in range(ROW_SPLIT):
        ref[h] = _pack_pair(v[:, 2 * h * q:(2 * h + 1) * q], v[:, (2 * h + 1) * q:(2 * h + 2) * q])


def _load_packed(ref):
    parts = []
    for h in range(ROW_SPLIT):
        parts.extend(_unpack_pair(ref[h]))
    return jnp.concatenate(parts, axis=-1)


def _ada_kernel(c_ref, w_ref, b_ref, o_ref):
    c = c_ref[...]
    sc = c * jax.nn.sigmoid(c)
    o_ref[...] = jnp.dot(sc, w_ref[...], precision=HIGHEST, preferred_element_type=F32) + b_ref[...]


def _ada(c, w_ada, b_ada):
    B, D = c.shape
    N = w_ada.shape[1]
    tn = _tile(N, 1024)
    return pl.pallas_call(
        _ada_kernel,
        grid=(N // tn,),
        in_specs=[
            pl.BlockSpec((B, D), lambda j: (0, 0)),
            pl.BlockSpec((D, tn), lambda j: (0, j)),
            pl.BlockSpec((1, tn), lambda j: (0, j)),
        ],
        out_specs=pl.BlockSpec((B, tn), lambda j: (0, j)),
        out_shape=jax.ShapeDtypeStruct((B, N), F32),
        compiler_params=_params("arbitrary"),
        name="ada",
    )(c, w_ada, b_ada.reshape(1, N))


def _group_sumsq(t, ones_blk):
    sq = t * t
    hi = sq.astype(BF16)
    lo = (sq - hi.astype(F32)).astype(BF16)
    return (jnp.dot(hi, ones_blk, preferred_element_type=F32)
            + jnp.dot(lo, ones_blk, preferred_element_type=F32))


def _rope(t, cos, sin_signed, first_half):
    fwd = pltpu.roll(t, LANES - ROPE_FREQS, axis=1)
    bwd = pltpu.roll(t, ROPE_FREQS, axis=1)
    return t * cos + jnp.where(first_half, fwd, bwd) * sin_signed


def _in_proj_kernel(x_ref, mod_ref, w_ref, gq_ref, gk_ref, cos_ref, sin_ref, ones_ref,
                    q_ref, kt_ref, v_ref, xl_ref, gl_ref):
    x = x_ref[...]
    shift1 = mod_ref[0:1, :]
    scale1 = mod_ref[1:2, :]
    u = _layer_norm(x) * (1.0 + scale1) + shift1
    proj = jnp.dot(u.astype(BF16), w_ref[...], preferred_element_type=F32)

    cos = cos_ref[...]
    sin_signed = sin_ref[...]
    lane = lax.broadcasted_iota(jnp.int32, cos.shape, 1)
    first_half = (lane % (2 * ROPE_FREQS)) < ROPE_FREQS
    ones2 = ones_ref[...]
    ones1 = ones2[0:LANES, 0:LANES]

    def norm_rope(t, g, ones_blk):
        ss = _group_sumsq(t, ones_blk)
        tn = t * lax.rsqrt(ss * (1.0 / HEAD_DIM) + EPS) * g
        chunks = [_rope(tn[:, c * LANES:(c + 1) * LANES], cos, sin_signed, first_half)
                  for c in range(t.shape[1] // LANES)]
        return chunks

    w2 = 2 * LANES
    for half in range(D_ATT // w2):
        t = proj[:, half * w2:(half + 1) * w2]
        chunks = norm_rope(t, gq_ref[:, half * w2:(half + 1) * w2], ones2)
        for c, ch in enumerate(chunks):
            lo = half * w2 + c * LANES
            q_ref[:, lo:lo + LANES] = ch.astype(BF16)
    k = proj[:, D_ATT:D_ATT + D_KV]
    (kr,) = norm_rope(k, gk_ref[...], ones1)
    kt_ref[...] = kr.T.astype(BF16)
    v_ref[...] = proj[:, D_ATT + D_KV:D_ATT + 2 * D_KV].astype(BF16)
    o = D_ATT + 2 * D_KV
    d_lru = xl_ref.shape[-1]
    xl_ref[...] = proj[:, o:o + d_lru]
    gl_ref[...] = proj[:, o + d_lru:o + 2 * d_lru]


def _in_proj(x, mod, w_in_bf, gq, gk, cos_t, sin_t, ones_blk, d_lru):
    B, S, D = x.shape
    tm = _tile(S, 512)
    d_in = w_in_bf.shape[1]
    full = lambda shape: pl.BlockSpec(shape, lambda b, i: tuple(0 for _ in shape))
    return pl.pallas_call(
        _in_proj_kernel,
        grid=(B, S // tm),
        in_specs=[
            pl.BlockSpec((None, tm, D), lambda b, i: (b, i, 0)),
            pl.BlockSpec((None, N_MOD, D), lambda b, i: (b, 0, 0)),
            full((D, d_in)),
            full((1, D_ATT)),
            full((1, D_KV)),
            pl.BlockSpec((tm, LANES), lambda b, i: (i, 0)),
            pl.BlockSpec((tm, LANES), lambda b, i: (i, 0)),
            full((2 * LANES, 2 * LANES)),
        ],
        out_specs=[
            pl.BlockSpec((None, tm, D_ATT), lambda b, i: (b, i, 0)),
            pl.BlockSpec((None, D_KV, tm), lambda b, i: (b, 0, i)),
            pl.BlockSpec((None, tm, D_KV), lambda b, i: (b, i, 0)),
            pl.BlockSpec((None, tm, d_lru), lambda b, i: (b, i, 0)),
            pl.BlockSpec((None, tm, d_lru), lambda b, i: (b, i, 0)),
        ],
        out_shape=[
            jax.ShapeDtypeStruct((B, S, D_ATT), BF16),
            jax.ShapeDtypeStruct((B, D_KV, S), BF16),
            jax.ShapeDtypeStruct((B, S, D_KV), BF16),
            jax.ShapeDtypeStruct((B, S, d_lru), F32),
            jax.ShapeDtypeStruct((B, S, d_lru), F32),
        ],
        compiler_params=_params("parallel", "parallel"),
        name="in_proj",
    )(x, mod, w_in_bf, gq, gk, cos_t, sin_t, ones_blk)


def _attn_kernel(q_ref, kt_ref, v_ref, o_ref):
    for j in range(N_KV_HEADS):
        kt = kt_ref[j * HEAD_DIM:(j + 1) * HEAD_DIM, :]
        vj = v_ref[:, j * HEAD_DIM:(j + 1) * HEAD_DIM]
        for g in range(Q_PER_KV):
            h = j * Q_PER_KV + g
            qh = q_ref[:, h * HEAD_DIM:(h + 1) * HEAD_DIM]
            s = jnp.dot(qh, kt, preferred_element_type=F32)
            m = jnp.max(s, axis=-1, keepdims=True)
            p = jnp.exp(s - m)
            l = jnp.sum(p, axis=-1, keepdims=True)
            o = jnp.dot(p.astype(BF16), vj, preferred_element_type=F32)
            o_ref[:, h * HEAD_DIM:(h + 1) * HEAD_DIM] = (o / l).astype(BF16)


def _attention(q, kt, v):
    B, S, _ = q.shape
    tq = _tile(S, 256)
    return pl.pallas_call(
        _attn_kernel,
        grid=(B, S // tq),
        in_specs=[
            pl.BlockSpec((None, tq, D_ATT), lambda b, i: (b, i, 0)),
            pl.BlockSpec((None, D_KV, S), lambda b, i: (b, 0, 0)),
            pl.BlockSpec((None, S, D_KV), lambda b, i: (b, 0, 0)),
        ],
        out_specs=pl.BlockSpec((None, tq, D_ATT), lambda b, i: (b, i, 0)),
        out_shape=jax.ShapeDtypeStruct((B, S, D_ATT), BF16),
        compiler_params=_params("parallel", "parallel"),
        name="attn",
    )(q, kt, v)


def _group_scan(a, b, reverse):
    sub = lax.broadcasted_iota(jnp.int32, a.shape, 1)
    for k in (1, 2, 4):
        if reverse:
            shift, valid = SUBLANES - k, sub < SUBLANES - k
        else:
            shift, valid = k, sub >= k
        a_n = jnp.where(valid, pltpu.roll(a, shift, axis=1), 1.0)
        b_n = jnp.where(valid, pltpu.roll(b, shift, axis=1), 0.0)
        b = a * b_n + b
        a = a * a_n
    return a, b


def _lru_kernel(xl_ref, gl_ref, cw_ref, cb_ref, w_ref, bias_ref, lam_ref, o_ref,
                xc_ref, af_ref, bf_ref, ab_ref, bb_ref, *, chunk):
    S, C = xl_ref.shape
    G = S // SUBLANES
    xl = xl_ref[...]
    row = lax.broadcasted_iota(jnp.int32, (S, 1), 0)
    xm2 = jnp.where(row >= 2, pltpu.roll(xl, 2, axis=0), 0.0)
    xm1 = jnp.where(row >= 1, pltpu.roll(xl, 1, axis=0), 0.0)
    xp1 = jnp.where(row < S - 1, pltpu.roll(xl, S - 1, axis=0), 0.0)
    xc_ref[...] = (xm2 * cw_ref[0:1, :] + xm1 * cw_ref[1:2, :] + xl * cw_ref[2:3, :]
                   + xp1 * cw_ref[3:4, :] + cb_ref[...])

    lam = lam_ref[...]
    nlam = -lam
    softplus = jnp.maximum(nlam, 0.0) + jnp.log(1.0 + jnp.exp(-jnp.abs(nlam)))
    cg = chunk // SUBLANES

    def gates(ci, carry):
        r0 = pl.multiple_of(ci * chunk, chunk)
        g0 = pl.multiple_of(ci * cg, cg)
        xc = xc_ref[pl.ds(r0, chunk), :]
        g = jnp.dot(xc.astype(BF16), w_ref[...], preferred_element_type=F32) + bias_ref[...]
        for d, (a_ref, b_ref) in enumerate(((af_ref, bf_ref), (ab_ref, bb_ref))):
            r = jax.nn.sigmoid(g[:, (2 * d) * C:(2 * d + 1) * C])
            i = jax.nn.sigmoid(g[:, (2 * d + 1) * C:(2 * d + 2) * C])
            log_a = (-LRU_C) * r * softplus[d:d + 1, :]
            a = jnp.exp(log_a)
            gain = jnp.sqrt(1.0 - a * a)
            bt = gain * i * xc
            a3, b3 = _group_scan(a.reshape(cg, SUBLANES, C), bt.reshape(cg, SUBLANES, C), d == 1)
            a_ref[pl.ds(g0, cg), :, :] = a3
            b_ref[pl.ds(g0, cg), :, :] = b3
        return carry

    lax.fori_loop(0, S // chunk, gates, 0)

    def step(gi, carry):
        hf, hb = carry
        hf_new = af_ref[gi] * hf + bf_ref[gi]
        bf_ref[gi] = hf_new
        gb = G - 1 - gi
        hb_new = ab_ref[gb] * hb + bb_ref[gb]
        bb_ref[gb] = hb_new
        return hf_new[SUBLANES - 1:SUBLANES, :], hb_new[0:1, :]

    zero = jnp.zeros((1, C), F32)
    lax.fori_loop(0, G, step, (zero, zero), unroll=4)

    def emit(ci, carry):
        r0 = pl.multiple_of(ci * chunk, chunk)
        g0 = pl.multiple_of(ci * cg, cg)
        h = (bf_ref[pl.ds(g0, cg), :, :] + bb_ref[pl.ds(g0, cg), :, :]).reshape(chunk, C)
        gl = gl_ref[pl.ds(r0, chunk), :]
        o_ref[pl.ds(r0, chunk), :] = (jax.nn.gelu(gl) * h).astype(BF16)
        return carry

    lax.fori_loop(0, S // chunk, emit, 0)


def _lru(xl, gl, conv_w, conv_b, w_gate, b_gate, lam):
    B, S, d_lru = xl.shape
    nh, C, _ = w_gate.shape
    chunk = _tile(S, 512)
    G = S // SUBLANES
    col = lambda b, h: (b, 0, h)
    return pl.pallas_call(
        functools.partial(_lru_kernel, chunk=chunk),
        grid=(B, nh),
        in_specs=[
            pl.BlockSpec((None, S, C), col),
            pl.BlockSpec((None, S, C), col),
            pl.BlockSpec((CONV_W, C), lambda b, h: (0, h)),
            pl.BlockSpec((1, C), lambda b, h: (0, h)),
            pl.BlockSpec((None, C, 4 * C), lambda b, h: (h, 0, 0)),
            pl.BlockSpec((None, 1, 4 * C), lambda b, h: (h, 0, 0)),
            pl.BlockSpec((2, C), lambda b, h: (0, h)),
        ],
        out_specs=pl.BlockSpec((None, S, C), col),
        out_shape=jax.ShapeDtypeStruct((B, S, d_lru), BF16),
        scratch_shapes=[pltpu.VMEM((S, C), F32)] + [pltpu.VMEM((G, SUBLANES, C), F32)] * 4,
        compiler_params=_params("parallel", "parallel"),
        name="lru",
    )(xl, gl, conv_w, conv_b, w_gate, b_gate, lam)


def _out_proj_kernel(att_ref, lru_ref, x_ref, mod_ref, ga_ref, gr_ref, w_ref, g1_ref, b1_ref, wr_ref,
                     x1_ref, u2_ref, lg_ref):
    d_att = att_ref.shape[-1]
    an = _rms(att_ref[...].astype(F32)) * ga_ref[...]
    rn = _rms(lru_ref[...].astype(F32)) * gr_ref[...]
    mix = (jnp.dot(an.astype(BF16), w_ref[0:d_att, :], preferred_element_type=F32)
           + jnp.dot(rn.astype(BF16), w_ref[d_att:, :], preferred_element_type=F32))
    gate1 = mod_ref[2:3, :]
    shift2 = mod_ref[3:4, :]
    scale2 = mod_ref[4:5, :]
    x1 = _layer_norm(DN_ALPHA * x_ref[...] + gate1 * mix) * g1_ref[...] + b1_ref[...]
    x1_ref[...] = x1
    u2 = _layer_norm(x1) * (1.0 + scale2) + shift2
    _store_packed(u2_ref, u2)
    lg_ref[...] = lax.dot_general(wr_ref[...], u2, (((1,), (1,)), ((), ())),
                                  precision=HIGHEST, preferred_element_type=F32)


def _out_proj(att, lru, x, mod, ga, gr, w_out_bf, g1, b1, wr_t):
    B, S, D = x.shape
    tm = _tile(S, 512)
    d_att, d_lru = att.shape[-1], lru.shape[-1]
    full = lambda shape: pl.BlockSpec(shape, lambda b, i: tuple(0 for _ in shape))
    row = lambda w: pl.BlockSpec((None, tm, w), lambda b, i: (b, i, 0))
    return pl.pallas_call(
        _out_proj_kernel,
        grid=(B, S // tm),
        in_specs=[
            row(d_att), row(d_lru), row(D),
            pl.BlockSpec((None, N_MOD, D), lambda b, i: (b, 0, 0)),
            full((1, d_att)), full((1, d_lru)), full((d_att + d_lru, D)),
            full((1, D)), full((1, D)), full((N_EXPERTS, D)),
        ],
        out_specs=[row(D),
                   pl.BlockSpec((ROW_SPLIT, None, tm, D // (2 * ROW_SPLIT)), lambda b, i: (0, b, i, 0)),
                   pl.BlockSpec((None, N_EXPERTS, tm), lambda b, i: (b, 0, i))],
        out_shape=[
            jax.ShapeDtypeStruct((B, S, D), F32),
            jax.ShapeDtypeStruct((ROW_SPLIT, B, S, D // (2 * ROW_SPLIT)), U32),
            jax.ShapeDtypeStruct((B, N_EXPERTS, S), F32),
        ],
        compiler_params=_params("parallel", "parallel"),
        name="out_proj",
    )(att, lru, x, mod, ga, gr, w_out_bf, g1, b1, wr_t)


def _route_kernel(lg_ref, bias_ref, w_ref, eid_ref, rank_ref, cnt_ref):
    E, T = lg_ref.shape
    s = jax.nn.sigmoid(lg_ref[...])
    choice = s + bias_ref[...]
    neg = -jnp.inf

    c3 = choice.reshape(N_GROUPS, GROUP_SIZE, T)
    mem = lax.broadcasted_iota(jnp.int32, c3.shape, 1)
    m1 = jnp.max(c3, axis=1, keepdims=True)
    first = jnp.min(jnp.where(c3 == m1, mem, GROUP_SIZE), axis=1, keepdims=True)
    m2 = jnp.max(jnp.where(mem == first, neg, c3), axis=1, keepdims=True)
    gs = (m1 + m2).reshape(N_GROUPS, T)

    gi = lax.broadcasted_iota(jnp.int32, gs.shape, 0)
    rank = jnp.zeros(gs.shape, jnp.int32)
    for g2 in range(N_GROUPS):
        other = gs[g2:g2 + 1, :]
        beats = (other > gs) | ((other == gs) & (gi > g2))
        rank = rank + beats.astype(jnp.int32)
    gmask = (rank < TOPK_GROUPS).astype(F32).reshape(N_GROUPS, 1, T)
    emask = jnp.broadcast_to(gmask, (N_GROUPS, GROUP_SIZE, T)).reshape(E, T) > 0.5
    cur = jnp.where(emask, choice, neg)

    ei = lax.broadcasted_iota(jnp.int32, (E, T), 0)
    sel = jnp.zeros((E, T), F32)
    eids, ws = [], []
    for _ in range(TOP_K):
        m = jnp.max(cur, axis=0, keepdims=True)
        first_e = jnp.min(jnp.where(cur == m, ei, E), axis=0, keepdims=True)
        hit = ei == first_e
        eids.append(first_e)
        ws.append(jnp.sum(jnp.where(hit, s, 0.0), axis=0, keepdims=True))
        sel = jnp.where(hit, 1.0, sel)
        cur = jnp.where(hit, neg, cur)

    denom = ws[0]
    for w in ws[1:]:
        denom = denom + w
    wk = [w / denom * ROUTED_SCALE for w in ws]
    w_ref[...] = jnp.concatenate(wk + [jnp.zeros((LANES - TOP_K, T), F32)], axis=0).T

    sel_bf = sel.astype(BF16)
    t_row = lax.broadcasted_iota(jnp.int32, (T, T), 0)
    t_col = lax.broadcasted_iota(jnp.int32, (T, T), 1)
    before = (t_row < t_col).astype(BF16)
    cum = jnp.dot(sel_bf, before, preferred_element_type=F32)
    ranks = [jnp.sum(jnp.where(ei == e, cum, 0.0), axis=0, keepdims=True).astype(jnp.int32) for e in eids]
    pad = [jnp.zeros((SUBLANES - TOP_K, T), jnp.int32)]
    eid_ref[...] = jnp.concatenate(eids + pad, axis=0)
    rank_ref[...] = jnp.concatenate(ranks + pad, axis=0)
    cnt_ref[...] = lax.dot_general(jnp.ones((SUBLANES, T), BF16), sel_bf, (((1,), (1,)), ((), ())),
                                   preferred_element_type=F32)


def _route(logits_t, router_bias):
    B, E, S = logits_t.shape
    tr = _tile(S, ROUTE_TILE)
    nt = S // tr
    tok = lambda b, i: (b, 0, i)
    return pl.pallas_call(
        _route_kernel,
        grid=(B, nt),
        in_specs=[
            pl.BlockSpec((None, E, tr), tok),
            pl.BlockSpec((E, 1), lambda b, i: (0, 0)),
        ],
        out_specs=[
            pl.BlockSpec((None, tr, LANES), lambda b, i: (b, i, 0)),
            pl.BlockSpec((None, SUBLANES, tr), tok),
            pl.BlockSpec((None, SUBLANES, tr), tok),
            pl.BlockSpec((None, None, SUBLANES, E), lambda b, i: (b, i, 0, 0)),
        ],
        out_shape=[
            jax.ShapeDtypeStruct((B, S, LANES), F32),
            jax.ShapeDtypeStruct((B, SUBLANES, S), jnp.int32),
            jax.ShapeDtypeStruct((B, SUBLANES, S), jnp.int32),
            jax.ShapeDtypeStruct((B, nt, SUBLANES, E), F32),
        ],
        compiler_params=_params("parallel", "parallel"),
        name="route",
    )(logits_t, router_bias.reshape(E, 1))


def _slots_kernel(eid_ref, rank_ref, base_ref, pos_ref):
    E = base_ref.shape[0]
    T = eid_ref.shape[-1]
    ei = lax.broadcasted_iota(jnp.int32, (E, T), 0)
    base = base_ref[...]
    eid = eid_ref[...]
    rows = [jnp.sum(jnp.where(ei == eid[k:k + 1, :], base, 0), axis=0, keepdims=True)
            for k in range(TOP_K)]
    rows.append(jnp.zeros((SUBLANES - TOP_K, T), jnp.int32))
    pos_ref[...] = jnp.concatenate(rows, axis=0) + rank_ref[...]


def _slots(eid, rank, base):
    B, _, S = eid.shape
    nt, E = base.shape[1], base.shape[2]
    tr = S // nt
    tok = lambda b, i: (b, 0, i)
    return pl.pallas_call(
        _slots_kernel,
        grid=(B, nt),
        in_specs=[
            pl.BlockSpec((None, SUBLANES, tr), tok),
            pl.BlockSpec((None, SUBLANES, tr), tok),
            pl.BlockSpec((None, None, E, 1), lambda b, i: (b, i, 0, 0)),
        ],
        out_specs=pl.BlockSpec((None, SUBLANES, tr), tok),
        out_shape=jax.ShapeDtypeStruct((B, SUBLANES, S), jnp.int32),
        compiler_params=_params("parallel", "parallel"),
        name="slots",
    )(eid, rank, base.reshape(B, nt, E, 1))


def _sc_mesh():
    return plsc.VectorSubcoreMesh(core_axis_name="core", subcore_axis_name="subcore")


def _sc_scatter_rows(src, idx2d, n_out):
    R, W = src.shape
    K = idx2d.shape[0]

    @functools.partial(pl.kernel, out_type=jax.ShapeDtypeStruct((n_out, W), src.dtype), mesh=_sc_mesh(),
                       scratch_types=[], name="sc_dispatch")
    def scatter(x_hbm, i_hbm, o_hbm):
        def body(x_vmem, i_vmem):
            pltpu.sync_copy(x_vmem, o_hbm.at[i_vmem.at[0]])

        pltpu.emit_pipeline(
            body, grid=(K, R // SC_WINDOW),
            in_specs=[pl.BlockSpec((SC_WINDOW, W), lambda k, i: (i, 0)),
                      pl.BlockSpec((1, SC_WINDOW), lambda k, i: (k, i))],
            out_specs=[],
            core_axis_name=("core", "subcore"),
            dimension_semantics=(pltpu.PARALLEL, pltpu.PARALLEL),
        )(x_hbm, i_hbm)

    return scatter(src, idx2d)


def _sc_gather_rows(table, idx):
    N = idx.shape[0]
    W = table.shape[1]

    @functools.partial(pl.kernel, out_type=jax.ShapeDtypeStruct((N, W), table.dtype), mesh=_sc_mesh(),
                       scratch_types=[], name="sc_combine")
    def gather(t_hbm, i_hbm, o_hbm):
        def body(i_vmem, o_vmem):
            pltpu.sync_copy(t_hbm.at[i_vmem.at[0]], o_vmem)

        pltpu.emit_pipeline(
            body, grid=(N // SC_WINDOW,),
            in_specs=[pl.BlockSpec((1, SC_WINDOW), lambda i: (0, i))],
            out_specs=[pl.BlockSpec((SC_WINDOW, W), lambda i: (i, 0))],
            core_axis_name=("core", "subcore"),
            dimension_semantics=(pltpu.PARALLEL,),
        )(i_hbm, o_hbm)

    return gather(table, idx.reshape(1, N))


def _swiglu_hidden(u, w_in):
    gu = jnp.dot(u, w_in, preferred_element_type=F32)
    d = gu.shape[-1] // 2
    g, up = gu[:, :d], gu[:, d:]
    return g * jax.nn.sigmoid(g) * up


def _experts_kernel(blk_ref, used_ref, xs_ref, wi_ref, wo_ref, ys_ref):
    @pl.when(pl.program_id(0) < used_ref[0])
    def _():
        x = _load_packed(xs_ref).astype(BF16)
        h = _swiglu_hidden(x, wi_ref[...])
        _store_packed(ys_ref, jnp.dot(h.astype(BF16), wo_ref[...], preferred_element_type=F32))


def _experts(xs, blk_expert, n_used, wei, weo, bm):
    _, P, Wd = xs.shape
    E, D, d2 = wei.shape
    de = weo.shape[1]
    rows = pl.BlockSpec((ROW_SPLIT, bm, Wd), lambda nb, blk, used: (0, nb, 0))
    return pl.pallas_call(
        _experts_kernel,
        grid_spec=pltpu.PrefetchScalarGridSpec(
            num_scalar_prefetch=2,
            grid=(P // bm,),
            in_specs=[
                rows,
                pl.BlockSpec((None, D, d2), lambda nb, blk, used: (blk[nb], 0, 0)),
                pl.BlockSpec((None, de, D), lambda nb, blk, used: (blk[nb], 0, 0)),
            ],
            out_specs=rows,
        ),
        out_shape=jax.ShapeDtypeStruct(xs.shape, U32),
        compiler_params=_params("arbitrary"),
        name="experts",
    )(blk_expert, n_used, xs, wei, weo)


def _moe_out_kernel(u_ref, x1_ref, yg_ref, w_ref, mod_ref, wsi_ref, wso_ref, g2_ref, b2_ref, o_ref):
    u = _load_packed(u_ref).astype(BF16)
    hs = _swiglu_hidden(u, wsi_ref[...])
    acc = jnp.dot(hs.astype(BF16), wso_ref[...], preferred_element_type=F32)
    w = w_ref[...]
    for k in range(TOP_K):
        acc = acc + w[:, k:k + 1] * _load_packed(yg_ref.at[k])
    gate2 = mod_ref[5:6, :]
    y = DN_ALPHA * x1_ref[...] + gate2 * acc
    o_ref[...] = _layer_norm(y) * g2_ref[...] + b2_ref[...]


def _moe_out(u2w, x1, yg, w_t, mod, wsi, wso, g2, b2):
    B, S, D = x1.shape
    Wd = u2w.shape[-1]
    tm = _tile(S, 512)
    full = lambda shape: pl.BlockSpec(shape, lambda b, i: tuple(0 for _ in shape))
    return pl.pallas_call(
        _moe_out_kernel,
        grid=(B, S // tm),
        in_specs=[
            pl.BlockSpec((ROW_SPLIT, None, tm, Wd), lambda b, i: (0, b, i, 0)),
            pl.BlockSpec((None, tm, D), lambda b, i: (b, i, 0)),
            pl.BlockSpec((TOP_K, ROW_SPLIT, None, tm, Wd), lambda b, i: (0, 0, b, i, 0)),
            pl.BlockSpec((None, tm, LANES), lambda b, i: (b, i, 0)),
            pl.BlockSpec((None, N_MOD, D), lambda b, i: (b, 0, 0)),
            full(wsi.shape), full(wso.shape), full((1, D)), full((1, D)),
        ],
        out_specs=pl.BlockSpec((None, tm, D), lambda b, i: (b, i, 0)),
        out_shape=jax.ShapeDtypeStruct((B, S, D), F32),
        compiler_params=_params("parallel", "parallel"),
        name="moe_out",
    )(u2w, x1, yg, w_t, mod, wsi, wso, g2, b2)


def _routed_experts(u2w, eid, rank, cnt, wei, weo):
    _, B, S, Wd = u2w.shape
    T = B * S
    E = wei.shape[0]
    bm = _tile(S, EXPERT_BLOCK)
    n_blocks = (T * TOP_K) // bm + E
    P = n_blocks * bm

    cnt_te = cnt[:, :, 0, :].astype(jnp.int32).reshape(-1, E)
    total = jnp.sum(cnt_te, axis=0)
    padded = (total + bm - 1) // bm * bm
    region_end = jnp.cumsum(padded)
    region_start = region_end - padded
    base = region_start[None, :] + jnp.cumsum(cnt_te, axis=0) - cnt_te
    blk_start = jnp.arange(n_blocks, dtype=jnp.int32) * bm
    blk_expert = jnp.minimum(jnp.sum(region_end[None, :] <= blk_start[:, None], axis=1), E - 1).astype(jnp.int32)
    n_used = (region_end[-1] // bm).astype(jnp.int32).reshape(1)

    pos = _slots(eid, rank, base.reshape(B, -1, E))
    pos = jnp.transpose(pos[:, :TOP_K, :], (1, 0, 2)).reshape(TOP_K, T)
    halves = jnp.stack([pos + h * P for h in range(ROW_SPLIT)], axis=1)

    xs = _sc_scatter_rows(u2w.reshape(ROW_SPLIT * T, Wd), halves.reshape(TOP_K, ROW_SPLIT * T), ROW_SPLIT * P)
    ys = _experts(xs.reshape(ROW_SPLIT, P, Wd), blk_expert, n_used, wei, weo, bm)
    yg = _sc_gather_rows(ys.reshape(ROW_SPLIT * P, Wd), halves.reshape(-1))
    return yg.reshape(TOP_K, ROW_SPLIT, B, S, Wd)


def _rope_tables(S):
    rows = S // GRID_W
    row_idx = jnp.repeat(jnp.arange(rows, dtype=F32), GRID_W)
    col_idx = jnp.tile(jnp.arange(GRID_W, dtype=F32), rows)
    inv_freq = ROPE_THETA ** (-jnp.arange(ROPE_FREQS, dtype=F32) / ROPE_FREQS)
    ang = jnp.stack([row_idx[:, None] * inv_freq, col_idx[:, None] * inv_freq], axis=1)
    cos, sin = jnp.cos(ang), jnp.sin(ang)
    cos_h = jnp.stack([cos, cos], axis=2).reshape(S, HEAD_DIM)
    sin_h = jnp.stack([-sin, sin], axis=2).reshape(S, HEAD_DIM)
    reps = LANES // HEAD_DIM
    return jnp.tile(cos_h, (1, reps)), jnp.tile(sin_h, (1, reps))


def _gate_weights(lru_wa, lru_ba, lru_wx, lru_bx, C):
    nb, blk, _ = lru_wa.shape[1:]
    d_lru = nb * blk
    per = C // blk

    def dense(w):
        w = w.reshape(d_lru // C, per, blk, blk)
        eye = jnp.eye(per, dtype=w.dtype)
        return jnp.einsum("hpcd,pq->hpcqd", w, eye).reshape(d_lru // C, C, C)

    w = jnp.concatenate([dense(lru_wa[0]), dense(lru_wx[0]), dense(lru_wa[1]), dense(lru_wx[1])], axis=-1)
    halves = lambda v: v.reshape(d_lru // C, 1, C)
    b = jnp.concatenate([halves(lru_ba[0]), halves(lru_bx[0]), halves(lru_ba[1]), halves(lru_bx[1])], axis=-1)
    return w.astype(BF16), b


def _encoder(x, c, P):
    B, S, D = x.shape
    mod = _ada(c, P["w_ada"], P["b_ada"]).reshape(B, N_MOD, D)
    cos_t, sin_t = _rope_tables(S)
    q, kt, v, xl, gl = _in_proj(x, mod, P["w_in"], P["gq"], P["gk"], cos_t, sin_t, P["ones_blk"], P["d_lru"])
    att = _attention(q, kt, v)
    lru = _lru(xl, gl, P["conv_w"], P["conv_b"], P["w_gate"], P["b_gate"], P["lam"])
    x1, u2w, logits_t = _out_proj(att, lru, x, mod, P["ga"], P["gr"], P["w_out"], P["ln1_g"], P["ln1_b"], P["wr_t"])
    w_t, eid, rank, cnt = _route(logits_t, P["router_bias"])
    yg = _routed_experts(u2w, eid, rank, cnt, P["w_exp_in"], P["w_exp_out"])
    return _moe_out(u2w, x1, yg, w_t, mod, P["w_sh_in"], P["w_sh_out"], P["ln2_g"], P["ln2_b"])


def kernel(x_prompt, x_sample, c_prompt, c_sample, w_ada, b_ada, w_in, q_norm_g, k_norm_g, conv_w, conv_b, lru_wa, lru_ba, lru_wx, lru_bx, lru_lambda, attn_out_g, lru_out_g, w_out, ln1_g, ln1_b, w_router, router_bias, w_exp_in, w_exp_out, w_sh_in, w_sh_out, ln2_g, ln2_b):
    l = 0
    D = x_prompt.shape[-1]
    d_lru = lru_out_g.shape[-1]
    lru_half = 2 * LANES
    w_gate, b_gate = _gate_weights(lru_wa[l], lru_ba[l], lru_wx[l], lru_bx[l], lru_half)
    idx = jnp.arange(2 * LANES)
    ones_blk = (idx[:, None] // HEAD_DIM == idx[None, :] // HEAD_DIM).astype(BF16)
    P = {
        "w_ada": w_ada[l], "b_ada": b_ada[l],
        "w_in": w_in[l].astype(BF16),
        "gq": (jnp.tile(q_norm_g[l], N_Q_HEADS) * HEAD_DIM ** -0.5).reshape(1, D_ATT),
        "gk": jnp.tile(k_norm_g[l], N_KV_HEADS).reshape(1, D_KV),
        "ones_blk": ones_blk, "d_lru": d_lru,
        "conv_w": conv_w[l], "conv_b": conv_b[l].reshape(1, d_lru),
        "w_gate": w_gate, "b_gate": b_gate, "lam": lru_lambda[l],
        "ga": attn_out_g[l].reshape(1, D_ATT), "gr": lru_out_g[l].reshape(1, d_lru),
        "w_out": w_out[l].astype(BF16),
        "ln1_g": ln1_g[l].reshape(1, D), "ln1_b": ln1_b[l].reshape(1, D),
        "wr_t": w_router[l].T, "router_bias": router_bias[l],
        "w_exp_in": w_exp_in[l].astype(BF16), "w_exp_out": w_exp_out[l].astype(BF16),
        "w_sh_in": w_sh_in[l].astype(BF16), "w_sh_out": w_sh_out[l].astype(BF16),
        "ln2_g": ln2_g[l].reshape(1, D), "ln2_b": ln2_b[l].reshape(1, D),
    }
    return (_encoder(x_prompt, c_prompt, P), _encoder(x_sample, c_sample, P))
```

```python
import functools

import jax
import jax.numpy as jnp
from jax import lax
from jax.experimental import pallas as pl
from jax.experimental.pallas import tpu as pltpu
from jax.experimental.pallas import tpu_sc as plsc

F32 = jnp.float32
BF16 = jnp.bfloat16
U32 = jnp.uint32
HIGHEST = lax.Precision.HIGHEST

HEAD_DIM = 64
N_Q_HEADS = 8
N_KV_HEADS = 2
Q_PER_KV = N_Q_HEADS // N_KV_HEADS
D_ATT = N_Q_HEADS * HEAD_DIM
D_KV = N_KV_HEADS * HEAD_DIM
N_LRU_BLOCKS = 8
CONV_W = 4
LRU_C = 8.0
GRID_W = 64
ROPE_THETA = 10000.0
ROPE_FREQS = HEAD_DIM // 4
N_EXPERTS = 64
TOP_K = 6
N_GROUPS = 8
GROUP_SIZE = N_EXPERTS // N_GROUPS
TOPK_GROUPS = 4
ROUTED_SCALE = 2.5
N_MOD = 6
EPS = 1e-6
DEPTH = 1
DN_ALPHA = (2.0 * DEPTH) ** 0.25
LOG2_E = 1.4426950408889634
SQRT_FLOOR = 1e-30

LANES = 128
SUBLANES = 8
VMEM_LIMIT_BYTES = 56 * 1024 * 1024

ROW_SPLIT = 2
SC_WINDOW = 128
ATTN_BLOCK = 1024
ATTN_SUB = 256
ROUTE_TILE = 512
EXPERT_BLOCK = 512


def _tile(n, pref):
    if n <= pref:
        return n
    for t in range(pref, 0, -1):
        if n % t == 0 and t % SUBLANES == 0:
            return t
    return n


def _params(*sem):
    return pltpu.CompilerParams(dimension_semantics=sem, vmem_limit_bytes=VMEM_LIMIT_BYTES)


def _layer_norm(x):
    mu = jnp.mean(x, axis=-1, keepdims=True)
    xc = x - mu
    var = jnp.mean(xc * xc, axis=-1, keepdims=True)
    return xc * lax.rsqrt(var + EPS)


def _rms(x):
    return x * lax.rsqrt(jnp.mean(x * x, axis=-1, keepdims=True) + EPS)


def _pack_pair(a, b):
    ah = lax.bitcast_convert_type(a.astype(BF16).astype(F32), U32)
    bh = lax.bitcast_convert_type(b.astype(BF16).astype(F32), U32)
    return ah | (bh >> 16)


def _unpack_pair(w):
    a = lax.bitcast_convert_type(w & jnp.uint32(0xFFFF0000), F32)
    b = lax.bitcast_convert_type(w << 16, F32)
    return a, b


def _store_packed(ref, v):
    q = v.shape[-1] // (2 * ROW_SPLIT)
    for h in range(ROW_SPLIT):
        ref[h] = _pack_pair(v[:, 2 * h * q:(2 * h + 1) * q], v[:, (2 * h + 1) * q:(2 * h + 2) * q])


def _load_packed(ref):
    parts = []
    for h in range(ROW_SPLIT):
        parts.extend(_unpack_pair(ref[h]))
    return jnp.concatenate(parts, axis=-1)


def _ada_kernel(c_ref, w_ref, b_ref, o_ref):
    c = c_ref[...]
    sc = c * jax.nn.sigmoid(c)
    o_ref[...] = jnp.dot(sc, w_ref[...], precision=HIGHEST, preferred_element_type=F32) + b_ref[...]


def _ada(c, w_ada, b_ada):
    B, D = c.shape
    N = w_ada.shape[1]
    tn = _tile(N, 1024)
    return pl.pallas_call(
        _ada_kernel,
        grid=(N // tn,),
        in_specs=[
            pl.BlockSpec((B, D), lambda j: (0, 0)),
            pl.BlockSpec((D, tn), lambda j: (0, j)),
            pl.BlockSpec((1, tn), lambda j: (0, j)),
        ],
        out_specs=pl.BlockSpec((B, tn), lambda j: (0, j)),
        out_shape=jax.ShapeDtypeStruct((B, N), F32),
        compiler_params=_params("arbitrary"),
        name="ada",
    )(c, w_ada, b_ada.reshape(1, N))


def _group_sumsq(t, ones_blk):
    sq = t * t
    hi = sq.astype(BF16)
    lo = (sq - hi.astype(F32)).astype(BF16)
    return (jnp.dot(hi, ones_blk, preferred_element_type=F32)
            + jnp.dot(lo, ones_blk, preferred_element_type=F32))


def _rope(t, cos, sin_signed, first_half):
    fwd = pltpu.roll(t, LANES - ROPE_FREQS, axis=1)
    bwd = pltpu.roll(t, ROPE_FREQS, axis=1)
    return t * cos + jnp.where(first_half, fwd, bwd) * sin_signed


def _in_proj_kernel(x_ref, mod_ref, w_ref, gq_ref, gk_ref, cos_ref, sin_ref, ones_ref,
                    qt_ref, k_ref, vt_ref, xl_ref, gl_ref):
    x = x_ref[...]
    shift1 = mod_ref[0:1, :]
    scale1 = mod_ref[1:2, :]
    u = _layer_norm(x) * (1.0 + scale1) + shift1
    proj = jnp.dot(u.astype(BF16), w_ref[...], preferred_element_type=F32)

    cos = cos_ref[...]
    sin_signed = sin_ref[...]
    lane = lax.broadcasted_iota(jnp.int32, cos.shape, 1)
    first_half = (lane % (2 * ROPE_FREQS)) < ROPE_FREQS
    ones2 = ones_ref[...]
    ones1 = ones2[0:LANES, 0:LANES]

    def norm_rope(t, g, ones_blk):
        ss = _group_sumsq(t, ones_blk)
        tn = t * lax.rsqrt(ss * (1.0 / HEAD_DIM) + EPS) * g
        chunks = [_rope(tn[:, c * LANES:(c + 1) * LANES], cos, sin_signed, first_half)
                  for c in range(t.shape[1] // LANES)]
        return chunks

    w2 = 2 * LANES
    for half in range(D_ATT // w2):
        t = proj[:, half * w2:(half + 1) * w2]
        chunks = norm_rope(t, gq_ref[:, half * w2:(half + 1) * w2], ones2)
        for c, ch in enumerate(chunks):
            lo = half * w2 + c * LANES
            qt_ref[lo:lo + LANES, :] = ch.T.astype(BF16)
    k = proj[:, D_ATT:D_ATT + D_KV]
    (kr,) = norm_rope(k, gk_ref[...], ones1)
    k_ref[...] = kr.astype(BF16)
    vt_ref[...] = proj[:, D_ATT + D_KV:D_ATT + 2 * D_KV].T.astype(BF16)
    o = D_ATT + 2 * D_KV
    d_lru = xl_ref.shape[-1]
    xl_ref[...] = proj[:, o:o + d_lru]
    gl_ref[...] = proj[:, o + d_lru:o + 2 * d_lru]


def _in_proj(x, mod, w_in_bf, gq, gk, cos_t, sin_t, ones_blk, d_lru):
    B, S, D = x.shape
    tm = _tile(S, 512)
    d_in = w_in_bf.shape[1]
    full = lambda shape: pl.BlockSpec(shape, lambda b, i: tuple(0 for _ in shape))
    return pl.pallas_call(
        _in_proj_kernel,
        grid=(B, S // tm),
        in_specs=[
            pl.BlockSpec((None, tm, D), lambda b, i: (b, i, 0)),
            pl.BlockSpec((None, N_MOD, D), lambda b, i: (b, 0, 0)),
            full((D, d_in)),
            full((1, D_ATT)),
            full((1, D_KV)),
            pl.BlockSpec((tm, LANES), lambda b, i: (i, 0)),
            pl.BlockSpec((tm, LANES), lambda b, i: (i, 0)),
            full((2 * LANES, 2 * LANES)),
        ],
        out_specs=[
            pl.BlockSpec((None, D_ATT, tm), lambda b, i: (b, 0, i)),
            pl.BlockSpec((None, tm, D_KV), lambda b, i: (b, i, 0)),
            pl.BlockSpec((None, D_KV, tm), lambda b, i: (b, 0, i)),
            pl.BlockSpec((None, tm, d_lru), lambda b, i: (b, i, 0)),
            pl.BlockSpec((None, tm, d_lru), lambda b, i: (b, i, 0)),
        ],
        out_shape=[
            jax.ShapeDtypeStruct((B, D_ATT, S), BF16),
            jax.ShapeDtypeStruct((B, S, D_KV), BF16),
            jax.ShapeDtypeStruct((B, D_KV, S), BF16),
            jax.ShapeDtypeStruct((B, S, d_lru), F32),
            jax.ShapeDtypeStruct((B, S, d_lru), F32),
        ],
        compiler_params=_params("parallel", "parallel"),
        name="in_proj",
    )(x, mod, w_in_bf, gq, gk, cos_t, sin_t, ones_blk)


def _attn_kernel(qt_ref, k_ref, vt_ref, o_ref, *, sub):
    tq = qt_ref.shape[-1]
    k = k_ref[...]
    zeros = jnp.zeros((HEAD_DIM, sub), BF16)
    ones_rows = jnp.ones((2 * SUBLANES, k.shape[0]), BF16)
    units = [(q0, pair) for q0 in range(0, tq, sub) for pair in range(N_Q_HEADS // 2)]

    def scores(q0, pair):
        j = (2 * pair) // Q_PER_KV
        cols = []
        for h in (2 * pair, 2 * pair + 1):
            qt = qt_ref[h * HEAD_DIM:(h + 1) * HEAD_DIM, q0:q0 + sub]
            cols.append(jnp.concatenate([qt, zeros] if j == 0 else [zeros, qt], axis=0))
        return jnp.dot(k, jnp.concatenate(cols, axis=1), preferred_element_type=F32)

    def finish(q0, pair, st):
        j = (2 * pair) // Q_PER_KV
        vt = jnp.concatenate([vt_ref[j * HEAD_DIM:(j + 1) * HEAD_DIM, :], ones_rows], axis=0)
        m = jnp.max(st, axis=0, keepdims=True)
        p = jnp.exp2(st - m)
        ol = jnp.dot(vt, p.astype(BF16), preferred_element_type=F32)
        ot = ol[:HEAD_DIM, :] / ol[HEAD_DIM:HEAD_DIM + 1, :]
        o2 = jnp.concatenate([ot[:, :sub], ot[:, sub:]], axis=0)
        o_ref[q0:q0 + sub, 2 * pair * HEAD_DIM:(2 * pair + 2) * HEAD_DIM] = o2.T.astype(BF16)

    st = scores(*units[0])
    for n, unit in enumerate(units):
        st_next = scores(*units[n + 1]) if n + 1 < len(units) else None
        finish(*unit, st)
        st = st_next


def _attention(qt, k, vt):
    B, _, S = qt.shape
    tq = _tile(S, ATTN_BLOCK)
    sub = _tile(tq, ATTN_SUB)
    return pl.pallas_call(
        functools.partial(_attn_kernel, sub=sub),
        grid=(B, S // tq),
        in_specs=[
            pl.BlockSpec((None, D_ATT, tq), lambda b, i: (b, 0, i)),
            pl.BlockSpec((None, S, D_KV), lambda b, i: (b, 0, 0)),
            pl.BlockSpec((None, D_KV, S), lambda b, i: (b, 0, 0)),
        ],
        out_specs=pl.BlockSpec((None, tq, D_ATT), lambda b, i: (b, i, 0)),
        out_shape=jax.ShapeDtypeStruct((B, S, D_ATT), BF16),
        compiler_params=_params("parallel", "parallel"),
        name="attn",
    )(qt, k, vt)


def _group_scan(a, b, reverse):
    sub = lax.broadcasted_iota(jnp.int32, a.shape, 1)
    for k in (1, 2, 4):
        if reverse:
            shift, valid = SUBLANES - k, sub < SUBLANES - k
        else:
            shift, valid = k, sub >= k
        a_n = jnp.where(valid, pltpu.roll(a, shift, axis=1), 1.0)
        b_n = jnp.where(valid, pltpu.roll(b, shift, axis=1), 0.0)
        b = a * b_n + b
        a = a * a_n
    return a, b


def _lru_kernel(xl_ref, gl_ref, cw_ref, cb_ref, w_ref, bias_ref, lam_ref, o_ref,
                xc_ref, af_ref, bf_ref, ab_ref, bb_ref, *, chunk):
    S, C = xl_ref.shape
    G = S // SUBLANES
    xl = xl_ref[...]
    row = lax.broadcasted_iota(jnp.int32, (S, 1), 0)
    xm2 = jnp.where(row >= 2, pltpu.roll(xl, 2, axis=0), 0.0)
    xm1 = jnp.where(row >= 1, pltpu.roll(xl, 1, axis=0), 0.0)
    xp1 = jnp.where(row < S - 1, pltpu.roll(xl, S - 1, axis=0), 0.0)
    xc_ref[...] = (xm2 * cw_ref[0:1, :] + xm1 * cw_ref[1:2, :] + xl * cw_ref[2:3, :]
                   + xp1 * cw_ref[3:4, :] + cb_ref[...])

    lam = lam_ref[...]
    nlam = -lam
    softplus = jnp.maximum(nlam, 0.0) + jnp.log(1.0 + jnp.exp(-jnp.abs(nlam)))
    decay_log2 = (-LRU_C * LOG2_E) * softplus
    cg = chunk // SUBLANES

    def gates(ci, carry):
        r0 = pl.multiple_of(ci * chunk, chunk)
        g0 = pl.multiple_of(ci * cg, cg)
        xc = xc_ref[pl.ds(r0, chunk), :]
        g = jnp.dot(xc.astype(BF16), w_ref[...], preferred_element_type=F32) + bias_ref[...]
        for d, (a_ref, b_ref) in enumerate(((af_ref, bf_ref), (ab_ref, bb_ref))):
            r = jax.nn.sigmoid(g[:, (2 * d) * C:(2 * d + 1) * C])
            i = jax.nn.sigmoid(g[:, (2 * d + 1) * C:(2 * d + 2) * C])
            a = jnp.exp2(r * decay_log2[d:d + 1, :])
            gain_sq = 1.0 - a * a
            gain = gain_sq * lax.rsqrt(jnp.maximum(gain_sq, SQRT_FLOOR))
            bt = gain * i * xc
            a3, b3 = _group_scan(a.reshape(cg, SUBLANES, C), bt.reshape(cg, SUBLANES, C), d == 1)
            a_ref[pl.ds(g0, cg), :, :] = a3
            b_ref[pl.ds(g0, cg), :, :] = b3
        return carry

    lax.fori_loop(0, S // chunk, gates, 0)

    def step(gi, carry):
        hf, hb = carry
        hf_new = af_ref[gi] * hf + bf_ref[gi]
        bf_ref[gi] = hf_new
        gb = G - 1 - gi
        hb_new = ab_ref[gb] * hb + bb_ref[gb]
        bb_ref[gb] = hb_new
        return hf_new[SUBLANES - 1:SUBLANES, :], hb_new[0:1, :]

    zero = jnp.zeros((1, C), F32)
    lax.fori_loop(0, G, step, (zero, zero), unroll=4)

    def emit(ci, carry):
        r0 = pl.multiple_of(ci * chunk, chunk)
        g0 = pl.multiple_of(ci * cg, cg)
        h = (bf_ref[pl.ds(g0, cg), :, :] + bb_ref[pl.ds(g0, cg), :, :]).reshape(chunk, C)
        gl = gl_ref[pl.ds(r0, chunk), :]
        o_ref[pl.ds(r0, chunk), :] = (jax.nn.gelu(gl) * h).astype(BF16)
        return carry

    lax.fori_loop(0, S // chunk, emit, 0)


def _lru(xl, gl, conv_w, conv_b, w_gate, b_gate, lam):
    B, S, d_lru = xl.shape
    nh, C, _ = w_gate.shape
    chunk = _tile(S, 512)
    G = S // SUBLANES
    col = lambda b, h: (b, 0, h)
    return pl.pallas_call(
        functools.partial(_lru_kernel, chunk=chunk),
        grid=(B, nh),
        in_specs=[
            pl.BlockSpec((None, S, C), col),
            pl.BlockSpec((None, S, C), col),
            pl.BlockSpec((CONV_W, C), lambda b, h: (0, h)),
            pl.BlockSpec((1, C), lambda b, h: (0, h)),
            pl.BlockSpec((None, C, 4 * C), lambda b, h: (h, 0, 0)),
            pl.BlockSpec((None, 1, 4 * C), lambda b, h: (h, 0, 0)),
            pl.BlockSpec((2, C), lambda b, h: (0, h)),
        ],
        out_specs=pl.BlockSpec((None, S, C), col),
        out_shape=jax.ShapeDtypeStruct((B, S, d_lru), BF16),
        scratch_shapes=[pltpu.VMEM((S, C), F32)] + [pltpu.VMEM((G, SUBLANES, C), F32)] * 4,
        compiler_params=_params("parallel", "parallel"),
        name="lru",
    )(xl, gl, conv_w, conv_b, w_gate, b_gate, lam)


def _out_proj_kernel(att_ref, lru_ref, x_ref, mod_ref, ga_ref, gr_ref, w_ref, g1_ref, b1_ref, wr_ref,
                     x1_ref, u2_ref, lg_ref):
    d_att = att_ref.shape[-1]
    an = _rms(att_ref[...].astype(F32)) * ga_ref[...]
    rn = _rms(lru_ref[...].astype(F32)) * gr_ref[...]
    mix = (jnp.dot(an.astype(BF16), w_ref[0:d_att, :], preferred_element_type=F32)
           + jnp.dot(rn.astype(BF16), w_ref[d_att:, :], preferred_element_type=F32))
    gate1 = mod_ref[2:3, :]
    shift2 = mod_ref[3:4, :]
    scale2 = mod_ref[4:5, :]
    x1 = _layer_norm(DN_ALPHA * x_ref[...] + gate1 * mix) * g1_ref[...] + b1_ref[...]
    x1_ref[...] = x1
    u2 = _layer_norm(x1) * (1.0 + scale2) + shift2
    _store_packed(u2_ref, u2)
    E = lg_ref.shape[0]
    u_hi = u2.astype(BF16)
    u_lo = (u2 - u_hi.astype(F32)).astype(BF16)
    parts = (jnp.dot(u_hi, wr_ref[0], preferred_element_type=F32)
             + jnp.dot(u_lo, wr_ref[1], preferred_element_type=F32))
    parts_t = parts.T
    lg_ref[...] = parts_t[:E, :] + parts_t[E:, :]


def _out_proj(att, lru, x, mod, ga, gr, w_out_bf, g1, b1, wr_split):
    B, S, D = x.shape
    tm = _tile(S, 512)
    d_att, d_lru = att.shape[-1], lru.shape[-1]
    full = lambda shape: pl.BlockSpec(shape, lambda b, i: tuple(0 for _ in shape))
    row = lambda w: pl.BlockSpec((None, tm, w), lambda b, i: (b, i, 0))
    return pl.pallas_call(
        _out_proj_kernel,
        grid=(B, S // tm),
        in_specs=[
            row(d_att), row(d_lru), row(D),
            pl.BlockSpec((None, N_MOD, D), lambda b, i: (b, 0, 0)),
            full((1, d_att)), full((1, d_lru)), full((d_att + d_lru, D)),
            full((1, D)), full((1, D)), full((2, D, 2 * N_EXPERTS)),
        ],
        out_specs=[row(D),
                   pl.BlockSpec((ROW_SPLIT, None, tm, D // (2 * ROW_SPLIT)), lambda b, i: (0, b, i, 0)),
                   pl.BlockSpec((None, N_EXPERTS, tm), lambda b, i: (b, 0, i))],
        out_shape=[
            jax.ShapeDtypeStruct((B, S, D), F32),
            jax.ShapeDtypeStruct((ROW_SPLIT, B, S, D // (2 * ROW_SPLIT)), U32),
            jax.ShapeDtypeStruct((B, N_EXPERTS, S), F32),
        ],
        compiler_params=_params("parallel", "parallel"),
        name="out_proj",
    )(att, lru, x, mod, ga, gr, w_out_bf, g1, b1, wr_split)


def _route_kernel(lg_ref, bias_ref, w_ref, eid_ref, rank_ref, cnt_ref):
    E, T = lg_ref.shape
    s = jax.nn.sigmoid(lg_ref[...])
    choice = s + bias_ref[...]
    neg = -jnp.inf

    c3 = choice.reshape(N_GROUPS, GROUP_SIZE, T)
    mem = lax.broadcasted_iota(jnp.int32, c3.shape, 1)
    m1 = jnp.max(c3, axis=1, keepdims=True)
    first = jnp.min(jnp.where(c3 == m1, mem, GROUP_SIZE), axis=1, keepdims=True)
    m2 = jnp.max(jnp.where(mem == first, neg, c3), axis=1, keepdims=True)
    gs = (m1 + m2).reshape(N_GROUPS, T)

    gi = lax.broadcasted_iota(jnp.int32, gs.shape, 0)
    rank = jnp.zeros(gs.shape, jnp.int32)
    for g2 in range(N_GROUPS):
        other = gs[g2:g2 + 1, :]
        beats = (other > gs) | ((other == gs) & (gi > g2))
        rank = rank + beats.astype(jnp.int32)
    gmask = (rank < TOPK_GROUPS).astype(F32).reshape(N_GROUPS, 1, T)
    emask = jnp.broadcast_to(gmask, (N_GROUPS, GROUP_SIZE, T)).reshape(E, T) > 0.5
    cur = jnp.where(emask, choice, neg)

    ei = lax.broadcasted_iota(jnp.int32, (E, T), 0)
    sel = jnp.zeros((E, T), F32)
    eids, ws = [], []
    for _ in range(TOP_K):
        m = jnp.max(cur, axis=0, keepdims=True)
        first_e = jnp.min(jnp.where(cur == m, ei, E), axis=0, keepdims=True)
        hit = ei == first_e
        eids.append(first_e)
        ws.append(jnp.sum(jnp.where(hit, s, 0.0), axis=0, keepdims=True))
        sel = jnp.where(hit, 1.0, sel)
        cur = jnp.where(hit, neg, cur)

    denom = ws[0]
    for w in ws[1:]:
        denom = denom + w
    wk = [w / denom * ROUTED_SCALE for w in ws]
    w_ref[...] = jnp.concatenate(wk + [jnp.zeros((LANES - TOP_K, T), F32)], axis=0).T

    sel_bf = sel.astype(BF16)
    t_row = lax.broadcasted_iota(jnp.int32, (T, T), 0)
    t_col = lax.broadcasted_iota(jnp.int32, (T, T), 1)
    before = (t_row < t_col).astype(BF16)
    cum = jnp.dot(sel_bf, before, preferred_element_type=F32)
    ranks = [jnp.sum(jnp.where(ei == e, cum, 0.0), axis=0, keepdims=True).astype(jnp.int32) for e in eids]
    pad = [jnp.zeros((SUBLANES - TOP_K, T), jnp.int32)]
    eid_ref[...] = jnp.concatenate(eids + pad, axis=0)
    rank_ref[...] = jnp.concatenate(ranks + pad, axis=0)
    cnt_ref[...] = lax.dot_general(jnp.ones((SUBLANES, T), BF16), sel_bf, (((1,), (1,)), ((), ())),
                                   preferred_element_type=F32)


def _route(logits_t, router_bias):
    B, E, S = logits_t.shape
    tr = _tile(S, ROUTE_TILE)
    nt = S // tr
    tok = lambda b, i: (b, 0, i)
    return pl.pallas_call(
        _route_kernel,
        grid=(B, nt),
        in_specs=[
            pl.BlockSpec((None, E, tr), tok),
            pl.BlockSpec((E, 1), lambda b, i: (0, 0)),
        ],
        out_specs=[
            pl.BlockSpec((None, tr, LANES), lambda b, i: (b, i, 0)),
            pl.BlockSpec((None, SUBLANES, tr), tok),
            pl.BlockSpec((None, SUBLANES, tr), tok),
            pl.BlockSpec((None, None, SUBLANES, E), lambda b, i: (b, i, 0, 0)),
        ],
        out_shape=[
            jax.ShapeDtypeStruct((B, S, LANES), F32),
            jax.ShapeDtypeStruct((B, SUBLANES, S), jnp.int32),
            jax.ShapeDtypeStruct((B, SUBLANES, S), jnp.int32),
            jax.ShapeDtypeStruct((B, nt, SUBLANES, E), F32),
        ],
        compiler_params=_params("parallel", "parallel"),
        name="route",
    )(logits_t, router_bias.reshape(E, 1))


def _slots_kernel(eid_ref, rank_ref, base_ref, pos_ref):
    E = base_ref.shape[0]
    T = eid_ref.shape[-1]
    ei = lax.broadcasted_iota(jnp.int32, (E, T), 0)
    base = base_ref[...]
    eid = eid_ref[...]
    rows = [jnp.sum(jnp.where(ei == eid[k:k + 1, :], base, 0), axis=0, keepdims=True)
            for k in range(TOP_K)]
    rows.append(jnp.zeros((SUBLANES - TOP_K, T), jnp.int32))
    pos_ref[...] = jnp.concatenate(rows, axis=0) + rank_ref[...]


def _slots(eid, rank, base):
    B, _, S = eid.shape
    nt, E = base.shape[1], base.shape[2]
    tr = S // nt
    tok = lambda b, i: (b, 0, i)
    return pl.pallas_call(
        _slots_kernel,
        grid=(B, nt),
        in_specs=[
            pl.BlockSpec((None, SUBLANES, tr), tok),
            pl.BlockSpec((None, SUBLANES, tr), tok),
            pl.BlockSpec((None, None, E, 1), lambda b, i: (b, i, 0, 0)),
        ],
        out_specs=pl.BlockSpec((None, SUBLANES, tr), tok),
        out_shape=jax.ShapeDtypeStruct((B, SUBLANES, S), jnp.int32),
        compiler_params=_params("parallel", "parallel"),
        name="slots",
    )(eid, rank, base.reshape(B, nt, E, 1))


def _sc_mesh():
    return plsc.VectorSubcoreMesh(core_axis_name="core", subcore_axis_name="subcore")


def _sc_scatter_rows(src, idx2d, n_out):
    R, W = src.shape
    K = idx2d.shape[0]

    @functools.partial(pl.kernel, out_type=jax.ShapeDtypeStruct((n_out, W), src.dtype), mesh=_sc_mesh(),
                       scratch_types=[], name="sc_dispatch")
    def scatter(x_hbm, i_hbm, o_hbm):
        def body(x_vmem, i_vmem):
            pltpu.sync_copy(x_vmem, o_hbm.at[i_vmem.at[0]])

        pltpu.emit_pipeline(
            body, grid=(R // SC_WINDOW, K),
            in_specs=[pl.BlockSpec((SC_WINDOW, W), lambda i, k: (i, 0)),
                      pl.BlockSpec((1, SC_WINDOW), lambda i, k: (k, i))],
            out_specs=[],
            core_axis_name=("core", "subcore"),
            dimension_semantics=(pltpu.PARALLEL, pltpu.ARBITRARY),
        )(x_hbm, i_hbm)

    return scatter(src, idx2d)


def _sc_gather_rows(table, idx):
    N = idx.shape[0]
    W = table.shape[1]

    @functools.partial(pl.kernel, out_type=jax.ShapeDtypeStruct((N, W), table.dtype), mesh=_sc_mesh(),
                       scratch_types=[], name="sc_combine")
    def gather(t_hbm, i_hbm, o_hbm):
        def body(i_vmem, o_vmem):
            pltpu.sync_copy(t_hbm.at[i_vmem.at[0]], o_vmem)

        pltpu.emit_pipeline(
            body, grid=(N // SC_WINDOW,),
            in_specs=[pl.BlockSpec((1, SC_WINDOW), lambda i: (0, i))],
            out_specs=[pl.BlockSpec((SC_WINDOW, W), lambda i: (i, 0))],
            core_axis_name=("core", "subcore"),
            dimension_semantics=(pltpu.PARALLEL,),
        )(i_hbm, o_hbm)

    return gather(table, idx.reshape(1, N))


def _swiglu_hidden(u, w_in):
    gu = jnp.dot(u, w_in, preferred_element_type=F32)
    d = gu.shape[-1] // 2
    g, up = gu[:, :d], gu[:, d:]
    return g * jax.nn.sigmoid(g) * up


def _experts_kernel(blk_ref, used_ref, xs_ref, wi_ref, wo_ref, ys_ref, wi_bf, wo_bf):
    nb = pl.program_id(0)

    @pl.when((nb == 0) | (blk_ref[nb] != blk_ref[jnp.maximum(nb - 1, 0)]))
    def _():
        wi_bf[...] = wi_ref[...].astype(BF16)
        wo_bf[...] = wo_ref[...].astype(BF16)

    @pl.when(nb < used_ref[0])
    def _():
        x = _load_packed(xs_ref).astype(BF16)
        h = _swiglu_hidden(x, wi_bf[...])
        _store_packed(ys_ref, jnp.dot(h.astype(BF16), wo_bf[...], preferred_element_type=F32))


def _experts(xs, blk_expert, n_used, wei, weo, bm):
    _, P, Wd = xs.shape
    E, D, d2 = wei.shape
    de = weo.shape[1]
    rows = pl.BlockSpec((ROW_SPLIT, bm, Wd), lambda nb, blk, used: (0, jnp.minimum(nb, used[0] - 1), 0))
    return pl.pallas_call(
        _experts_kernel,
        grid_spec=pltpu.PrefetchScalarGridSpec(
            num_scalar_prefetch=2,
            grid=(P // bm,),
            in_specs=[
                rows,
                pl.BlockSpec((None, D, d2), lambda nb, blk, used: (blk[nb], 0, 0)),
                pl.BlockSpec((None, de, D), lambda nb, blk, used: (blk[nb], 0, 0)),
            ],
            out_specs=rows,
            scratch_shapes=[pltpu.VMEM((D, d2), BF16), pltpu.VMEM((de, D), BF16)],
        ),
        out_shape=jax.ShapeDtypeStruct(xs.shape, U32),
        compiler_params=_params("arbitrary"),
        name="experts",
    )(blk_expert, n_used, xs, wei, weo)


def _moe_out_kernel(u_ref, x1_ref, yg_ref, w_ref, mod_ref, wsi_ref, wso_ref, g2_ref, b2_ref, o_ref):
    u = _load_packed(u_ref).astype(BF16)
    hs = _swiglu_hidden(u, wsi_ref[...])
    acc = jnp.dot(hs.astype(BF16), wso_ref[...], preferred_element_type=F32)
    w = w_ref[...]
    for k in range(TOP_K):
        acc = acc + w[:, k:k + 1] * _load_packed(yg_ref.at[k])
    gate2 = mod_ref[5:6, :]
    y = DN_ALPHA * x1_ref[...] + gate2 * acc
    o_ref[...] = _layer_norm(y) * g2_ref[...] + b2_ref[...]


def _moe_out(u2w, x1, yg, w_t, mod, wsi, wso, g2, b2):
    B, S, D = x1.shape
    Wd = u2w.shape[-1]
    tm = _tile(S, 512)
    full = lambda shape: pl.BlockSpec(shape, lambda b, i: tuple(0 for _ in shape))
    return pl.pallas_call(
        _moe_out_kernel,
        grid=(B, S // tm),
        in_specs=[
            pl.BlockSpec((ROW_SPLIT, None, tm, Wd), lambda b, i: (0, b, i, 0)),
            pl.BlockSpec((None, tm, D), lambda b, i: (b, i, 0)),
            pl.BlockSpec((TOP_K, ROW_SPLIT, None, tm, Wd), lambda b, i: (0, 0, b, i, 0)),
            pl.BlockSpec((None, tm, LANES), lambda b, i: (b, i, 0)),
            pl.BlockSpec((None, N_MOD, D), lambda b, i: (b, 0, 0)),
            full(wsi.shape), full(wso.shape), full((1, D)), full((1, D)),
        ],
        out_specs=pl.BlockSpec((None, tm, D), lambda b, i: (b, i, 0)),
        out_shape=jax.ShapeDtypeStruct((B, S, D), F32),
        compiler_params=_params("parallel", "parallel"),
        name="moe_out",
    )(u2w, x1, yg, w_t, mod, wsi, wso, g2, b2)


def _routed_experts(u2w, eid, rank, cnt, wei, weo):
    _, B, S, Wd = u2w.shape
    T = B * S
    E = wei.shape[0]
    bm = _tile(S, EXPERT_BLOCK)
    n_blocks = (T * TOP_K) // bm + E
    P = n_blocks * bm

    cnt_te = cnt[:, :, 0, :].astype(jnp.int32).reshape(-1, E)
    total = jnp.sum(cnt_te, axis=0)
    padded = (total + bm - 1) // bm * bm
    region_end = jnp.cumsum(padded)
    region_start = region_end - padded
    base = region_start[None, :] + jnp.cumsum(cnt_te, axis=0) - cnt_te
    blk_start = jnp.arange(n_blocks, dtype=jnp.int32) * bm
    blk_expert = jnp.minimum(jnp.sum(region_end[None, :] <= blk_start[:, None], axis=1), E - 1).astype(jnp.int32)
    n_used = (region_end[-1] // bm).astype(jnp.int32).reshape(1)

    pos = _slots(eid, rank, base.reshape(B, -1, E))
    pos = jnp.transpose(pos[:, :TOP_K, :], (1, 0, 2)).reshape(TOP_K, T)
    halves = jnp.stack([pos + h * P for h in range(ROW_SPLIT)], axis=1)

    xs = _sc_scatter_rows(u2w.reshape(ROW_SPLIT * T, Wd), halves.reshape(TOP_K, ROW_SPLIT * T), ROW_SPLIT * P)
    ys = _experts(xs.reshape(ROW_SPLIT, P, Wd), blk_expert, n_used, wei, weo, bm)
    yg = _sc_gather_rows(ys.reshape(ROW_SPLIT * P, Wd), halves.reshape(-1))
    return yg.reshape(TOP_K, ROW_SPLIT, B, S, Wd)


def _rope_tables(S):
    rows = S // GRID_W
    row_idx = jnp.repeat(jnp.arange(rows, dtype=F32), GRID_W)
    col_idx = jnp.tile(jnp.arange(GRID_W, dtype=F32), rows)
    inv_freq = ROPE_THETA ** (-jnp.arange(ROPE_FREQS, dtype=F32) / ROPE_FREQS)
    ang = jnp.stack([row_idx[:, None] * inv_freq, col_idx[:, None] * inv_freq], axis=1)
    cos, sin = jnp.cos(ang), jnp.sin(ang)
    cos_h = jnp.stack([cos, cos], axis=2).reshape(S, HEAD_DIM)
    sin_h = jnp.stack([-sin, sin], axis=2).reshape(S, HEAD_DIM)
    reps = LANES // HEAD_DIM
    return jnp.tile(cos_h, (1, reps)), jnp.tile(sin_h, (1, reps))


def _gate_weights(lru_wa, lru_ba, lru_wx, lru_bx, C):
    nb, blk, _ = lru_wa.shape[1:]
    d_lru = nb * blk
    per = C // blk

    def dense(w):
        w = w.reshape(d_lru // C, per, blk, blk)
        eye = jnp.eye(per, dtype=w.dtype)
        return jnp.einsum("hpcd,pq->hpcqd", w, eye).reshape(d_lru // C, C, C)

    w = jnp.concatenate([dense(lru_wa[0]), dense(lru_wx[0]), dense(lru_wa[1]), dense(lru_wx[1])], axis=-1)
    halves = lambda v: v.reshape(d_lru // C, 1, C)
    b = jnp.concatenate([halves(lru_ba[0]), halves(lru_bx[0]), halves(lru_ba[1]), halves(lru_bx[1])], axis=-1)
    return w.astype(BF16), b


def _split_bf16(w):
    hi = w.astype(BF16)
    lo = (w - hi.astype(F32)).astype(BF16)
    return jnp.stack([jnp.concatenate([hi, lo], axis=1), jnp.concatenate([hi, jnp.zeros_like(hi)], axis=1)])


def _encoder(x, c, P):
    B, S, D = x.shape
    mod = _ada(c, P["w_ada"], P["b_ada"]).reshape(B, N_MOD, D)
    cos_t, sin_t = _rope_tables(S)
    qt, k, vt, xl, gl = _in_proj(x, mod, P["w_in"], P["gq"], P["gk"], cos_t, sin_t, P["ones_blk"], P["d_lru"])
    att = _attention(qt, k, vt)
    lru = _lru(xl, gl, P["conv_w"], P["conv_b"], P["w_gate"], P["b_gate"], P["lam"])
    x1, u2w, logits_t = _out_proj(att, lru, x, mod, P["ga"], P["gr"], P["w_out"], P["ln1_g"], P["ln1_b"], P["wr_split"])
    w_t, eid, rank, cnt = _route(logits_t, P["router_bias"])
    yg = _routed_experts(u2w, eid, rank, cnt, P["w_exp_in"], P["w_exp_out"])
    return _moe_out(u2w, x1, yg, w_t, mod, P["w_sh_in"], P["w_sh_out"], P["ln2_g"], P["ln2_b"])


def kernel(x_prompt, x_sample, c_prompt, c_sample, w_ada, b_ada, w_in, q_norm_g, k_norm_g, conv_w, conv_b, lru_wa, lru_ba, lru_wx, lru_bx, lru_lambda, attn_out_g, lru_out_g, w_out, ln1_g, ln1_b, w_router, router_bias, w_exp_in, w_exp_out, w_sh_in, w_sh_out, ln2_g, ln2_b):
    l = 0
    D = x_prompt.shape[-1]
    d_lru = lru_out_g.shape[-1]
    lru_half = 2 * LANES
    w_gate, b_gate = _gate_weights(lru_wa[l], lru_ba[l], lru_wx[l], lru_bx[l], lru_half)
    idx = jnp.arange(2 * LANES)
    ones_blk = (idx[:, None] // HEAD_DIM == idx[None, :] // HEAD_DIM).astype(BF16)
    P = {
        "w_ada": w_ada[l], "b_ada": b_ada[l],
        "w_in": w_in[l].astype(BF16),
        "gq": (jnp.tile(q_norm_g[l], N_Q_HEADS) * (HEAD_DIM ** -0.5 * LOG2_E)).reshape(1, D_ATT),
        "gk": jnp.tile(k_norm_g[l], N_KV_HEADS).reshape(1, D_KV),
        "ones_blk": ones_blk, "d_lru": d_lru,
        "conv_w": conv_w[l], "conv_b": conv_b[l].reshape(1, d_lru),
        "w_gate": w_gate, "b_gate": b_gate, "lam": lru_lambda[l],
        "ga": attn_out_g[l].reshape(1, D_ATT), "gr": lru_out_g[l].reshape(1, d_lru),
        "w_out": w_out[l].astype(BF16),
        "ln1_g": ln1_g[l].reshape(1, D), "ln1_b": ln1_b[l].reshape(1, D),
        "wr_split": _split_bf16(w_router[l]), "router_bias": router_bias[l],
        "w_exp_in": w_exp_in[l], "w_exp_out": w_exp_out[l],
        "w_sh_in": w_sh_in[l].astype(BF16), "w_sh_out": w_sh_out[l].astype(BF16),
        "ln2_g": ln2_g[l].reshape(1, D), "ln2_b": ln2_b[l].reshape(1, D),
    }
    return (_encoder(x_prompt, c_prompt, P), _encoder(x_sample, c_sample, P))
```

```python
import functools

import jax
import jax.numpy as jnp
from jax import lax
from jax.experimental import pallas as pl
from jax.experimental.pallas import tpu as pltpu
from jax.experimental.pallas import tpu_sc as plsc

F32 = jnp.float32
BF16 = jnp.bfloat16
U32 = jnp.uint32
HIGHEST = lax.Precision.HIGHEST

HEAD_DIM = 64
N_Q_HEADS = 8
N_KV_HEADS = 2
Q_PER_KV = N_Q_HEADS // N_KV_HEADS
D_ATT = N_Q_HEADS * HEAD_DIM
D_KV = N_KV_HEADS * HEAD_DIM
N_LRU_BLOCKS = 8
CONV_W = 4
LRU_C = 8.0
GRID_W = 64
ROPE_THETA = 10000.0
ROPE_FREQS = HEAD_DIM // 4
N_EXPERTS = 64
TOP_K = 6
N_GROUPS = 8
GROUP_SIZE = N_EXPERTS // N_GROUPS
TOPK_GROUPS = 4
ROUTED_SCALE = 2.5
N_MOD = 6
EPS = 1e-6
DEPTH = 1
DN_ALPHA = (2.0 * DEPTH) ** 0.25
LOG2_E = 1.4426950408889634
SQRT_FLOOR = 1e-30

LANES = 128
SUBLANES = 8
VMEM_LIMIT_BYTES = 56 * 1024 * 1024

ROW_SPLIT = 2
SC_WINDOW = 128
ATTN_BLOCK = 1024
ATTN_SUB = 256
ROUTE_TILE = 512
EXPERT_BLOCK = 1024
IN_PROJ_TILE = 512
PROJ_TILE = 1024
MOE_OUT_TILE = 512


def _tile(n, pref):
    if n <= pref:
        return n
    for t in range(pref, 0, -1):
        if n % t == 0 and t % SUBLANES == 0:
            return t
    return n


def _params(*sem):
    return pltpu.CompilerParams(dimension_semantics=sem, vmem_limit_bytes=VMEM_LIMIT_BYTES)


def _layer_norm(x):
    mu = jnp.mean(x, axis=-1, keepdims=True)
    xc = x - mu
    var = jnp.mean(xc * xc, axis=-1, keepdims=True)
    return xc * lax.rsqrt(var + EPS)


def _rms(x):
    return x * lax.rsqrt(jnp.mean(x * x, axis=-1, keepdims=True) + EPS)


def _pack_pair(a, b):
    ah = lax.bitcast_convert_type(a.astype(BF16).astype(F32), U32)
    bh = lax.bitcast_convert_type(b.astype(BF16).astype(F32), U32)
    return ah | (bh >> 16)


def _unpack_pair(w):
    a = lax.bitcast_convert_type(w & jnp.uint32(0xFFFF0000), F32)
    b = lax.bitcast_convert_type(w << 16, F32)
    return a, b


def _store_packed(ref, v):
    q = v.shape[-1] // (2 * ROW_SPLIT)
    for h in range(ROW_SPLIT):
        ref[h] = _pack_pair(v[:, 2 * h * q:(2 * h + 1) * q], v[:, (2 * h + 1) * q:(2 * h + 2) * q])


def _load_packed(ref):
    parts = []
    for h in range(ROW_SPLIT):
        parts.extend(_unpack_pair(ref[h]))
    return jnp.concatenate(parts, axis=-1)


def _ada_kernel(c_ref, w_ref, b_ref, o_ref):
    c = c_ref[...]
    sc = c * jax.nn.sigmoid(c)
    o_ref[...] = jnp.dot(sc, w_ref[...], precision=HIGHEST, preferred_element_type=F32) + b_ref[...]


def _ada(c, w_ada, b_ada):
    B, D = c.shape
    N = w_ada.shape[1]
    tn = _tile(N, 1024)
    return pl.pallas_call(
        _ada_kernel,
        grid=(N // tn,),
        in_specs=[
            pl.BlockSpec((B, D), lambda j: (0, 0)),
            pl.BlockSpec((D, tn), lambda j: (0, j)),
            pl.BlockSpec((1, tn), lambda j: (0, j)),
        ],
        out_specs=pl.BlockSpec((B, tn), lambda j: (0, j)),
        out_shape=jax.ShapeDtypeStruct((B, N), F32),
        compiler_params=_params("arbitrary"),
        name="ada",
    )(c, w_ada, b_ada.reshape(1, N))


def _group_sumsq(t, ones_blk):
    sq = t * t
    hi = sq.astype(BF16)
    lo = (sq - hi.astype(F32)).astype(BF16)
    return (jnp.dot(hi, ones_blk, preferred_element_type=F32)
            + jnp.dot(lo, ones_blk, preferred_element_type=F32))


def _rope(t, cos, sin_signed, first_half):
    fwd = pltpu.roll(t, LANES - ROPE_FREQS, axis=1)
    bwd = pltpu.roll(t, ROPE_FREQS, axis=1)
    return t * cos + jnp.where(first_half, fwd, bwd) * sin_signed


def _in_proj_kernel(x_ref, mod_ref, w_ref, gq_ref, gk_ref, cos_ref, sin_ref, ones_ref,
                    qt_ref, k_ref, vt_ref, xl_ref, gl_ref):
    x = x_ref[...]
    shift1 = mod_ref[0:1, :]
    scale1 = mod_ref[1:2, :]
    u = _layer_norm(x) * (1.0 + scale1) + shift1
    proj = jnp.dot(u.astype(BF16), w_ref[...], preferred_element_type=F32)

    cos = cos_ref[...]
    sin_signed = sin_ref[...]
    lane = lax.broadcasted_iota(jnp.int32, cos.shape, 1)
    first_half = (lane % (2 * ROPE_FREQS)) < ROPE_FREQS
    ones2 = ones_ref[...]
    ones1 = ones2[0:LANES, 0:LANES]

    def norm_rope(t, g, ones_blk):
        ss = _group_sumsq(t, ones_blk)
        tn = t * lax.rsqrt(ss * (1.0 / HEAD_DIM) + EPS) * g
        chunks = [_rope(tn[:, c * LANES:(c + 1) * LANES], cos, sin_signed, first_half)
                  for c in range(t.shape[1] // LANES)]
        return chunks

    w2 = 2 * LANES
    for half in range(D_ATT // w2):
        t = proj[:, half * w2:(half + 1) * w2]
        chunks = norm_rope(t, gq_ref[:, half * w2:(half + 1) * w2], ones2)
        for c, ch in enumerate(chunks):
            lo = half * w2 + c * LANES
            qt_ref[lo:lo + LANES, :] = ch.T.astype(BF16)
    k = proj[:, D_ATT:D_ATT + D_KV]
    (kr,) = norm_rope(k, gk_ref[...], ones1)
    k_ref[...] = kr.astype(BF16)
    vt_ref[...] = proj[:, D_ATT + D_KV:D_ATT + 2 * D_KV].T.astype(BF16)
    o = D_ATT + 2 * D_KV
    d_lru = xl_ref.shape[-1]
    xl_ref[...] = proj[:, o:o + d_lru]
    gl_ref[...] = proj[:, o + d_lru:o + 2 * d_lru]


def _in_proj(x, mod, w_in_bf, gq, gk, cos_t, sin_t, ones_blk, d_lru):
    B, S, D = x.shape
    tm = _tile(S, IN_PROJ_TILE)
    d_in = w_in_bf.shape[1]
    full = lambda shape: pl.BlockSpec(shape, lambda b, i: tuple(0 for _ in shape))
    return pl.pallas_call(
        _in_proj_kernel,
        grid=(B, S // tm),
        in_specs=[
            pl.BlockSpec((None, tm, D), lambda b, i: (b, i, 0)),
            pl.BlockSpec((None, N_MOD, D), lambda b, i: (b, 0, 0)),
            full((D, d_in)),
            full((1, D_ATT)),
            full((1, D_KV)),
            pl.BlockSpec((tm, LANES), lambda b, i: (i, 0)),
            pl.BlockSpec((tm, LANES), lambda b, i: (i, 0)),
            full((2 * LANES, 2 * LANES)),
        ],
        out_specs=[
            pl.BlockSpec((None, D_ATT, tm), lambda b, i: (b, 0, i)),
            pl.BlockSpec((None, tm, D_KV), lambda b, i: (b, i, 0)),
            pl.BlockSpec((None, D_KV, tm), lambda b, i: (b, 0, i)),
            pl.BlockSpec((None, tm, d_lru), lambda b, i: (b, i, 0)),
            pl.BlockSpec((None, tm, d_lru), lambda b, i: (b, i, 0)),
        ],
        out_shape=[
            jax.ShapeDtypeStruct((B, D_ATT, S), BF16),
            jax.ShapeDtypeStruct((B, S, D_KV), BF16),
            jax.ShapeDtypeStruct((B, D_KV, S), BF16),
            jax.ShapeDtypeStruct((B, S, d_lru), F32),
            jax.ShapeDtypeStruct((B, S, d_lru), F32),
        ],
        compiler_params=_params("parallel", "parallel"),
        name="in_proj",
    )(x, mod, w_in_bf, gq, gk, cos_t, sin_t, ones_blk)


def _attn_kernel(qt_ref, k_ref, vt_ref, o_ref, *, sub):
    tq = qt_ref.shape[-1]
    k = k_ref[...]
    zeros = jnp.zeros((HEAD_DIM, sub), BF16)
    ones_rows = jnp.ones((2 * SUBLANES, k.shape[0]), BF16)
    units = [(q0, pair) for q0 in range(0, tq, sub) for pair in range(N_Q_HEADS // 2)]

    def scores(q0, pair):
        j = (2 * pair) // Q_PER_KV
        cols = []
        for h in (2 * pair, 2 * pair + 1):
            qt = qt_ref[h * HEAD_DIM:(h + 1) * HEAD_DIM, q0:q0 + sub]
            cols.append(jnp.concatenate([qt, zeros] if j == 0 else [zeros, qt], axis=0))
        return jnp.dot(k, jnp.concatenate(cols, axis=1), preferred_element_type=F32)

    def finish(q0, pair, st):
        j = (2 * pair) // Q_PER_KV
        vt = jnp.concatenate([vt_ref[j * HEAD_DIM:(j + 1) * HEAD_DIM, :], ones_rows], axis=0)
        m = jnp.max(st, axis=0, keepdims=True)
        p = jnp.exp2(st - m)
        ol = jnp.dot(vt, p.astype(BF16), preferred_element_type=F32)
        ot = ol[:HEAD_DIM, :] / ol[HEAD_DIM:HEAD_DIM + 1, :]
        o2 = jnp.concatenate([ot[:, :sub], ot[:, sub:]], axis=0)
        o_ref[q0:q0 + sub, 2 * pair * HEAD_DIM:(2 * pair + 2) * HEAD_DIM] = o2.T.astype(BF16)

    st = scores(*units[0])
    for n, unit in enumerate(units):
        st_next = scores(*units[n + 1]) if n + 1 < len(units) else None
        finish(*unit, st)
        st = st_next


def _attention(qt, k, vt):
    B, _, S = qt.shape
    tq = _tile(S, ATTN_BLOCK)
    sub = _tile(tq, ATTN_SUB)
    return pl.pallas_call(
        functools.partial(_attn_kernel, sub=sub),
        grid=(B, S // tq),
        in_specs=[
            pl.BlockSpec((None, D_ATT, tq), lambda b, i: (b, 0, i)),
            pl.BlockSpec((None, S, D_KV), lambda b, i: (b, 0, 0)),
            pl.BlockSpec((None, D_KV, S), lambda b, i: (b, 0, 0)),
        ],
        out_specs=pl.BlockSpec((None, tq, D_ATT), lambda b, i: (b, i, 0)),
        out_shape=jax.ShapeDtypeStruct((B, S, D_ATT), BF16),
        compiler_params=_params("parallel", "parallel"),
        name="attn",
    )(qt, k, vt)


def _group_scan(a, b, reverse):
    sub = lax.broadcasted_iota(jnp.int32, a.shape, 1)
    for k in (1, 2, 4):
        if reverse:
            shift, valid = SUBLANES - k, sub < SUBLANES - k
        else:
            shift, valid = k, sub >= k
        a_n = jnp.where(valid, pltpu.roll(a, shift, axis=1), 1.0)
        b_n = jnp.where(valid, pltpu.roll(b, shift, axis=1), 0.0)
        b = a * b_n + b
        a = a * a_n
    return a, b


def _lru_kernel(xl_ref, gl_ref, cw_ref, cb_ref, w_ref, bias_ref, lam_ref, o_ref,
                xc_ref, af_ref, bf_ref, ab_ref, bb_ref, *, chunk):
    S, C = xl_ref.shape
    G = S // SUBLANES
    xl = xl_ref[...]
    row = lax.broadcasted_iota(jnp.int32, (S, 1), 0)
    xm2 = jnp.where(row >= 2, pltpu.roll(xl, 2, axis=0), 0.0)
    xm1 = jnp.where(row >= 1, pltpu.roll(xl, 1, axis=0), 0.0)
    xp1 = jnp.where(row < S - 1, pltpu.roll(xl, S - 1, axis=0), 0.0)
    xc_ref[...] = (xm2 * cw_ref[0:1, :] + xm1 * cw_ref[1:2, :] + xl * cw_ref[2:3, :]
                   + xp1 * cw_ref[3:4, :] + cb_ref[...])

    lam = lam_ref[...]
    nlam = -lam
    softplus = jnp.maximum(nlam, 0.0) + jnp.log(1.0 + jnp.exp(-jnp.abs(nlam)))
    decay_log2 = (-LRU_C * LOG2_E) * softplus
    cg = chunk // SUBLANES

    def gates(ci, carry):
        r0 = pl.multiple_of(ci * chunk, chunk)
        g0 = pl.multiple_of(ci * cg, cg)
        xc = xc_ref[pl.ds(r0, chunk), :]
        g = jnp.dot(xc.astype(BF16), w_ref[...], preferred_element_type=F32) + bias_ref[...]
        for d, (a_ref, b_ref) in enumerate(((af_ref, bf_ref), (ab_ref, bb_ref))):
            r = jax.nn.sigmoid(g[:, (2 * d) * C:(2 * d + 1) * C])
            i = jax.nn.sigmoid(g[:, (2 * d + 1) * C:(2 * d + 2) * C])
            a = jnp.exp2(r * decay_log2[d:d + 1, :])
            gain_sq = 1.0 - a * a
            gain = gain_sq * lax.rsqrt(jnp.maximum(gain_sq, SQRT_FLOOR))
            bt = gain * i * xc
            a3, b3 = _group_scan(a.reshape(cg, SUBLANES, C), bt.reshape(cg, SUBLANES, C), d == 1)
            a_ref[pl.ds(g0, cg), :, :] = a3
            b_ref[pl.ds(g0, cg), :, :] = b3
        return carry

    lax.fori_loop(0, S // chunk, gates, 0)

    def step(gi, carry):
        hf, hb = carry
        hf_new = af_ref[gi] * hf + bf_ref[gi]
        bf_ref[gi] = hf_new
        gb = G - 1 - gi
        hb_new = ab_ref[gb] * hb + bb_ref[gb]
        bb_ref[gb] = hb_new
        return hf_new[SUBLANES - 1:SUBLANES, :], hb_new[0:1, :]

    zero = jnp.zeros((1, C), F32)
    lax.fori_loop(0, G, step, (zero, zero), unroll=4)

    def emit(ci, carry):
        r0 = pl.multiple_of(ci * chunk, chunk)
        g0 = pl.multiple_of(ci * cg, cg)
        h = (bf_ref[pl.ds(g0, cg), :, :] + bb_ref[pl.ds(g0, cg), :, :]).reshape(chunk, C)
        gl = gl_ref[pl.ds(r0, chunk), :]
        o_ref[pl.ds(r0, chunk), :] = (jax.nn.gelu(gl) * h).astype(BF16)
        return carry

    lax.fori_loop(0, S // chunk, emit, 0)


def _lru(xl, gl, conv_w, conv_b, w_gate, b_gate, lam):
    B, S, d_lru = xl.shape
    nh, C, _ = w_gate.shape
    chunk = _tile(S, 512)
    G = S // SUBLANES
    col = lambda b, h: (b, 0, h)
    return pl.pallas_call(
        functools.partial(_lru_kernel, chunk=chunk),
        grid=(B, nh),
        in_specs=[
            pl.BlockSpec((None, S, C), col),
            pl.BlockSpec((None, S, C), col),
            pl.BlockSpec((CONV_W, C), lambda b, h: (0, h)),
            pl.BlockSpec((1, C), lambda b, h: (0, h)),
            pl.BlockSpec((None, C, 4 * C), lambda b, h: (h, 0, 0)),
            pl.BlockSpec((None, 1, 4 * C), lambda b, h: (h, 0, 0)),
            pl.BlockSpec((2, C), lambda b, h: (0, h)),
        ],
        out_specs=pl.BlockSpec((None, S, C), col),
        out_shape=jax.ShapeDtypeStruct((B, S, d_lru), BF16),
        scratch_shapes=[pltpu.VMEM((S, C), F32)] + [pltpu.VMEM((G, SUBLANES, C), F32)] * 4,
        compiler_params=_params("parallel", "parallel"),
        name="lru",
    )(xl, gl, conv_w, conv_b, w_gate, b_gate, lam)


def _out_proj_kernel(att_ref, lru_ref, x_ref, mod_ref, ga_ref, gr_ref, w_ref, g1_ref, b1_ref, wr_ref,
                     x1_ref, u2_ref, lg_ref):
    d_att = att_ref.shape[-1]
    an = _rms(att_ref[...].astype(F32)) * ga_ref[...]
    rn = _rms(lru_ref[...].astype(F32)) * gr_ref[...]
    mix = (jnp.dot(an.astype(BF16), w_ref[0:d_att, :], preferred_element_type=F32)
           + jnp.dot(rn.astype(BF16), w_ref[d_att:, :], preferred_element_type=F32))
    gate1 = mod_ref[2:3, :]
    shift2 = mod_ref[3:4, :]
    scale2 = mod_ref[4:5, :]
    x1 = _layer_norm(DN_ALPHA * x_ref[...] + gate1 * mix) * g1_ref[...] + b1_ref[...]
    x1_ref[...] = x1
    u2 = _layer_norm(x1) * (1.0 + scale2) + shift2
    _store_packed(u2_ref, u2)
    E = lg_ref.shape[0]
    u_hi = u2.astype(BF16)
    u_lo = (u2 - u_hi.astype(F32)).astype(BF16)
    parts = (jnp.dot(u_hi, wr_ref[...], preferred_element_type=F32)
             + jnp.dot(u_lo, wr_ref[...], preferred_element_type=F32))
    parts_t = parts.T
    lg_ref[...] = parts_t[:E, :] + parts_t[E:, :]


def _out_proj(att, lru, x, mod, ga, gr, w_out_bf, g1, b1, wr_split):
    B, S, D = x.shape
    tm = _tile(S, PROJ_TILE)
    d_att, d_lru = att.shape[-1], lru.shape[-1]
    full = lambda shape: pl.BlockSpec(shape, lambda b, i: tuple(0 for _ in shape))
    row = lambda w: pl.BlockSpec((None, tm, w), lambda b, i: (b, i, 0))
    return pl.pallas_call(
        _out_proj_kernel,
        grid=(B, S // tm),
        in_specs=[
            row(d_att), row(d_lru), row(D),
            pl.BlockSpec((None, N_MOD, D), lambda b, i: (b, 0, 0)),
            full((1, d_att)), full((1, d_lru)), full((d_att + d_lru, D)),
            full((1, D)), full((1, D)), full((D, 2 * N_EXPERTS)),
        ],
        out_specs=[row(D),
                   pl.BlockSpec((ROW_SPLIT, None, tm, D // (2 * ROW_SPLIT)), lambda b, i: (0, b, i, 0)),
                   pl.BlockSpec((None, N_EXPERTS, tm), lambda b, i: (b, 0, i))],
        out_shape=[
            jax.ShapeDtypeStruct((B, S, D), F32),
            jax.ShapeDtypeStruct((ROW_SPLIT, B, S, D // (2 * ROW_SPLIT)), U32),
            jax.ShapeDtypeStruct((B, N_EXPERTS, S), F32),
        ],
        compiler_params=_params("parallel", "parallel"),
        name="out_proj",
    )(att, lru, x, mod, ga, gr, w_out_bf, g1, b1, wr_split)


def _route_kernel(lg_ref, bias_ref, w_ref, eid_ref, rank_ref, cnt_ref):
    E, T = lg_ref.shape
    s = jax.nn.sigmoid(lg_ref[...])
    choice = s + bias_ref[...]
    neg = -jnp.inf

    c3 = choice.reshape(N_GROUPS, GROUP_SIZE, T)
    mem = lax.broadcasted_iota(jnp.int32, c3.shape, 1)
    m1 = jnp.max(c3, axis=1, keepdims=True)
    first = jnp.min(jnp.where(c3 == m1, mem, GROUP_SIZE), axis=1, keepdims=True)
    m2 = jnp.max(jnp.where(mem == first, neg, c3), axis=1, keepdims=True)
    gs = (m1 + m2).reshape(N_GROUPS, T)

    gi = lax.broadcasted_iota(jnp.int32, gs.shape, 0)
    rank = jnp.zeros(gs.shape, jnp.int32)
    for g2 in range(N_GROUPS):
        other = gs[g2:g2 + 1, :]
        beats = (other > gs) | ((other == gs) & (gi > g2))
        rank = rank + beats.astype(jnp.int32)
    gmask = (rank < TOPK_GROUPS).astype(F32).reshape(N_GROUPS, 1, T)
    emask = jnp.broadcast_to(gmask, (N_GROUPS, GROUP_SIZE, T)).reshape(E, T) > 0.5
    cur = jnp.where(emask, choice, neg)

    ei = lax.broadcasted_iota(jnp.int32, (E, T), 0)
    sel = jnp.zeros((E, T), F32)
    eids, ws = [], []
    for _ in range(TOP_K):
        m = jnp.max(cur, axis=0, keepdims=True)
        first_e = jnp.min(jnp.where(cur == m, ei, E), axis=0, keepdims=True)
        hit = ei == first_e
        eids.append(first_e)
        ws.append(jnp.sum(jnp.where(hit, s, 0.0), axis=0, keepdims=True))
        sel = jnp.where(hit, 1.0, sel)
        cur = jnp.where(hit, neg, cur)

    denom = ws[0]
    for w in ws[1:]:
        denom = denom + w
    wk = [w / denom * ROUTED_SCALE for w in ws]
    w_ref[...] = jnp.concatenate(wk + [jnp.zeros((LANES - TOP_K, T), F32)], axis=0).T

    sel_bf = sel.astype(BF16)
    t_row = lax.broadcasted_iota(jnp.int32, (T, T), 0)
    t_col = lax.broadcasted_iota(jnp.int32, (T, T), 1)
    before = (t_row < t_col).astype(BF16)
    cum = jnp.dot(sel_bf, before, preferred_element_type=F32)
    ranks = [jnp.sum(jnp.where(ei == e, cum, 0.0), axis=0, keepdims=True).astype(jnp.int32) for e in eids]
    pad = [jnp.zeros((SUBLANES - TOP_K, T), jnp.int32)]
    eid_ref[...] = jnp.concatenate(eids + pad, axis=0)
    rank_ref[...] = jnp.concatenate(ranks + pad, axis=0)
    cnt_ref[...] = lax.dot_general(jnp.ones((SUBLANES, T), BF16), sel_bf, (((1,), (1,)), ((), ())),
                                   preferred_element_type=F32)


def _route(logits_t, router_bias):
    B, E, S = logits_t.shape
    tr = _tile(S, ROUTE_TILE)
    nt = S // tr
    tok = lambda b, i: (b, 0, i)
    return pl.pallas_call(
        _route_kernel,
        grid=(B, nt),
        in_specs=[
            pl.BlockSpec((None, E, tr), tok),
            pl.BlockSpec((E, 1), lambda b, i: (0, 0)),
        ],
        out_specs=[
            pl.BlockSpec((None, tr, LANES), lambda b, i: (b, i, 0)),
            pl.BlockSpec((None, SUBLANES, tr), tok),
            pl.BlockSpec((None, SUBLANES, tr), tok),
            pl.BlockSpec((None, None, SUBLANES, E), lambda b, i: (b, i, 0, 0)),
        ],
        out_shape=[
            jax.ShapeDtypeStruct((B, S, LANES), F32),
            jax.ShapeDtypeStruct((B, SUBLANES, S), jnp.int32),
            jax.ShapeDtypeStruct((B, SUBLANES, S), jnp.int32),
            jax.ShapeDtypeStruct((B, nt, SUBLANES, E), F32),
        ],
        compiler_params=_params("parallel", "parallel"),
        name="route",
    )(logits_t, router_bias.reshape(E, 1))


def _slots_kernel(eid_ref, rank_ref, base_ref, pos_ref):
    E = base_ref.shape[0]
    T = eid_ref.shape[-1]
    ei = lax.broadcasted_iota(jnp.int32, (E, T), 0)
    base = base_ref[...]
    eid = eid_ref[...]
    rows = [jnp.sum(jnp.where(ei == eid[k:k + 1, :], base, 0), axis=0, keepdims=True)
            for k in range(TOP_K)]
    rows.append(jnp.zeros((SUBLANES - TOP_K, T), jnp.int32))
    pos_ref[...] = jnp.concatenate(rows, axis=0) + rank_ref[...]


def _slots(eid, rank, base):
    B, _, S = eid.shape
    nt, E = base.shape[1], base.shape[2]
    tr = S // nt
    tok = lambda b, i: (b, 0, i)
    return pl.pallas_call(
        _slots_kernel,
        grid=(B, nt),
        in_specs=[
            pl.BlockSpec((None, SUBLANES, tr), tok),
            pl.BlockSpec((None, SUBLANES, tr), tok),
            pl.BlockSpec((None, None, E, 1), lambda b, i: (b, i, 0, 0)),
        ],
        out_specs=pl.BlockSpec((None, SUBLANES, tr), tok),
        out_shape=jax.ShapeDtypeStruct((B, SUBLANES, S), jnp.int32),
        compiler_params=_params("parallel", "parallel"),
        name="slots",
    )(eid, rank, base.reshape(B, nt, E, 1))


def _sc_mesh():
    return plsc.VectorSubcoreMesh(core_axis_name="core", subcore_axis_name="subcore")


def _sc_scatter_rows(src, idx2d, n_out):
    R, W = src.shape
    K = idx2d.shape[0]

    @functools.partial(pl.kernel, out_type=jax.ShapeDtypeStruct((n_out, W), src.dtype), mesh=_sc_mesh(),
                       scratch_types=[], name="sc_dispatch")
    def scatter(x_hbm, i_hbm, o_hbm):
        def body(x_vmem, i_vmem):
            pltpu.sync_copy(x_vmem, o_hbm.at[i_vmem.at[0]])

        pltpu.emit_pipeline(
            body, grid=(R // SC_WINDOW, K),
            in_specs=[pl.BlockSpec((SC_WINDOW, W), lambda i, k: (i, 0)),
                      pl.BlockSpec((1, SC_WINDOW), lambda i, k: (k, i))],
            out_specs=[],
            core_axis_name=("core", "subcore"),
            dimension_semantics=(pltpu.PARALLEL, pltpu.ARBITRARY),
        )(x_hbm, i_hbm)

    return scatter(src, idx2d)


def _sc_gather_rows(table, idx):
    N = idx.shape[0]
    W = table.shape[1]

    @functools.partial(pl.kernel, out_type=jax.ShapeDtypeStruct((N, W), table.dtype), mesh=_sc_mesh(),
                       scratch_types=[], name="sc_combine")
    def gather(t_hbm, i_hbm, o_hbm):
        def body(i_vmem, o_vmem):
            pltpu.sync_copy(t_hbm.at[i_vmem.at[0]], o_vmem)

        pltpu.emit_pipeline(
            body, grid=(N // SC_WINDOW,),
            in_specs=[pl.BlockSpec((1, SC_WINDOW), lambda i: (0, i))],
            out_specs=[pl.BlockSpec((SC_WINDOW, W), lambda i: (i, 0))],
            core_axis_name=("core", "subcore"),
            dimension_semantics=(pltpu.PARALLEL,),
        )(i_hbm, o_hbm)

    return gather(table, idx.reshape(1, N))


def _swiglu_hidden(u, w_in):
    gu = jnp.dot(u, w_in, preferred_element_type=F32)
    d = gu.shape[-1] // 2
    g, up = gu[:, :d], gu[:, d:]
    return g * jax.nn.sigmoid(g) * up


def _experts_kernel(blk_ref, used_ref, xs_ref, wi_ref, wo_ref, ys_ref, wi_bf, wo_bf):
    nb = pl.program_id(0)

    @pl.when((nb == 0) | (blk_ref[nb] != blk_ref[jnp.maximum(nb - 1, 0)]))
    def _():
        wi_bf[...] = wi_ref[...].astype(BF16)
        wo_bf[...] = wo_ref[...].astype(BF16)

    @pl.when(nb < used_ref[0])
    def _():
        x = _load_packed(xs_ref).astype(BF16)
        h = _swiglu_hidden(x, wi_bf[...])
        _store_packed(ys_ref, jnp.dot(h.astype(BF16), wo_bf[...], preferred_element_type=F32))


def _experts(xs, blk_expert, n_used, wei, weo, bm):
    _, P, Wd = xs.shape
    E, D, d2 = wei.shape
    de = weo.shape[1]
    rows = pl.BlockSpec((ROW_SPLIT, bm, Wd), lambda nb, blk, used: (0, jnp.minimum(nb, used[0] - 1), 0))
    return pl.pallas_call(
        _experts_kernel,
        grid_spec=pltpu.PrefetchScalarGridSpec(
            num_scalar_prefetch=2,
            grid=(P // bm,),
            in_specs=[
                rows,
                pl.BlockSpec((None, D, d2), lambda nb, blk, used: (blk[nb], 0, 0)),
                pl.BlockSpec((None, de, D), lambda nb, blk, used: (blk[nb], 0, 0)),
            ],
            out_specs=rows,
            scratch_shapes=[pltpu.VMEM((D, d2), BF16), pltpu.VMEM((de, D), BF16)],
        ),
        out_shape=jax.ShapeDtypeStruct(xs.shape, U32),
        compiler_params=_params("arbitrary"),
        name="experts",
    )(blk_expert, n_used, xs, wei, weo)


def _moe_out_kernel(u_ref, x1_ref, yg_ref, w_ref, mod_ref, wsi_ref, wso_ref, g2_ref, b2_ref, o_ref):
    u = _load_packed(u_ref).astype(BF16)
    hs = _swiglu_hidden(u, wsi_ref[...])
    acc = jnp.dot(hs.astype(BF16), wso_ref[...], preferred_element_type=F32)
    w = w_ref[...]
    for k in range(TOP_K):
        acc = acc + w[:, k:k + 1] * _load_packed(yg_ref.at[k])
    gate2 = mod_ref[5:6, :]
    y = DN_ALPHA * x1_ref[...] + gate2 * acc
    o_ref[...] = _layer_norm(y) * g2_ref[...] + b2_ref[...]


def _moe_out(u2w, x1, yg, w_t, mod, wsi, wso, g2, b2):
    B, S, D = x1.shape
    Wd = u2w.shape[-1]
    tm = _tile(S, MOE_OUT_TILE)
    full = lambda shape: pl.BlockSpec(shape, lambda b, i: tuple(0 for _ in shape))
    return pl.pallas_call(
        _moe_out_kernel,
        grid=(B, S // tm),
        in_specs=[
            pl.BlockSpec((ROW_SPLIT, None, tm, Wd), lambda b, i: (0, b, i, 0)),
            pl.BlockSpec((None, tm, D), lambda b, i: (b, i, 0)),
            pl.BlockSpec((TOP_K, ROW_SPLIT, None, tm, Wd), lambda b, i: (0, 0, b, i, 0)),
            pl.BlockSpec((None, tm, LANES), lambda b, i: (b, i, 0)),
            pl.BlockSpec((None, N_MOD, D), lambda b, i: (b, 0, 0)),
            full(wsi.shape), full(wso.shape), full((1, D)), full((1, D)),
        ],
        out_specs=pl.BlockSpec((None, tm, D), lambda b, i: (b, i, 0)),
        out_shape=jax.ShapeDtypeStruct((B, S, D), F32),
        compiler_params=_params("parallel", "parallel"),
        name="moe_out",
    )(u2w, x1, yg, w_t, mod, wsi, wso, g2, b2)


def _routed_experts(u2w, eid, rank, cnt, wei, weo):
    _, B, S, Wd = u2w.shape
    T = B * S
    E = wei.shape[0]
    bm = _tile(S, EXPERT_BLOCK)
    n_blocks = (T * TOP_K) // bm + E
    P = n_blocks * bm

    cnt_te = cnt[:, :, 0, :].astype(jnp.int32).reshape(-1, E)
    total = jnp.sum(cnt_te, axis=0)
    padded = (total + bm - 1) // bm * bm
    region_end = jnp.cumsum(padded)
    region_start = region_end - padded
    base = region_start[None, :] + jnp.cumsum(cnt_te, axis=0) - cnt_te
    blk_start = jnp.arange(n_blocks, dtype=jnp.int32) * bm
    blk_expert = jnp.minimum(jnp.sum(region_end[None, :] <= blk_start[:, None], axis=1), E - 1).astype(jnp.int32)
    n_used = (region_end[-1] // bm).astype(jnp.int32).reshape(1)

    pos = _slots(eid, rank, base.reshape(B, -1, E))
    pos = jnp.transpose(pos[:, :TOP_K, :], (1, 0, 2)).reshape(TOP_K, T)
    halves = jnp.stack([pos + h * P for h in range(ROW_SPLIT)], axis=1)

    xs = _sc_scatter_rows(u2w.reshape(ROW_SPLIT * T, Wd), halves.reshape(TOP_K, ROW_SPLIT * T), ROW_SPLIT * P)
    ys = _experts(xs.reshape(ROW_SPLIT, P, Wd), blk_expert, n_used, wei, weo, bm)
    yg = _sc_gather_rows(ys.reshape(ROW_SPLIT * P, Wd), halves.reshape(-1))
    return yg.reshape(TOP_K, ROW_SPLIT, B, S, Wd)


def _rope_tables(S):
    rows = S // GRID_W
    row_idx = jnp.repeat(jnp.arange(rows, dtype=F32), GRID_W)
    col_idx = jnp.tile(jnp.arange(GRID_W, dtype=F32), rows)
    inv_freq = ROPE_THETA ** (-jnp.arange(ROPE_FREQS, dtype=F32) / ROPE_FREQS)
    ang = jnp.stack([row_idx[:, None] * inv_freq, col_idx[:, None] * inv_freq], axis=1)
    cos, sin = jnp.cos(ang), jnp.sin(ang)
    cos_h = jnp.stack([cos, cos], axis=2).reshape(S, HEAD_DIM)
    sin_h = jnp.stack([-sin, sin], axis=2).reshape(S, HEAD_DIM)
    reps = LANES // HEAD_DIM
    return jnp.tile(cos_h, (1, reps)), jnp.tile(sin_h, (1, reps))


def _gate_weights(lru_wa, lru_ba, lru_wx, lru_bx, C):
    nb, blk, _ = lru_wa.shape[1:]
    d_lru = nb * blk
    per = C // blk

    def dense(w):
        w = w.reshape(d_lru // C, per, blk, blk)
        eye = jnp.eye(per, dtype=w.dtype)
        return jnp.einsum("hpcd,pq->hpcqd", w, eye).reshape(d_lru // C, C, C)

    w = jnp.concatenate([dense(lru_wa[0]), dense(lru_wx[0]), dense(lru_wa[1]), dense(lru_wx[1])], axis=-1)
    halves = lambda v: v.reshape(d_lru // C, 1, C)
    b = jnp.concatenate([halves(lru_ba[0]), halves(lru_bx[0]), halves(lru_ba[1]), halves(lru_bx[1])], axis=-1)
    return w.astype(BF16), b


def _split_bf16(w):
    hi = w.astype(BF16)
    lo = (w - hi.astype(F32)).astype(BF16)
    return jnp.concatenate([hi, lo], axis=1)


def _encoder(x, c, P):
    B, S, D = x.shape
    mod = _ada(c, P["w_ada"], P["b_ada"]).reshape(B, N_MOD, D)
    cos_t, sin_t = _rope_tables(S)
    qt, k, vt, xl, gl = _in_proj(x, mod, P["w_in"], P["gq"], P["gk"], cos_t, sin_t, P["ones_blk"], P["d_lru"])
    att = _attention(qt, k, vt)
    lru = _lru(xl, gl, P["conv_w"], P["conv_b"], P["w_gate"], P["b_gate"], P["lam"])
    x1, u2w, logits_t = _out_proj(att, lru, x, mod, P["ga"], P["gr"], P["w_out"], P["ln1_g"], P["ln1_b"], P["wr_split"])
    w_t, eid, rank, cnt = _route(logits_t, P["router_bias"])
    yg = _routed_experts(u2w, eid, rank, cnt, P["w_exp_in"], P["w_exp_out"])
    return _moe_out(u2w, x1, yg, w_t, mod, P["w_sh_in"], P["w_sh_out"], P["ln2_g"], P["ln2_b"])


def kernel(x_prompt, x_sample, c_prompt, c_sample, w_ada, b_ada, w_in, q_norm_g, k_norm_g, conv_w, conv_b, lru_wa, lru_ba, lru_wx, lru_bx, lru_lambda, attn_out_g, lru_out_g, w_out, ln1_g, ln1_b, w_router, router_bias, w_exp_in, w_exp_out, w_sh_in, w_sh_out, ln2_g, ln2_b):
    l = 0
    D = x_prompt.shape[-1]
    d_lru = lru_out_g.shape[-1]
    lru_half = 2 * LANES
    w_gate, b_gate = _gate_weights(lru_wa[l], lru_ba[l], lru_wx[l], lru_bx[l], lru_half)
    idx = jnp.arange(2 * LANES)
    ones_blk = (idx[:, None] // HEAD_DIM == idx[None, :] // HEAD_DIM).astype(BF16)
    P = {
        "w_ada": w_ada[l], "b_ada": b_ada[l],
        "w_in": w_in[l].astype(BF16),
        "gq": (jnp.tile(q_norm_g[l], N_Q_HEADS) * (HEAD_DIM ** -0.5 * LOG2_E)).reshape(1, D_ATT),
        "gk": jnp.tile(k_norm_g[l], N_KV_HEADS).reshape(1, D_KV),
        "ones_blk": ones_blk, "d_lru": d_lru,
        "conv_w": conv_w[l], "conv_b": conv_b[l].reshape(1, d_lru),
        "w_gate": w_gate, "b_gate": b_gate, "lam": lru_lambda[l],
        "ga": attn_out_g[l].reshape(1, D_ATT), "gr": lru_out_g[l].reshape(1, d_lru),
        "w_out": w_out[l].astype(BF16),
        "ln1_g": ln1_g[l].reshape(1, D), "ln1_b": ln1_b[l].reshape(1, D),
        "wr_split": _split_bf16(w_router[l]), "router_bias": router_bias[l],
        "w_exp_in": w_exp_in[l], "w_exp_out": w_exp_out[l],
        "w_sh_in": w_sh_in[l].astype(BF16), "w_sh_out": w_sh_out[l].astype(BF16),
        "ln2_g": ln2_g[l].reshape(1, D), "ln2_b": ln2_b[l].reshape(1, D),
    }
    return (_encoder(x_prompt, c_prompt, P), _encoder(x_sample, c_sample, P))
```

```python
import functools

import jax
import jax.numpy as jnp
from jax import lax
from jax.experimental import pallas as pl
from jax.experimental.pallas import tpu as pltpu
from jax.experimental.pallas import tpu_sc as plsc

F32 = jnp.float32
BF16 = jnp.bfloat16
U32 = jnp.uint32
HIGHEST = lax.Precision.HIGHEST

HEAD_DIM = 64
N_Q_HEADS = 8
N_KV_HEADS = 2
Q_PER_KV = N_Q_HEADS // N_KV_HEADS
D_ATT = N_Q_HEADS * HEAD_DIM
D_KV = N_KV_HEADS * HEAD_DIM
N_LRU_BLOCKS = 8
CONV_W = 4
LRU_C = 8.0
GRID_W = 64
ROPE_THETA = 10000.0
ROPE_FREQS = HEAD_DIM // 4
N_EXPERTS = 64
TOP_K = 6
N_GROUPS = 8
GROUP_SIZE = N_EXPERTS // N_GROUPS
TOPK_GROUPS = 4
ROUTED_SCALE = 2.5
N_MOD = 6
EPS = 1e-6
DEPTH = 1
DN_ALPHA = (2.0 * DEPTH) ** 0.25
LOG2_E = 1.4426950408889634
SQRT_FLOOR = 1e-30

LANES = 128
SUBLANES = 8
VMEM_LIMIT_BYTES = 56 * 1024 * 1024

ROW_SPLIT = 2
SC_WINDOW = 128
ATTN_BLOCK = 1024
ATTN_SUB = 512
ROUTE_TILE = 512
EXPERT_BLOCK = 1024
IN_PROJ_TILE = 512
PROJ_TILE = 1024
MOE_OUT_TILE = 512


def _tile(n, pref):
    if n <= pref:
        return n
    for t in range(pref, 0, -1):
        if n % t == 0 and t % SUBLANES == 0:
            return t
    return n


def _params(*sem):
    return pltpu.CompilerParams(dimension_semantics=sem, vmem_limit_bytes=VMEM_LIMIT_BYTES)


def _layer_norm(x):
    mu = jnp.mean(x, axis=-1, keepdims=True)
    xc = x - mu
    var = jnp.mean(xc * xc, axis=-1, keepdims=True)
    return xc * lax.rsqrt(var + EPS)


def _rms(x):
    return x * lax.rsqrt(jnp.mean(x * x, axis=-1, keepdims=True) + EPS)


def _pack_pair(a, b):
    ah = lax.bitcast_convert_type(a.astype(BF16).astype(F32), U32)
    bh = lax.bitcast_convert_type(b.astype(BF16).astype(F32), U32)
    return ah | (bh >> 16)


def _unpack_pair(w):
    a = lax.bitcast_convert_type(w & jnp.uint32(0xFFFF0000), F32)
    b = lax.bitcast_convert_type(w << 16, F32)
    return a, b


def _store_packed(ref, v):
    q = v.shape[-1] // (2 * ROW_SPLIT)
    for h in range(ROW_SPLIT):
        ref[h] = _pack_pair(v[:, 2 * h * q:(2 * h + 1) * q], v[:, (2 * h + 1) * q:(2 * h + 2) * q])


def _load_packed(ref):
    parts = []
    for h in range(ROW_SPLIT):
        parts.extend(_unpack_pair(ref[h]))
    return jnp.concatenate(parts, axis=-1)


def _ada_kernel(c_ref, w_ref, b_ref, o_ref):
    c = c_ref[...]
    sc = c * jax.nn.sigmoid(c)
    o_ref[...] = jnp.dot(sc, w_ref[...], precision=HIGHEST, preferred_element_type=F32) + b_ref[...]


def _ada(c, w_ada, b_ada):
    B, D = c.shape
    N = w_ada.shape[1]
    tn = _tile(N, 1024)
    return pl.pallas_call(
        _ada_kernel,
        grid=(N // tn,),
        in_specs=[
            pl.BlockSpec((B, D), lambda j: (0, 0)),
            pl.BlockSpec((D, tn), lambda j: (0, j)),
            pl.BlockSpec((1, tn), lambda j: (0, j)),
        ],
        out_specs=pl.BlockSpec((B, tn), lambda j: (0, j)),
        out_shape=jax.ShapeDtypeStruct((B, N), F32),
        compiler_params=_params("arbitrary"),
        name="ada",
    )(c, w_ada, b_ada.reshape(1, N))


def _group_sumsq(t, ones_blk):
    sq = t * t
    hi = sq.astype(BF16)
    lo = (sq - hi.astype(F32)).astype(BF16)
    return (jnp.dot(hi, ones_blk, preferred_element_type=F32)
            + jnp.dot(lo, ones_blk, preferred_element_type=F32))


def _rope(t, cos, sin_signed, first_half):
    fwd = pltpu.roll(t, LANES - ROPE_FREQS, axis=1)
    bwd = pltpu.roll(t, ROPE_FREQS, axis=1)
    return t * cos + jnp.where(first_half, fwd, bwd) * sin_signed


def _in_proj_kernel(x_ref, mod_ref, w_ref, gq_ref, gk_ref, cos_ref, sin_ref, ones_ref,
                    qt_ref, k_ref, vt_ref, xl_ref, gl_ref):
    x = x_ref[...]
    shift1 = mod_ref[0:1, :]
    scale1 = mod_ref[1:2, :]
    u = _layer_norm(x) * (1.0 + scale1) + shift1
    proj = jnp.dot(u.astype(BF16), w_ref[...], preferred_element_type=F32)

    cos = cos_ref[...]
    sin_signed = sin_ref[...]
    lane = lax.broadcasted_iota(jnp.int32, cos.shape, 1)
    first_half = (lane % (2 * ROPE_FREQS)) < ROPE_FREQS
    ones2 = ones_ref[...]
    ones1 = ones2[0:LANES, 0:LANES]

    def norm_rope(t, g, ones_blk):
        ss = _group_sumsq(t, ones_blk)
        tn = t * lax.rsqrt(ss * (1.0 / HEAD_DIM) + EPS) * g
        chunks = [_rope(tn[:, c * LANES:(c + 1) * LANES], cos, sin_signed, first_half)
                  for c in range(t.shape[1] // LANES)]
        return chunks

    w2 = 2 * LANES
    for half in range(D_ATT // w2):
        t = proj[:, half * w2:(half + 1) * w2]
        chunks = norm_rope(t, gq_ref[:, half * w2:(half + 1) * w2], ones2)
        for c, ch in enumerate(chunks):
            lo = half * w2 + c * LANES
            qt_ref[lo:lo + LANES, :] = ch.T.astype(BF16)
    k = proj[:, D_ATT:D_ATT + D_KV]
    (kr,) = norm_rope(k, gk_ref[...], ones1)
    k_ref[...] = kr.astype(BF16)
    vt_ref[...] = proj[:, D_ATT + D_KV:D_ATT + 2 * D_KV].T.astype(BF16)
    o = D_ATT + 2 * D_KV
    d_lru = xl_ref.shape[-1]
    xl_ref[...] = proj[:, o:o + d_lru]
    gl_ref[...] = proj[:, o + d_lru:o + 2 * d_lru]


def _in_proj(x, mod, w_in_bf, gq, gk, cos_t, sin_t, ones_blk, d_lru):
    B, S, D = x.shape
    tm = _tile(S, IN_PROJ_TILE)
    d_in = w_in_bf.shape[1]
    full = lambda shape: pl.BlockSpec(shape, lambda b, i: tuple(0 for _ in shape))
    return pl.pallas_call(
        _in_proj_kernel,
        grid=(B, S // tm),
        in_specs=[
            pl.BlockSpec((None, tm, D), lambda b, i: (b, i, 0)),
            pl.BlockSpec((None, N_MOD, D), lambda b, i: (b, 0, 0)),
            full((D, d_in)),
            full((1, D_ATT)),
            full((1, D_KV)),
            pl.BlockSpec((tm, LANES), lambda b, i: (i, 0)),
            pl.BlockSpec((tm, LANES), lambda b, i: (i, 0)),
            full((2 * LANES, 2 * LANES)),
        ],
        out_specs=[
            pl.BlockSpec((None, D_ATT, tm), lambda b, i: (b, 0, i)),
            pl.BlockSpec((None, tm, D_KV), lambda b, i: (b, i, 0)),
            pl.BlockSpec((None, D_KV, tm), lambda b, i: (b, 0, i)),
            pl.BlockSpec((None, tm, d_lru), lambda b, i: (b, i, 0)),
            pl.BlockSpec((None, tm, d_lru), lambda b, i: (b, i, 0)),
        ],
        out_shape=[
            jax.ShapeDtypeStruct((B, D_ATT, S), BF16),
            jax.ShapeDtypeStruct((B, S, D_KV), BF16),
            jax.ShapeDtypeStruct((B, D_KV, S), BF16),
            jax.ShapeDtypeStruct((B, S, d_lru), F32),
            jax.ShapeDtypeStruct((B, S, d_lru), F32),
        ],
        compiler_params=_params("parallel", "parallel"),
        name="in_proj",
    )(x, mod, w_in_bf, gq, gk, cos_t, sin_t, ones_blk)


def _attn_kernel(qt_ref, k_ref, vt_ref, o_ref, *, sub):
    tq = qt_ref.shape[-1]
    k = k_ref[...]
    zeros = jnp.zeros((HEAD_DIM, sub), BF16)
    ones_rows = jnp.ones((2 * SUBLANES, k.shape[0]), BF16)
    units = [(q0, pair) for q0 in range(0, tq, sub) for pair in range(N_Q_HEADS // 2)]

    def scores(q0, pair):
        j = (2 * pair) // Q_PER_KV
        cols = []
        for h in (2 * pair, 2 * pair + 1):
            qt = qt_ref[h * HEAD_DIM:(h + 1) * HEAD_DIM, q0:q0 + sub]
            cols.append(jnp.concatenate([qt, zeros] if j == 0 else [zeros, qt], axis=0))
        return jnp.dot(k, jnp.concatenate(cols, axis=1), preferred_element_type=F32)

    def finish(q0, pair, st):
        j = (2 * pair) // Q_PER_KV
        vt = jnp.concatenate([vt_ref[j * HEAD_DIM:(j + 1) * HEAD_DIM, :], ones_rows], axis=0)
        m = jnp.max(st, axis=0, keepdims=True)
        p = jnp.exp2(st - m)
        ol = jnp.dot(vt, p.astype(BF16), preferred_element_type=F32)
        ot = ol[:HEAD_DIM, :] / ol[HEAD_DIM:HEAD_DIM + 1, :]
        o2 = jnp.concatenate([ot[:, :sub], ot[:, sub:]], axis=0)
        o_ref[q0:q0 + sub, 2 * pair * HEAD_DIM:(2 * pair + 2) * HEAD_DIM] = o2.T.astype(BF16)

    st = scores(*units[0])
    for n, unit in enumerate(units):
        st_next = scores(*units[n + 1]) if n + 1 < len(units) else None
        finish(*unit, st)
        st = st_next


def _attention(qt, k, vt):
    B, _, S = qt.shape
    tq = _tile(S, ATTN_BLOCK)
    sub = _tile(tq, ATTN_SUB)
    return pl.pallas_call(
        functools.partial(_attn_kernel, sub=sub),
        grid=(B, S // tq),
        in_specs=[
            pl.BlockSpec((None, D_ATT, tq), lambda b, i: (b, 0, i)),
            pl.BlockSpec((None, S, D_KV), lambda b, i: (b, 0, 0)),
            pl.BlockSpec((None, D_KV, S), lambda b, i: (b, 0, 0)),
        ],
        out_specs=pl.BlockSpec((None, tq, D_ATT), lambda b, i: (b, i, 0)),
        out_shape=jax.ShapeDtypeStruct((B, S, D_ATT), BF16),
        compiler_params=_params("parallel", "parallel"),
        name="attn",
    )(qt, k, vt)


def _group_scan(a, b, reverse):
    sub = lax.broadcasted_iota(jnp.int32, a.shape, 1)
    for k in (1, 2, 4):
        if reverse:
            shift, valid = SUBLANES - k, sub < SUBLANES - k
        else:
            shift, valid = k, sub >= k
        a_n = jnp.where(valid, pltpu.roll(a, shift, axis=1), 1.0)
        b_n = jnp.where(valid, pltpu.roll(b, shift, axis=1), 0.0)
        b = a * b_n + b
        a = a * a_n
    return a, b


def _lru_kernel(xl_ref, gl_ref, cw_ref, cb_ref, w_ref, bias_ref, lam_ref, o_ref,
                xc_ref, af_ref, bf_ref, ab_ref, bb_ref, *, chunk):
    S, C = xl_ref.shape
    G = S // SUBLANES
    xl = xl_ref[...]
    row = lax.broadcasted_iota(jnp.int32, (S, 1), 0)
    xm2 = jnp.where(row >= 2, pltpu.roll(xl, 2, axis=0), 0.0)
    xm1 = jnp.where(row >= 1, pltpu.roll(xl, 1, axis=0), 0.0)
    xp1 = jnp.where(row < S - 1, pltpu.roll(xl, S - 1, axis=0), 0.0)
    xc_ref[...] = (xm2 * cw_ref[0:1, :] + xm1 * cw_ref[1:2, :] + xl * cw_ref[2:3, :]
                   + xp1 * cw_ref[3:4, :] + cb_ref[...])

    lam = lam_ref[...]
    nlam = -lam
    softplus = jnp.maximum(nlam, 0.0) + jnp.log(1.0 + jnp.exp(-jnp.abs(nlam)))
    decay_log2 = (-LRU_C * LOG2_E) * softplus
    cg = chunk // SUBLANES

    def gates(ci, carry):
        r0 = pl.multiple_of(ci * chunk, chunk)
        g0 = pl.multiple_of(ci * cg, cg)
        xc = xc_ref[pl.ds(r0, chunk), :]
        g = jnp.dot(xc.astype(BF16), w_ref[...], preferred_element_type=F32) + bias_ref[...]
        for d, (a_ref, b_ref) in enumerate(((af_ref, bf_ref), (ab_ref, bb_ref))):
            r = jax.nn.sigmoid(g[:, (2 * d) * C:(2 * d + 1) * C])
            i = jax.nn.sigmoid(g[:, (2 * d + 1) * C:(2 * d + 2) * C])
            a = jnp.exp2(r * decay_log2[d:d + 1, :])
            gain_sq = 1.0 - a * a
            gain = gain_sq * lax.rsqrt(jnp.maximum(gain_sq, SQRT_FLOOR))
            bt = gain * i * xc
            a3, b3 = _group_scan(a.reshape(cg, SUBLANES, C), bt.reshape(cg, SUBLANES, C), d == 1)
            a_ref[pl.ds(g0, cg), :, :] = a3
            b_ref[pl.ds(g0, cg), :, :] = b3
        return carry

    lax.fori_loop(0, S // chunk, gates, 0)

    def step(gi, carry):
        hf, hb = carry
        hf_new = af_ref[gi] * hf + bf_ref[gi]
        bf_ref[gi] = hf_new
        gb = G - 1 - gi
        hb_new = ab_ref[gb] * hb + bb_ref[gb]
        bb_ref[gb] = hb_new
        return hf_new[SUBLANES - 1:SUBLANES, :], hb_new[0:1, :]

    zero = jnp.zeros((1, C), F32)
    lax.fori_loop(0, G, step, (zero, zero), unroll=4)

    def emit(ci, carry):
        r0 = pl.multiple_of(ci * chunk, chunk)
        g0 = pl.multiple_of(ci * cg, cg)
        h = (bf_ref[pl.ds(g0, cg), :, :] + bb_ref[pl.ds(g0, cg), :, :]).reshape(chunk, C)
        gl = gl_ref[pl.ds(r0, chunk), :]
        o_ref[pl.ds(r0, chunk), :] = (jax.nn.gelu(gl) * h).astype(BF16)
        return carry

    lax.fori_loop(0, S // chunk, emit, 0)


def _lru(xl, gl, conv_w, conv_b, w_gate, b_gate, lam):
    B, S, d_lru = xl.shape
    nh, C, _ = w_gate.shape
    chunk = _tile(S, 512)
    G = S // SUBLANES
    col = lambda b, h: (b, 0, h)
    return pl.pallas_call(
        functools.partial(_lru_kernel, chunk=chunk),
        grid=(B, nh),
        in_specs=[
            pl.BlockSpec((None, S, C), col),
            pl.BlockSpec((None, S, C), col),
            pl.BlockSpec((CONV_W, C), lambda b, h: (0, h)),
            pl.BlockSpec((1, C), lambda b, h: (0, h)),
            pl.BlockSpec((None, C, 4 * C), lambda b, h: (h, 0, 0)),
            pl.BlockSpec((None, 1, 4 * C), lambda b, h: (h, 0, 0)),
            pl.BlockSpec((2, C), lambda b, h: (0, h)),
        ],
        out_specs=pl.BlockSpec((None, S, C), col),
        out_shape=jax.ShapeDtypeStruct((B, S, d_lru), BF16),
        scratch_shapes=[pltpu.VMEM((S, C), F32)] + [pltpu.VMEM((G, SUBLANES, C), F32)] * 4,
        compiler_params=_params("parallel", "parallel"),
        name="lru",
    )(xl, gl, conv_w, conv_b, w_gate, b_gate, lam)


def _out_proj_kernel(att_ref, lru_ref, x_ref, mod_ref, ga_ref, gr_ref, w_ref, g1_ref, b1_ref, wr_ref,
                     x1_ref, u2_ref, lg_ref):
    d_att = att_ref.shape[-1]
    an = _rms(att_ref[...].astype(F32)) * ga_ref[...]
    rn = _rms(lru_ref[...].astype(F32)) * gr_ref[...]
    mix = (jnp.dot(an.astype(BF16), w_ref[0:d_att, :], preferred_element_type=F32)
           + jnp.dot(rn.astype(BF16), w_ref[d_att:, :], preferred_element_type=F32))
    gate1 = mod_ref[2:3, :]
    shift2 = mod_ref[3:4, :]
    scale2 = mod_ref[4:5, :]
    x1 = _layer_norm(DN_ALPHA * x_ref[...] + gate1 * mix) * g1_ref[...] + b1_ref[...]
    x1_ref[...] = x1
    u2 = _layer_norm(x1) * (1.0 + scale2) + shift2
    _store_packed(u2_ref, u2)
    E = lg_ref.shape[0]
    u_hi = u2.astype(BF16)
    u_lo = (u2 - u_hi.astype(F32)).astype(BF16)
    parts = (jnp.dot(u_hi, wr_ref[...], preferred_element_type=F32)
             + jnp.dot(u_lo, wr_ref[...], preferred_element_type=F32))
    parts_t = parts.T
    lg_ref[...] = parts_t[:E, :] + parts_t[E:, :]


def _out_proj(att, lru, x, mod, ga, gr, w_out_bf, g1, b1, wr_split):
    B, S, D = x.shape
    tm = _tile(S, PROJ_TILE)
    d_att, d_lru = att.shape[-1], lru.shape[-1]
    full = lambda shape: pl.BlockSpec(shape, lambda b, i: tuple(0 for _ in shape))
    row = lambda w: pl.BlockSpec((None, tm, w), lambda b, i: (b, i, 0))
    return pl.pallas_call(
        _out_proj_kernel,
        grid=(B, S // tm),
        in_specs=[
            row(d_att), row(d_lru), row(D),
            pl.BlockSpec((None, N_MOD, D), lambda b, i: (b, 0, 0)),
            full((1, d_att)), full((1, d_lru)), full((d_att + d_lru, D)),
            full((1, D)), full((1, D)), full((D, 2 * N_EXPERTS)),
        ],
        out_specs=[row(D),
                   pl.BlockSpec((ROW_SPLIT, None, tm, D // (2 * ROW_SPLIT)), lambda b, i: (0, b, i, 0)),
                   pl.BlockSpec((None, N_EXPERTS, tm), lambda b, i: (b, 0, i))],
        out_shape=[
            jax.ShapeDtypeStruct((B, S, D), F32),
            jax.ShapeDtypeStruct((ROW_SPLIT, B, S, D // (2 * ROW_SPLIT)), U32),
            jax.ShapeDtypeStruct((B, N_EXPERTS, S), F32),
        ],
        compiler_params=_params("parallel", "parallel"),
        name="out_proj",
    )(att, lru, x, mod, ga, gr, w_out_bf, g1, b1, wr_split)


def _route_kernel(lg_ref, bias_ref, w_ref, eid_ref, rank_ref, cnt_ref):
    E, T = lg_ref.shape
    s = jax.nn.sigmoid(lg_ref[...])
    choice = s + bias_ref[...]
    neg = -jnp.inf

    c3 = choice.reshape(N_GROUPS, GROUP_SIZE, T)
    mem = lax.broadcasted_iota(jnp.int32, c3.shape, 1)
    m1 = jnp.max(c3, axis=1, keepdims=True)
    first = jnp.min(jnp.where(c3 == m1, mem, GROUP_SIZE), axis=1, keepdims=True)
    m2 = jnp.max(jnp.where(mem == first, neg, c3), axis=1, keepdims=True)
    gs = (m1 + m2).reshape(N_GROUPS, T)

    gi = lax.broadcasted_iota(jnp.int32, gs.shape, 0)
    rank = jnp.zeros(gs.shape, jnp.int32)
    for g2 in range(N_GROUPS):
        other = gs[g2:g2 + 1, :]
        beats = (other > gs) | ((other == gs) & (gi > g2))
        rank = rank + beats.astype(jnp.int32)
    gmask = (rank < TOPK_GROUPS).astype(F32).reshape(N_GROUPS, 1, T)
    emask = jnp.broadcast_to(gmask, (N_GROUPS, GROUP_SIZE, T)).reshape(E, T) > 0.5
    cur = jnp.where(emask, choice, neg)

    ei = lax.broadcasted_iota(jnp.int32, (E, T), 0)
    sel = jnp.zeros((E, T), F32)
    eids, ws = [], []
    for _ in range(TOP_K):
        m = jnp.max(cur, axis=0, keepdims=True)
        first_e = jnp.min(jnp.where(cur == m, ei, E), axis=0, keepdims=True)
        hit = ei == first_e
        eids.append(first_e)
        ws.append(jnp.sum(jnp.where(hit, s, 0.0), axis=0, keepdims=True))
        sel = jnp.where(hit, 1.0, sel)
        cur = jnp.where(hit, neg, cur)

    denom = ws[0]
    for w in ws[1:]:
        denom = denom + w
    wk = [w / denom * ROUTED_SCALE for w in ws]
    w_ref[...] = jnp.concatenate(wk + [jnp.zeros((LANES - TOP_K, T), F32)], axis=0).T

    sel_bf = sel.astype(BF16)
    t_row = lax.broadcasted_iota(jnp.int32, (T, T), 0)
    t_col = lax.broadcasted_iota(jnp.int32, (T, T), 1)
    before = (t_row < t_col).astype(BF16)
    cum = jnp.dot(sel_bf, before, preferred_element_type=F32)
    ranks = [jnp.sum(jnp.where(ei == e, cum, 0.0), axis=0, keepdims=True).astype(jnp.int32) for e in eids]
    pad = [jnp.zeros((SUBLANES - TOP_K, T), jnp.int32)]
    eid_ref[...] = jnp.concatenate(eids + pad, axis=0)
    rank_ref[...] = jnp.concatenate(ranks + pad, axis=0)
    cnt_ref[...] = lax.dot_general(jnp.ones((SUBLANES, T), BF16), sel_bf, (((1,), (1,)), ((), ())),
                                   preferred_element_type=F32)


def _route(logits_t, router_bias):
    B, E, S = logits_t.shape
    tr = _tile(S, ROUTE_TILE)
    nt = S // tr
    tok = lambda b, i: (b, 0, i)
    return pl.pallas_call(
        _route_kernel,
        grid=(B, nt),
        in_specs=[
            pl.BlockSpec((None, E, tr), tok),
            pl.BlockSpec((E, 1), lambda b, i: (0, 0)),
        ],
        out_specs=[
            pl.BlockSpec((None, tr, LANES), lambda b, i: (b, i, 0)),
            pl.BlockSpec((None, SUBLANES, tr), tok),
            pl.BlockSpec((None, SUBLANES, tr), tok),
            pl.BlockSpec((None, None, SUBLANES, E), lambda b, i: (b, i, 0, 0)),
        ],
        out_shape=[
            jax.ShapeDtypeStruct((B, S, LANES), F32),
            jax.ShapeDtypeStruct((B, SUBLANES, S), jnp.int32),
            jax.ShapeDtypeStruct((B, SUBLANES, S), jnp.int32),
            jax.ShapeDtypeStruct((B, nt, SUBLANES, E), F32),
        ],
        compiler_params=_params("parallel", "parallel"),
        name="route",
    )(logits_t, router_bias.reshape(E, 1))


def _slots_kernel(eid_ref, rank_ref, base_ref, pos_ref):
    nt, E, _ = base_ref.shape
    T = eid_ref.shape[-1] // nt
    ei = lax.broadcasted_iota(jnp.int32, (E, T), 0)
    pad = jnp.zeros((SUBLANES - TOP_K, T), jnp.int32)
    for t in range(nt):
        base = base_ref[t]
        eid = eid_ref[:, t * T:(t + 1) * T]
        rows = [jnp.sum(jnp.where(ei == eid[k:k + 1, :], base, 0), axis=0, keepdims=True)
                for k in range(TOP_K)]
        pos_ref[:, t * T:(t + 1) * T] = jnp.concatenate(rows + [pad], axis=0) + rank_ref[:, t * T:(t + 1) * T]


def _slots(eid, rank, base):
    B, _, S = eid.shape
    nt, E = base.shape[1], base.shape[2]
    seq = lambda b: (b, 0, 0)
    return pl.pallas_call(
        _slots_kernel,
        grid=(B,),
        in_specs=[
            pl.BlockSpec((None, SUBLANES, S), seq),
            pl.BlockSpec((None, SUBLANES, S), seq),
            pl.BlockSpec((None, nt, E, 1), lambda b: (b, 0, 0, 0)),
        ],
        out_specs=pl.BlockSpec((None, SUBLANES, S), seq),
        out_shape=jax.ShapeDtypeStruct((B, SUBLANES, S), jnp.int32),
        compiler_params=_params("parallel"),
        name="slots",
    )(eid, rank, base.reshape(B, nt, E, 1))


def _sc_mesh():
    return plsc.VectorSubcoreMesh(core_axis_name="core", subcore_axis_name="subcore")


def _sc_scatter_rows(src, idx2d, n_out):
    R, W = src.shape
    K = idx2d.shape[0]

    @functools.partial(pl.kernel, out_type=jax.ShapeDtypeStruct((n_out, W), src.dtype), mesh=_sc_mesh(),
                       scratch_types=[], name="sc_dispatch")
    def scatter(x_hbm, i_hbm, o_hbm):
        def body(x_vmem, i_vmem):
            pltpu.sync_copy(x_vmem, o_hbm.at[i_vmem.at[0]])

        pltpu.emit_pipeline(
            body, grid=(R // SC_WINDOW, K),
            in_specs=[pl.BlockSpec((SC_WINDOW, W), lambda i, k: (i, 0)),
                      pl.BlockSpec((1, SC_WINDOW), lambda i, k: (k, i))],
            out_specs=[],
            core_axis_name=("core", "subcore"),
            dimension_semantics=(pltpu.PARALLEL, pltpu.ARBITRARY),
        )(x_hbm, i_hbm)

    return scatter(src, idx2d)


def _sc_gather_rows(table, idx):
    N = idx.shape[0]
    W = table.shape[1]

    @functools.partial(pl.kernel, out_type=jax.ShapeDtypeStruct((N, W), table.dtype), mesh=_sc_mesh(),
                       scratch_types=[], name="sc_combine")
    def gather(t_hbm, i_hbm, o_hbm):
        def body(i_vmem, o_vmem):
            pltpu.sync_copy(t_hbm.at[i_vmem.at[0]], o_vmem)

        pltpu.emit_pipeline(
            body, grid=(N // SC_WINDOW,),
            in_specs=[pl.BlockSpec((1, SC_WINDOW), lambda i: (0, i))],
            out_specs=[pl.BlockSpec((SC_WINDOW, W), lambda i: (i, 0))],
            core_axis_name=("core", "subcore"),
            dimension_semantics=(pltpu.PARALLEL,),
        )(i_hbm, o_hbm)

    return gather(table, idx.reshape(1, N))


def _swiglu_hidden(u, w_in):
    gu = jnp.dot(u, w_in, preferred_element_type=F32)
    d = gu.shape[-1] // 2
    g, up = gu[:, :d], gu[:, d:]
    return g * jax.nn.sigmoid(g) * up


def _experts_kernel(blk_ref, used_ref, xs_ref, wi_ref, wo_ref, ys_ref, wi_bf, wo_bf):
    nb = pl.program_id(0)

    @pl.when((nb == 0) | (blk_ref[nb] != blk_ref[jnp.maximum(nb - 1, 0)]))
    def _():
        wi_bf[...] = wi_ref[...].astype(BF16)
        wo_bf[...] = wo_ref[...].astype(BF16)

    @pl.when(nb < used_ref[0])
    def _():
        x = _load_packed(xs_ref).astype(BF16)
        h = _swiglu_hidden(x, wi_bf[...])
        _store_packed(ys_ref, jnp.dot(h.astype(BF16), wo_bf[...], preferred_element_type=F32))


def _experts(xs, blk_expert, n_used, wei, weo, bm):
    _, P, Wd = xs.shape
    E, D, d2 = wei.shape
    de = weo.shape[1]
    rows = pl.BlockSpec((ROW_SPLIT, bm, Wd), lambda nb, blk, used: (0, jnp.minimum(nb, used[0] - 1), 0))
    return pl.pallas_call(
        _experts_kernel,
        grid_spec=pltpu.PrefetchScalarGridSpec(
            num_scalar_prefetch=2,
            grid=(P // bm,),
            in_specs=[
                rows,
                pl.BlockSpec((None, D, d2), lambda nb, blk, used: (blk[nb], 0, 0)),
                pl.BlockSpec((None, de, D), lambda nb, blk, used: (blk[nb], 0, 0)),
            ],
            out_specs=rows,
            scratch_shapes=[pltpu.VMEM((D, d2), BF16), pltpu.VMEM((de, D), BF16)],
        ),
        out_shape=jax.ShapeDtypeStruct(xs.shape, U32),
        compiler_params=_params("arbitrary"),
        name="experts",
    )(blk_expert, n_used, xs, wei, weo)


def _moe_out_kernel(u_ref, x1_ref, yg_ref, w_ref, mod_ref, wsi_ref, wso_ref, g2_ref, b2_ref, o_ref):
    u = _load_packed(u_ref).astype(BF16)
    hs = _swiglu_hidden(u, wsi_ref[...])
    acc = jnp.dot(hs.astype(BF16), wso_ref[...], preferred_element_type=F32)
    w = w_ref[...]
    for k in range(TOP_K):
        acc = acc + w[:, k:k + 1] * _load_packed(yg_ref.at[k])
    gate2 = mod_ref[5:6, :]
    y = DN_ALPHA * x1_ref[...] + gate2 * acc
    o_ref[...] = _layer_norm(y) * g2_ref[...] + b2_ref[...]


def _moe_out(u2w, x1, yg, w_t, mod, wsi, wso, g2, b2):
    B, S, D = x1.shape
    Wd = u2w.shape[-1]
    tm = _tile(S, MOE_OUT_TILE)
    full = lambda shape: pl.BlockSpec(shape, lambda b, i: tuple(0 for _ in shape))
    return pl.pallas_call(
        _moe_out_kernel,
        grid=(B, S // tm),
        in_specs=[
            pl.BlockSpec((ROW_SPLIT, None, tm, Wd), lambda b, i: (0, b, i, 0)),
            pl.BlockSpec((None, tm, D), lambda b, i: (b, i, 0)),
            pl.BlockSpec((TOP_K, ROW_SPLIT, None, tm, Wd), lambda b, i: (0, 0, b, i, 0)),
            pl.BlockSpec((None, tm, LANES), lambda b, i: (b, i, 0)),
            pl.BlockSpec((None, N_MOD, D), lambda b, i: (b, 0, 0)),
            full(wsi.shape), full(wso.shape), full((1, D)), full((1, D)),
        ],
        out_specs=pl.BlockSpec((None, tm, D), lambda b, i: (b, i, 0)),
        out_shape=jax.ShapeDtypeStruct((B, S, D), F32),
        compiler_params=_params("parallel", "parallel"),
        name="moe_out",
    )(u2w, x1, yg, w_t, mod, wsi, wso, g2, b2)


def _routed_experts(u2w, eid, rank, cnt, wei, weo):
    _, B, S, Wd = u2w.shape
    T = B * S
    E = wei.shape[0]
    bm = _tile(S, EXPERT_BLOCK)
    n_blocks = (T * TOP_K) // bm + E
    P = n_blocks * bm

    cnt_te = cnt[:, :, 0, :].astype(jnp.int32).reshape(-1, E)
    total = jnp.sum(cnt_te, axis=0)
    padded = (total + bm - 1) // bm * bm
    region_end = jnp.cumsum(padded)
    region_start = region_end - padded
    base = region_start[None, :] + jnp.cumsum(cnt_te, axis=0) - cnt_te
    blk_start = jnp.arange(n_blocks, dtype=jnp.int32) * bm
    blk_expert = jnp.minimum(jnp.sum(region_end[None, :] <= blk_start[:, None], axis=1), E - 1).astype(jnp.int32)
    n_used = (region_end[-1] // bm).astype(jnp.int32).reshape(1)

    pos = _slots(eid, rank, base.reshape(B, -1, E))
    pos = jnp.transpose(pos[:, :TOP_K, :], (1, 0, 2)).reshape(TOP_K, T)
    halves = jnp.stack([pos + h * P for h in range(ROW_SPLIT)], axis=1)

    xs = _sc_scatter_rows(u2w.reshape(ROW_SPLIT * T, Wd), halves.reshape(TOP_K, ROW_SPLIT * T), ROW_SPLIT * P)
    ys = _experts(xs.reshape(ROW_SPLIT, P, Wd), blk_expert, n_used, wei, weo, bm)
    yg = _sc_gather_rows(ys.reshape(ROW_SPLIT * P, Wd), halves.reshape(-1))
    return yg.reshape(TOP_K, ROW_SPLIT, B, S, Wd)


def _rope_tables(S):
    rows = S // GRID_W
    row_idx = jnp.repeat(jnp.arange(rows, dtype=F32), GRID_W)
    col_idx = jnp.tile(jnp.arange(GRID_W, dtype=F32), rows)
    inv_freq = ROPE_THETA ** (-jnp.arange(ROPE_FREQS, dtype=F32) / ROPE_FREQS)
    ang = jnp.stack([row_idx[:, None] * inv_freq, col_idx[:, None] * inv_freq], axis=1)
    cos, sin = jnp.cos(ang), jnp.sin(ang)
    cos_h = jnp.stack([cos, cos], axis=2).reshape(S, HEAD_DIM)
    sin_h = jnp.stack([-sin, sin], axis=2).reshape(S, HEAD_DIM)
    reps = LANES // HEAD_DIM
    return jnp.tile(cos_h, (1, reps)), jnp.tile(sin_h, (1, reps))


def _gate_weights(lru_wa, lru_ba, lru_wx, lru_bx, C):
    nb, blk, _ = lru_wa.shape[1:]
    d_lru = nb * blk
    per = C // blk

    def dense(w):
        w = w.reshape(d_lru // C, per, blk, blk)
        eye = jnp.eye(per, dtype=w.dtype)
        return jnp.einsum("hpcd,pq->hpcqd", w, eye).reshape(d_lru // C, C, C)

    w = jnp.concatenate([dense(lru_wa[0]), dense(lru_wx[0]), dense(lru_wa[1]), dense(lru_wx[1])], axis=-1)
    halves = lambda v: v.reshape(d_lru // C, 1, C)
    b = jnp.concatenate([halves(lru_ba[0]), halves(lru_bx[0]), halves(lru_ba[1]), halves(lru_bx[1])], axis=-1)
    return w.astype(BF16), b


def _split_bf16(w):
    hi = w.astype(BF16)
    lo = (w - hi.astype(F32)).astype(BF16)
    return jnp.concatenate([hi, lo], axis=1)


def _encoder(x, c, P):
    B, S, D = x.shape
    mod = _ada(c, P["w_ada"], P["b_ada"]).reshape(B, N_MOD, D)
    cos_t, sin_t = _rope_tables(S)
    qt, k, vt, xl, gl = _in_proj(x, mod, P["w_in"], P["gq"], P["gk"], cos_t, sin_t, P["ones_blk"], P["d_lru"])
    att = _attention(qt, k, vt)
    lru = _lru(xl, gl, P["conv_w"], P["conv_b"], P["w_gate"], P["b_gate"], P["lam"])
    x1, u2w, logits_t = _out_proj(att, lru, x, mod, P["ga"], P["gr"], P["w_out"], P["ln1_g"], P["ln1_b"], P["wr_split"])
    w_t, eid, rank, cnt = _route(logits_t, P["router_bias"])
    yg = _routed_experts(u2w, eid, rank, cnt, P["w_exp_in"], P["w_exp_out"])
    return _moe_out(u2w, x1, yg, w_t, mod, P["w_sh_in"], P["w_sh_out"], P["ln2_g"], P["ln2_b"])


def kernel(x_prompt, x_sample, c_prompt, c_sample, w_ada, b_ada, w_in, q_norm_g, k_norm_g, conv_w, conv_b, lru_wa, lru_ba, lru_wx, lru_bx, lru_lambda, attn_out_g, lru_out_g, w_out, ln1_g, ln1_b, w_router, router_bias, w_exp_in, w_exp_out, w_sh_in, w_sh_out, ln2_g, ln2_b):
    l = 0
    D = x_prompt.shape[-1]
    d_lru = lru_out_g.shape[-1]
    lru_half = 2 * LANES
    w_gate, b_gate = _gate_weights(lru_wa[l], lru_ba[l], lru_wx[l], lru_bx[l], lru_half)
    idx = jnp.arange(2 * LANES)
    ones_blk = (idx[:, None] // HEAD_DIM == idx[None, :] // HEAD_DIM).astype(BF16)
    P = {
        "w_ada": w_ada[l], "b_ada": b_ada[l],
        "w_in": w_in[l].astype(BF16),
        "gq": (jnp.tile(q_norm_g[l], N_Q_HEADS) * (HEAD_DIM ** -0.5 * LOG2_E)).reshape(1, D_ATT),
        "gk": jnp.tile(k_norm_g[l], N_KV_HEADS).reshape(1, D_KV),
        "ones_blk": ones_blk, "d_lru": d_lru,
        "conv_w": conv_w[l], "conv_b": conv_b[l].reshape(1, d_lru),
        "w_gate": w_gate, "b_gate": b_gate, "lam": lru_lambda[l],
        "ga": attn_out_g[l].reshape(1, D_ATT), "gr": lru_out_g[l].reshape(1, d_lru),
        "w_out": w_out[l].astype(BF16),
        "ln1_g": ln1_g[l].reshape(1, D), "ln1_b": ln1_b[l].reshape(1, D),
        "wr_split": _split_bf16(w_router[l]), "router_bias": router_bias[l],
        "w_exp_in": w_exp_in[l], "w_exp_out": w_exp_out[l],
        "w_sh_in": w_sh_in[l].astype(BF16), "w_sh_out": w_sh_out[l].astype(BF16),
        "ln2_g": ln2_g[l].reshape(1, D), "ln2_b": ln2_b[l].reshape(1, D),
    }
    return (_encoder(x_prompt, c_prompt, P), _encoder(x_sample, c_sample, P))
```

```python
import functools

import jax
import jax.numpy as jnp
from jax import lax
from jax.experimental import pallas as pl
from jax.experimental.pallas import tpu as pltpu
from jax.experimental.pallas import tpu_sc as plsc

F32 = jnp.float32
BF16 = jnp.bfloat16
U32 = jnp.uint32
HIGHEST = lax.Precision.HIGHEST

HEAD_DIM = 64
N_Q_HEADS = 8
N_KV_HEADS = 2
Q_PER_KV = N_Q_HEADS // N_KV_HEADS
D_ATT = N_Q_HEADS * HEAD_DIM
D_KV = N_KV_HEADS * HEAD_DIM
N_LRU_BLOCKS = 8
CONV_W = 4
LRU_C = 8.0
GRID_W = 64
ROPE_THETA = 10000.0
ROPE_FREQS = HEAD_DIM // 4
N_EXPERTS = 64
TOP_K = 6
N_GROUPS = 8
GROUP_SIZE = N_EXPERTS // N_GROUPS
TOPK_GROUPS = 4
ROUTED_SCALE = 2.5
N_MOD = 6
EPS = 1e-6
DEPTH = 1
DN_ALPHA = (2.0 * DEPTH) ** 0.25
LOG2_E = 1.4426950408889634
LRU_PITCH_PAD = 4
SQRT_FLOOR = 1e-30

LANES = 128
SUBLANES = 8
VMEM_LIMIT_BYTES = 56 * 1024 * 1024

ROW_SPLIT = 2
SC_WINDOW = 128
ATTN_BLOCK = 1024
ATTN_SUB = 512
ROUTE_TILE = 512
EXPERT_BLOCK = 1024
IN_PROJ_TILE = 512
PROJ_TILE = 1024
MOE_OUT_TILE = 512


def _tile(n, pref):
    if n <= pref:
        return n
    for t in range(pref, 0, -1):
        if n % t == 0 and t % SUBLANES == 0:
            return t
    return n


def _params(*sem):
    return pltpu.CompilerParams(dimension_semantics=sem, vmem_limit_bytes=VMEM_LIMIT_BYTES)


def _layer_norm(x):
    mu = jnp.mean(x, axis=-1, keepdims=True)
    xc = x - mu
    var = jnp.mean(xc * xc, axis=-1, keepdims=True)
    return xc * lax.rsqrt(var + EPS)


def _rms(x):
    return x * lax.rsqrt(jnp.mean(x * x, axis=-1, keepdims=True) + EPS)


def _pack_pair(a, b):
    ah = lax.bitcast_convert_type(a.astype(BF16).astype(F32), U32)
    bh = lax.bitcast_convert_type(b.astype(BF16).astype(F32), U32)
    return ah | (bh >> 16)


def _unpack_pair(w):
    a = lax.bitcast_convert_type(w & jnp.uint32(0xFFFF0000), F32)
    b = lax.bitcast_convert_type(w << 16, F32)
    return a, b


def _store_packed(ref, v):
    q = v.shape[-1] // (2 * ROW_SPLIT)
    for h in range(ROW_SPLIT):
        ref[h] = _pack_pair(v[:, 2 * h * q:(2 * h + 1) * q], v[:, (2 * h + 1) * q:(2 * h + 2) * q])


def _load_packed(ref):
    parts = []
    for h in range(ROW_SPLIT):
        parts.extend(_unpack_pair(ref[h]))
    return jnp.concatenate(parts, axis=-1)


def _ada_kernel(c_ref, w_ref, b_ref, o_ref):
    c = c_ref[...]
    sc = c * jax.nn.sigmoid(c)
    o_ref[...] = jnp.dot(sc, w_ref[...], precision=HIGHEST, preferred_element_type=F32) + b_ref[...]


def _ada(c, w_ada, b_ada):
    B, D = c.shape
    N = w_ada.shape[1]
    tn = _tile(N, 1024)
    return pl.pallas_call(
        _ada_kernel,
        grid=(N // tn,),
        in_specs=[
            pl.BlockSpec((B, D), lambda j: (0, 0)),
            pl.BlockSpec((D, tn), lambda j: (0, j)),
            pl.BlockSpec((1, tn), lambda j: (0, j)),
        ],
        out_specs=pl.BlockSpec((B, tn), lambda j: (0, j)),
        out_shape=jax.ShapeDtypeStruct((B, N), F32),
        compiler_params=_params("arbitrary"),
        name="ada",
    )(c, w_ada, b_ada.reshape(1, N))


def _group_sumsq(t, ones_blk):
    sq = t * t
    hi = sq.astype(BF16)
    lo = (sq - hi.astype(F32)).astype(BF16)
    return (jnp.dot(hi, ones_blk, preferred_element_type=F32)
            + jnp.dot(lo, ones_blk, preferred_element_type=F32))


def _rope(t, cos, sin_signed, first_half):
    fwd = pltpu.roll(t, LANES - ROPE_FREQS, axis=1)
    bwd = pltpu.roll(t, ROPE_FREQS, axis=1)
    return t * cos + jnp.where(first_half, fwd, bwd) * sin_signed


def _in_proj_kernel(x_ref, mod_ref, w_ref, gq_ref, gk_ref, cos_ref, sin_ref, ones_ref,
                    qt_ref, k_ref, vt_ref, xl_ref, gl_ref):
    x = x_ref[...]
    shift1 = mod_ref[0:1, :]
    scale1 = mod_ref[1:2, :]
    u = _layer_norm(x) * (1.0 + scale1) + shift1
    proj = jnp.dot(u.astype(BF16), w_ref[...], preferred_element_type=F32)

    cos = cos_ref[...]
    sin_signed = sin_ref[...]
    lane = lax.broadcasted_iota(jnp.int32, cos.shape, 1)
    first_half = (lane % (2 * ROPE_FREQS)) < ROPE_FREQS
    ones2 = ones_ref[...]
    ones1 = ones2[0:LANES, 0:LANES]

    def norm_rope(t, g, ones_blk):
        ss = _group_sumsq(t, ones_blk)
        tn = t * lax.rsqrt(ss * (1.0 / HEAD_DIM) + EPS) * g
        chunks = [_rope(tn[:, c * LANES:(c + 1) * LANES], cos, sin_signed, first_half)
                  for c in range(t.shape[1] // LANES)]
        return chunks

    w2 = 2 * LANES
    for half in range(D_ATT // w2):
        t = proj[:, half * w2:(half + 1) * w2]
        chunks = norm_rope(t, gq_ref[:, half * w2:(half + 1) * w2], ones2)
        for c, ch in enumerate(chunks):
            lo = half * w2 + c * LANES
            qt_ref[lo:lo + LANES, :] = ch.T.astype(BF16)
    k = proj[:, D_ATT:D_ATT + D_KV]
    (kr,) = norm_rope(k, gk_ref[...], ones1)
    k_ref[...] = kr.astype(BF16)
    vt_ref[...] = proj[:, D_ATT + D_KV:D_ATT + 2 * D_KV].T.astype(BF16)
    o = D_ATT + 2 * D_KV
    d_lru = xl_ref.shape[-1]
    xl_ref[...] = proj[:, o:o + d_lru]
    gl_ref[...] = proj[:, o + d_lru:o + 2 * d_lru]


def _in_proj(x, mod, w_in_bf, gq, gk, cos_t, sin_t, ones_blk, d_lru):
    B, S, D = x.shape
    tm = _tile(S, IN_PROJ_TILE)
    d_in = w_in_bf.shape[1]
    full = lambda shape: pl.BlockSpec(shape, lambda b, i: tuple(0 for _ in shape))
    return pl.pallas_call(
        _in_proj_kernel,
        grid=(B, S // tm),
        in_specs=[
            pl.BlockSpec((None, tm, D), lambda b, i: (b, i, 0)),
            pl.BlockSpec((None, N_MOD, D), lambda b, i: (b, 0, 0)),
            full((D, d_in)),
            full((1, D_ATT)),
            full((1, D_KV)),
            pl.BlockSpec((tm, LANES), lambda b, i: (i, 0)),
            pl.BlockSpec((tm, LANES), lambda b, i: (i, 0)),
            full((2 * LANES, 2 * LANES)),
        ],
        out_specs=[
            pl.BlockSpec((None, D_ATT, tm), lambda b, i: (b, 0, i)),
            pl.BlockSpec((None, tm, D_KV), lambda b, i: (b, i, 0)),
            pl.BlockSpec((None, D_KV, tm), lambda b, i: (b, 0, i)),
            pl.BlockSpec((None, tm, d_lru), lambda b, i: (b, i, 0)),
            pl.BlockSpec((None, tm, d_lru), lambda b, i: (b, i, 0)),
        ],
        out_shape=[
            jax.ShapeDtypeStruct((B, D_ATT, S), BF16),
            jax.ShapeDtypeStruct((B, S, D_KV), BF16),
            jax.ShapeDtypeStruct((B, D_KV, S), BF16),
            jax.ShapeDtypeStruct((B, S, d_lru), F32),
            jax.ShapeDtypeStruct((B, S, d_lru), F32),
        ],
        compiler_params=_params("parallel", "parallel"),
        name="in_proj",
    )(x, mod, w_in_bf, gq, gk, cos_t, sin_t, ones_blk)


def _attn_kernel(qt_ref, k_ref, vt_ref, o_ref, *, sub):
    tq = qt_ref.shape[-1]
    k = k_ref[...]
    zeros = jnp.zeros((HEAD_DIM, sub), BF16)
    ones_rows = jnp.ones((2 * SUBLANES, k.shape[0]), BF16)
    units = [(q0, pair) for q0 in range(0, tq, sub) for pair in range(N_Q_HEADS // 2)]

    def scores(q0, pair):
        j = (2 * pair) // Q_PER_KV
        cols = []
        for h in (2 * pair, 2 * pair + 1):
            qt = qt_ref[h * HEAD_DIM:(h + 1) * HEAD_DIM, q0:q0 + sub]
            cols.append(jnp.concatenate([qt, zeros] if j == 0 else [zeros, qt], axis=0))
        return jnp.dot(k, jnp.concatenate(cols, axis=1), preferred_element_type=F32)

    def finish(q0, pair, st):
        j = (2 * pair) // Q_PER_KV
        vt = jnp.concatenate([vt_ref[j * HEAD_DIM:(j + 1) * HEAD_DIM, :], ones_rows], axis=0)
        m = jnp.max(st, axis=0, keepdims=True)
        p = jnp.exp2(st - m)
        ol = jnp.dot(vt, p.astype(BF16), preferred_element_type=F32)
        ot = ol[:HEAD_DIM, :] / ol[HEAD_DIM:HEAD_DIM + 1, :]
        o2 = jnp.concatenate([ot[:, :sub], ot[:, sub:]], axis=0)
        o_ref[q0:q0 + sub, 2 * pair * HEAD_DIM:(2 * pair + 2) * HEAD_DIM] = o2.T.astype(BF16)

    st = scores(*units[0])
    for n, unit in enumerate(units):
        st_next = scores(*units[n + 1]) if n + 1 < len(units) else None
        finish(*unit, st)
        st = st_next


def _attention(qt, k, vt):
    B, _, S = qt.shape
    tq = _tile(S, ATTN_BLOCK)
    sub = _tile(tq, ATTN_SUB)
    return pl.pallas_call(
        functools.partial(_attn_kernel, sub=sub),
        grid=(B, S // tq),
        in_specs=[
            pl.BlockSpec((None, D_ATT, tq), lambda b, i: (b, 0, i)),
            pl.BlockSpec((None, S, D_KV), lambda b, i: (b, 0, 0)),
            pl.BlockSpec((None, D_KV, S), lambda b, i: (b, 0, 0)),
        ],
        out_specs=pl.BlockSpec((None, tq, D_ATT), lambda b, i: (b, i, 0)),
        out_shape=jax.ShapeDtypeStruct((B, S, D_ATT), BF16),
        compiler_params=_params("parallel", "parallel"),
        name="attn",
    )(qt, k, vt)


def _lru_kernel(xl_ref, gl_ref, cw_ref, cb_ref, w_ref, bias_ref, lam_ref, o_ref,
                xc_ref, af_ref, bf_ref, ab_ref, bb_ref, *, seg, pitch):
    S, C = xl_ref.shape
    n_seg = S // seg
    slabs = C // LANES
    xl = xl_ref[...]
    row = lax.broadcasted_iota(jnp.int32, (S, 1), 0)
    xm2 = jnp.where(row >= 2, pltpu.roll(xl, 2, axis=0), 0.0)
    xm1 = jnp.where(row >= 1, pltpu.roll(xl, 1, axis=0), 0.0)
    xp1 = jnp.where(row < S - 1, pltpu.roll(xl, S - 1, axis=0), 0.0)
    xc_ref[...] = (xm2 * cw_ref[0:1, :] + xm1 * cw_ref[1:2, :] + xl * cw_ref[2:3, :]
                   + xp1 * cw_ref[3:4, :] + cb_ref[...])

    lam = lam_ref[...]
    nlam = -lam
    softplus = jnp.maximum(nlam, 0.0) + jnp.log(1.0 + jnp.exp(-jnp.abs(nlam)))
    decay_log2 = (-LRU_C * LOG2_E) * softplus

    for s in range(n_seg):
        xc = xc_ref[s * seg:(s + 1) * seg, :]
        g = jnp.dot(xc.astype(BF16), w_ref[...], preferred_element_type=F32) + bias_ref[...]
        for d, (a_ref, b_ref) in enumerate(((af_ref, bf_ref), (ab_ref, bb_ref))):
            r = jax.nn.sigmoid(g[:, (2 * d) * C:(2 * d + 1) * C])
            i = jax.nn.sigmoid(g[:, (2 * d + 1) * C:(2 * d + 2) * C])
            a = jnp.exp2(r * decay_log2[d:d + 1, :])
            gain_sq = 1.0 - a * a
            gain = gain_sq * lax.rsqrt(jnp.maximum(gain_sq, SQRT_FLOOR))
            bt = gain * i * xc
            for sl in range(slabs):
                a_ref[sl, s * pitch:s * pitch + seg, :] = a[:, sl * LANES:(sl + 1) * LANES]
                b_ref[sl, s * pitch:s * pitch + seg, :] = bt[:, sl * LANES:(sl + 1) * LANES]

    def strided(ref, sl, i):
        return ref.at[sl][pl.ds(i, n_seg, stride=pitch), :]

    def put(ref, sl, i, v):
        ref.at[sl][pl.ds(i, n_seg, stride=pitch), :] = v

    def step(i, carry):
        ib = seg - 1 - i
        new = []
        for sl in range(slabs):
            hf, cf, hb, cb = carry[sl]
            a = strided(af_ref, sl, i)
            hf = a * hf + strided(bf_ref, sl, i)
            cf = a * cf
            put(bf_ref, sl, i, hf)
            put(af_ref, sl, i, cf)
            a = strided(ab_ref, sl, ib)
            hb = a * hb + strided(bb_ref, sl, ib)
            cb = a * cb
            put(bb_ref, sl, ib, hb)
            put(ab_ref, sl, ib, cb)
            new.append((hf, cf, hb, cb))
        return tuple(new)

    zero = jnp.zeros((n_seg, LANES), F32)
    one = jnp.ones((n_seg, LANES), F32)
    ends = lax.fori_loop(0, seg, step, tuple((zero, one, zero, one) for _ in range(slabs)), unroll=4)

    carry_in = []
    for sl in range(slabs):
        hf, cf, hb, cb = ends[sl]
        c = jnp.zeros((1, LANES), F32)
        rows_f = []
        for s in range(n_seg):
            rows_f.append(c)
            c = hf[s:s + 1, :] + cf[s:s + 1, :] * c
        c = jnp.zeros((1, LANES), F32)
        rows_b = [None] * n_seg
        for s in reversed(range(n_seg)):
            rows_b[s] = c
            c = hb[s:s + 1, :] + cb[s:s + 1, :] * c
        carry_in.append((jnp.concatenate(rows_f, axis=0), jnp.concatenate(rows_b, axis=0)))

    def fix(i, carry):
        for sl in range(slabs):
            cin_f, cin_b = carry_in[sl]
            h = (strided(bf_ref, sl, i) + strided(af_ref, sl, i) * cin_f
                 + strided(bb_ref, sl, i) + strided(ab_ref, sl, i) * cin_b)
            put(bf_ref, sl, i, h)
        return carry

    lax.fori_loop(0, seg, fix, 0, unroll=4)

    for s in range(n_seg):
        h = jnp.concatenate([bf_ref[sl, s * pitch:s * pitch + seg, :] for sl in range(slabs)], axis=1)
        gl = gl_ref[s * seg:(s + 1) * seg, :]
        o_ref[s * seg:(s + 1) * seg, :] = (jax.nn.gelu(gl) * h).astype(BF16)


def _lru(xl, gl, conv_w, conv_b, w_gate, b_gate, lam):
    B, S, d_lru = xl.shape
    nh, C, _ = w_gate.shape
    seg = S // SUBLANES
    pitch = seg + LRU_PITCH_PAD
    col = lambda b, h: (b, 0, h)
    state = pltpu.VMEM((C // LANES, SUBLANES * pitch, LANES), F32)
    return pl.pallas_call(
        functools.partial(_lru_kernel, seg=seg, pitch=pitch),
        grid=(B, nh),
        in_specs=[
            pl.BlockSpec((None, S, C), col),
            pl.BlockSpec((None, S, C), col),
            pl.BlockSpec((CONV_W, C), lambda b, h: (0, h)),
            pl.BlockSpec((1, C), lambda b, h: (0, h)),
            pl.BlockSpec((None, C, 4 * C), lambda b, h: (h, 0, 0)),
            pl.BlockSpec((None, 1, 4 * C), lambda b, h: (h, 0, 0)),
            pl.BlockSpec((2, C), lambda b, h: (0, h)),
        ],
        out_specs=pl.BlockSpec((None, S, C), col),
        out_shape=jax.ShapeDtypeStruct((B, S, d_lru), BF16),
        scratch_shapes=[pltpu.VMEM((S, C), F32), state, state, state, state],
        compiler_params=_params("parallel", "parallel"),
        name="lru",
    )(xl, gl, conv_w, conv_b, w_gate, b_gate, lam)


def _out_proj_kernel(att_ref, lru_ref, x_ref, mod_ref, ga_ref, gr_ref, w_ref, g1_ref, b1_ref, wr_ref,
                     x1_ref, u2_ref, lg_ref):
    d_att = att_ref.shape[-1]
    an = _rms(att_ref[...].astype(F32)) * ga_ref[...]
    rn = _rms(lru_ref[...].astype(F32)) * gr_ref[...]
    mix = (jnp.dot(an.astype(BF16), w_ref[0:d_att, :], preferred_element_type=F32)
           + jnp.dot(rn.astype(BF16), w_ref[d_att:, :], preferred_element_type=F32))
    gate1 = mod_ref[2:3, :]
    shift2 = mod_ref[3:4, :]
    scale2 = mod_ref[4:5, :]
    x1 = _layer_norm(DN_ALPHA * x_ref[...] + gate1 * mix) * g1_ref[...] + b1_ref[...]
    x1_ref[...] = x1
    u2 = _layer_norm(x1) * (1.0 + scale2) + shift2
    _store_packed(u2_ref, u2)
    E = lg_ref.shape[0]
    u_hi = u2.astype(BF16)
    u_lo = (u2 - u_hi.astype(F32)).astype(BF16)
    parts = (jnp.dot(u_hi, wr_ref[...], preferred_element_type=F32)
             + jnp.dot(u_lo, wr_ref[...], preferred_element_type=F32))
    parts_t = parts.T
    lg_ref[...] = parts_t[:E, :] + parts_t[E:, :]


def _out_proj(att, lru, x, mod, ga, gr, w_out_bf, g1, b1, wr_split):
    B, S, D = x.shape
    tm = _tile(S, PROJ_TILE)
    d_att, d_lru = att.shape[-1], lru.shape[-1]
    full = lambda shape: pl.BlockSpec(shape, lambda b, i: tuple(0 for _ in shape))
    row = lambda w: pl.BlockSpec((None, tm, w), lambda b, i: (b, i, 0))
    return pl.pallas_call(
        _out_proj_kernel,
        grid=(B, S // tm),
        in_specs=[
            row(d_att), row(d_lru), row(D),
            pl.BlockSpec((None, N_MOD, D), lambda b, i: (b, 0, 0)),
            full((1, d_att)), full((1, d_lru)), full((d_att + d_lru, D)),
            full((1, D)), full((1, D)), full((D, 2 * N_EXPERTS)),
        ],
        out_specs=[row(D),
                   pl.BlockSpec((ROW_SPLIT, None, tm, D // (2 * ROW_SPLIT)), lambda b, i: (0, b, i, 0)),
                   pl.BlockSpec((None, N_EXPERTS, tm), lambda b, i: (b, 0, i))],
        out_shape=[
            jax.ShapeDtypeStruct((B, S, D), F32),
            jax.ShapeDtypeStruct((ROW_SPLIT, B, S, D // (2 * ROW_SPLIT)), U32),
            jax.ShapeDtypeStruct((B, N_EXPERTS, S), F32),
        ],
        compiler_params=_params("parallel", "parallel"),
        name="out_proj",
    )(att, lru, x, mod, ga, gr, w_out_bf, g1, b1, wr_split)


def _route_kernel(lg_ref, bias_ref, w_ref, eid_ref, rank_ref, cnt_ref):
    E, T = lg_ref.shape
    s = jax.nn.sigmoid(lg_ref[...])
    choice = s + bias_ref[...]
    neg = -jnp.inf

    c3 = choice.reshape(N_GROUPS, GROUP_SIZE, T)
    mem = lax.broadcasted_iota(jnp.int32, c3.shape, 1)
    m1 = jnp.max(c3, axis=1, keepdims=True)
    first = jnp.min(jnp.where(c3 == m1, mem, GROUP_SIZE), axis=1, keepdims=True)
    m2 = jnp.max(jnp.where(mem == first, neg, c3), axis=1, keepdims=True)
    gs = (m1 + m2).reshape(N_GROUPS, T)

    gi = lax.broadcasted_iota(jnp.int32, gs.shape, 0)
    rank = jnp.zeros(gs.shape, jnp.int32)
    for g2 in range(N_GROUPS):
        other = gs[g2:g2 + 1, :]
        beats = (other > gs) | ((other == gs) & (gi > g2))
        rank = rank + beats.astype(jnp.int32)
    gmask = (rank < TOPK_GROUPS).astype(F32).reshape(N_GROUPS, 1, T)
    emask = jnp.broadcast_to(gmask, (N_GROUPS, GROUP_SIZE, T)).reshape(E, T) > 0.5
    cur = jnp.where(emask, choice, neg)

    ei = lax.broadcasted_iota(jnp.int32, (E, T), 0)
    sel = jnp.zeros((E, T), F32)
    eids, ws = [], []
    for _ in range(TOP_K):
        m = jnp.max(cur, axis=0, keepdims=True)
        first_e = jnp.min(jnp.where(cur == m, ei, E), axis=0, keepdims=True)
        hit = ei == first_e
        eids.append(first_e)
        ws.append(jnp.sum(jnp.where(hit, s, 0.0), axis=0, keepdims=True))
        sel = jnp.where(hit, 1.0, sel)
        cur = jnp.where(hit, neg, cur)

    denom = ws[0]
    for w in ws[1:]:
        denom = denom + w
    wk = [w / denom * ROUTED_SCALE for w in ws]
    w_ref[...] = jnp.concatenate(wk + [jnp.zeros((LANES - TOP_K, T), F32)], axis=0).T

    sel_bf = sel.astype(BF16)
    t_row = lax.broadcasted_iota(jnp.int32, (T, T), 0)
    t_col = lax.broadcasted_iota(jnp.int32, (T, T), 1)
    before = (t_row < t_col).astype(BF16)
    cum = jnp.dot(sel_bf, before, preferred_element_type=F32)
    ranks = [jnp.sum(jnp.where(ei == e, cum, 0.0), axis=0, keepdims=True).astype(jnp.int32) for e in eids]
    pad = [jnp.zeros((SUBLANES - TOP_K, T), jnp.int32)]
    eid_ref[...] = jnp.concatenate(eids + pad, axis=0)
    rank_ref[...] = jnp.concatenate(ranks + pad, axis=0)
    cnt_ref[...] = lax.dot_general(jnp.ones((SUBLANES, T), BF16), sel_bf, (((1,), (1,)), ((), ())),
                                   preferred_element_type=F32)


def _route(logits_t, router_bias):
    B, E, S = logits_t.shape
    tr = _tile(S, ROUTE_TILE)
    nt = S // tr
    tok = lambda b, i: (b, 0, i)
    return pl.pallas_call(
        _route_kernel,
        grid=(B, nt),
        in_specs=[
            pl.BlockSpec((None, E, tr), tok),
            pl.BlockSpec((E, 1), lambda b, i: (0, 0)),
        ],
        out_specs=[
            pl.BlockSpec((None, tr, LANES), lambda b, i: (b, i, 0)),
            pl.BlockSpec((None, SUBLANES, tr), tok),
            pl.BlockSpec((None, SUBLANES, tr), tok),
            pl.BlockSpec((None, None, SUBLANES, E), lambda b, i: (b, i, 0, 0)),
        ],
        out_shape=[
            jax.ShapeDtypeStruct((B, S, LANES), F32),
            jax.ShapeDtypeStruct((B, SUBLANES, S), jnp.int32),
            jax.ShapeDtypeStruct((B, SUBLANES, S), jnp.int32),
            jax.ShapeDtypeStruct((B, nt, SUBLANES, E), F32),
        ],
        compiler_params=_params("parallel", "parallel"),
        name="route",
    )(logits_t, router_bias.reshape(E, 1))


def _slots_kernel(eid_ref, rank_ref, base_ref, pos_ref):
    nt, E, _ = base_ref.shape
    T = eid_ref.shape[-1] // nt
    ei = lax.broadcasted_iota(jnp.int32, (E, T), 0)
    pad = jnp.zeros((SUBLANES - TOP_K, T), jnp.int32)
    for t in range(nt):
        base = base_ref[t]
        eid = eid_ref[:, t * T:(t + 1) * T]
        rows = [jnp.sum(jnp.where(ei == eid[k:k + 1, :], base, 0), axis=0, keepdims=True)
                for k in range(TOP_K)]
        pos_ref[:, t * T:(t + 1) * T] = jnp.concatenate(rows + [pad], axis=0) + rank_ref[:, t * T:(t + 1) * T]


def _slots(eid, rank, base):
    B, _, S = eid.shape
    nt, E = base.shape[1], base.shape[2]
    seq = lambda b: (b, 0, 0)
    return pl.pallas_call(
        _slots_kernel,
        grid=(B,),
        in_specs=[
            pl.BlockSpec((None, SUBLANES, S), seq),
            pl.BlockSpec((None, SUBLANES, S), seq),
            pl.BlockSpec((None, nt, E, 1), lambda b: (b, 0, 0, 0)),
        ],
        out_specs=pl.BlockSpec((None, SUBLANES, S), seq),
        out_shape=jax.ShapeDtypeStruct((B, SUBLANES, S), jnp.int32),
        compiler_params=_params("parallel"),
        name="slots",
    )(eid, rank, base.reshape(B, nt, E, 1))


def _sc_mesh():
    return plsc.VectorSubcoreMesh(core_axis_name="core", subcore_axis_name="subcore")


def _sc_scatter_rows(src, idx2d, n_out):
    R, W = src.shape
    K = idx2d.shape[0]

    @functools.partial(pl.kernel, out_type=jax.ShapeDtypeStruct((n_out, W), src.dtype), mesh=_sc_mesh(),
                       scratch_types=[], name="sc_dispatch")
    def scatter(x_hbm, i_hbm, o_hbm):
        def body(x_vmem, i_vmem):
            pltpu.sync_copy(x_vmem, o_hbm.at[i_vmem.at[0]])

        pltpu.emit_pipeline(
            body, grid=(R // SC_WINDOW, K),
            in_specs=[pl.BlockSpec((SC_WINDOW, W), lambda i, k: (i, 0)),
                      pl.BlockSpec((1, SC_WINDOW), lambda i, k: (k, i))],
            out_specs=[],
            core_axis_name=("core", "subcore"),
            dimension_semantics=(pltpu.PARALLEL, pltpu.ARBITRARY),
        )(x_hbm, i_hbm)

    return scatter(src, idx2d)


def _sc_gather_rows(table, idx):
    N = idx.shape[0]
    W = table.shape[1]

    @functools.partial(pl.kernel, out_type=jax.ShapeDtypeStruct((N, W), table.dtype), mesh=_sc_mesh(),
                       scratch_types=[], name="sc_combine")
    def gather(t_hbm, i_hbm, o_hbm):
        def body(i_vmem, o_vmem):
            pltpu.sync_copy(t_hbm.at[i_vmem.at[0]], o_vmem)

        pltpu.emit_pipeline(
            body, grid=(N // SC_WINDOW,),
            in_specs=[pl.BlockSpec((1, SC_WINDOW), lambda i: (0, i))],
            out_specs=[pl.BlockSpec((SC_WINDOW, W), lambda i: (i, 0))],
            core_axis_name=("core", "subcore"),
            dimension_semantics=(pltpu.PARALLEL,),
        )(i_hbm, o_hbm)

    return gather(table, idx.reshape(1, N))


def _swiglu_hidden(u, w_in):
    gu = jnp.dot(u, w_in, preferred_element_type=F32)
    d = gu.shape[-1] // 2
    g, up = gu[:, :d], gu[:, d:]
    return g * jax.nn.sigmoid(g) * up


def _experts_kernel(blk_ref, used_ref, xs_ref, wi_ref, wo_ref, ys_ref, wi_bf, wo_bf):
    nb = pl.program_id(0)

    @pl.when((nb == 0) | (blk_ref[nb] != blk_ref[jnp.maximum(nb - 1, 0)]))
    def _():
        wi_bf[...] = wi_ref[...].astype(BF16)
        wo_bf[...] = wo_ref[...].astype(BF16)

    @pl.when(nb < used_ref[0])
    def _():
        x = _load_packed(xs_ref).astype(BF16)
        h = _swiglu_hidden(x, wi_bf[...])
        _store_packed(ys_ref, jnp.dot(h.astype(BF16), wo_bf[...], preferred_element_type=F32))


def _experts(xs, blk_expert, n_used, wei, weo, bm):
    _, P, Wd = xs.shape
    E, D, d2 = wei.shape
    de = weo.shape[1]
    rows = pl.BlockSpec((ROW_SPLIT, bm, Wd), lambda nb, blk, used: (0, jnp.minimum(nb, used[0] - 1), 0))
    return pl.pallas_call(
        _experts_kernel,
        grid_spec=pltpu.PrefetchScalarGridSpec(
            num_scalar_prefetch=2,
            grid=(P // bm,),
            in_specs=[
                rows,
                pl.BlockSpec((None, D, d2), lambda nb, blk, used: (blk[nb], 0, 0)),
                pl.BlockSpec((None, de, D), lambda nb, blk, used: (blk[nb], 0, 0)),
            ],
            out_specs=rows,
            scratch_shapes=[pltpu.VMEM((D, d2), BF16), pltpu.VMEM((de, D), BF16)],
        ),
        out_shape=jax.ShapeDtypeStruct(xs.shape, U32),
        compiler_params=_params("arbitrary"),
        name="experts",
    )(blk_expert, n_used, xs, wei, weo)


def _moe_out_kernel(u_ref, x1_ref, yg_ref, w_ref, mod_ref, wsi_ref, wso_ref, g2_ref, b2_ref, o_ref):
    u = _load_packed(u_ref).astype(BF16)
    hs = _swiglu_hidden(u, wsi_ref[...])
    acc = jnp.dot(hs.astype(BF16), wso_ref[...], preferred_element_type=F32)
    w = w_ref[...]
    for k in range(TOP_K):
        acc = acc + w[:, k:k + 1] * _load_packed(yg_ref.at[k])
    gate2 = mod_ref[5:6, :]
    y = DN_ALPHA * x1_ref[...] + gate2 * acc
    o_ref[...] = _layer_norm(y) * g2_ref[...] + b2_ref[...]


def _moe_out(u2w, x1, yg, w_t, mod, wsi, wso, g2, b2):
    B, S, D = x1.shape
    Wd = u2w.shape[-1]
    tm = _tile(S, MOE_OUT_TILE)
    full = lambda shape: pl.BlockSpec(shape, lambda b, i: tuple(0 for _ in shape))
    return pl.pallas_call(
        _moe_out_kernel,
        grid=(B, S // tm),
        in_specs=[
            pl.BlockSpec((ROW_SPLIT, None, tm, Wd), lambda b, i: (0, b, i, 0)),
            pl.BlockSpec((None, tm, D), lambda b, i: (b, i, 0)),
            pl.BlockSpec((TOP_K, ROW_SPLIT, None, tm, Wd), lambda b, i: (0, 0, b, i, 0)),
            pl.BlockSpec((None, tm, LANES), lambda b, i: (b, i, 0)),
            pl.BlockSpec((None, N_MOD, D), lambda b, i: (b, 0, 0)),
            full(wsi.shape), full(wso.shape), full((1, D)), full((1, D)),
        ],
        out_specs=pl.BlockSpec((None, tm, D), lambda b, i: (b, i, 0)),
        out_shape=jax.ShapeDtypeStruct((B, S, D), F32),
        compiler_params=_params("parallel", "parallel"),
        name="moe_out",
    )(u2w, x1, yg, w_t, mod, wsi, wso, g2, b2)


def _routed_experts(u2w, eid, rank, cnt, wei, weo):
    _, B, S, Wd = u2w.shape
    T = B * S
    E = wei.shape[0]
    bm = _tile(S, EXPERT_BLOCK)
    n_blocks = (T * TOP_K) // bm + E
    P = n_blocks * bm

    cnt_te = cnt[:, :, 0, :].astype(jnp.int32).reshape(-1, E)
    total = jnp.sum(cnt_te, axis=0)
    padded = (total + bm - 1) // bm * bm
    region_end = jnp.cumsum(padded)
    region_start = region_end - padded
    base = region_start[None, :] + jnp.cumsum(cnt_te, axis=0) - cnt_te
    blk_start = jnp.arange(n_blocks, dtype=jnp.int32) * bm
    blk_expert = jnp.minimum(jnp.sum(region_end[None, :] <= blk_start[:, None], axis=1), E - 1).astype(jnp.int32)
    n_used = (region_end[-1] // bm).astype(jnp.int32).reshape(1)

    pos = _slots(eid, rank, base.reshape(B, -1, E))
    pos = jnp.transpose(pos[:, :TOP_K, :], (1, 0, 2)).reshape(TOP_K, T)
    halves = jnp.stack([pos + h * P for h in range(ROW_SPLIT)], axis=1)

    xs = _sc_scatter_rows(u2w.reshape(ROW_SPLIT * T, Wd), halves.reshape(TOP_K, ROW_SPLIT * T), ROW_SPLIT * P)
    ys = _experts(xs.reshape(ROW_SPLIT, P, Wd), blk_expert, n_used, wei, weo, bm)
    yg = _sc_gather_rows(ys.reshape(ROW_SPLIT * P, Wd), halves.reshape(-1))
    return yg.reshape(TOP_K, ROW_SPLIT, B, S, Wd)


def _rope_tables(S):
    rows = S // GRID_W
    row_idx = jnp.repeat(jnp.arange(rows, dtype=F32), GRID_W)
    col_idx = jnp.tile(jnp.arange(GRID_W, dtype=F32), rows)
    inv_freq = ROPE_THETA ** (-jnp.arange(ROPE_FREQS, dtype=F32) / ROPE_FREQS)
    ang = jnp.stack([row_idx[:, None] * inv_freq, col_idx[:, None] * inv_freq], axis=1)
    cos, sin = jnp.cos(ang), jnp.sin(ang)
    cos_h = jnp.stack([cos, cos], axis=2).reshape(S, HEAD_DIM)
    sin_h = jnp.stack([-sin, sin], axis=2).reshape(S, HEAD_DIM)
    reps = LANES // HEAD_DIM
    return jnp.tile(cos_h, (1, reps)), jnp.tile(sin_h, (1, reps))


def _gate_weights(lru_wa, lru_ba, lru_wx, lru_bx, C):
    nb, blk, _ = lru_wa.shape[1:]
    d_lru = nb * blk
    per = C // blk

    def dense(w):
        w = w.reshape(d_lru // C, per, blk, blk)
        eye = jnp.eye(per, dtype=w.dtype)
        return jnp.einsum("hpcd,pq->hpcqd", w, eye).reshape(d_lru // C, C, C)

    w = jnp.concatenate([dense(lru_wa[0]), dense(lru_wx[0]), dense(lru_wa[1]), dense(lru_wx[1])], axis=-1)
    halves = lambda v: v.reshape(d_lru // C, 1, C)
    b = jnp.concatenate([halves(lru_ba[0]), halves(lru_bx[0]), halves(lru_ba[1]), halves(lru_bx[1])], axis=-1)
    return w.astype(BF16), b


def _split_bf16(w):
    hi = w.astype(BF16)
    lo = (w - hi.astype(F32)).astype(BF16)
    return jnp.concatenate([hi, lo], axis=1)


def _encoder(x, c, P):
    B, S, D = x.shape
    mod = _ada(c, P["w_ada"], P["b_ada"]).reshape(B, N_MOD, D)
    cos_t, sin_t = _rope_tables(S)
    qt, k, vt, xl, gl = _in_proj(x, mod, P["w_in"], P["gq"], P["gk"], cos_t, sin_t, P["ones_blk"], P["d_lru"])
    att = _attention(qt, k, vt)
    lru = _lru(xl, gl, P["conv_w"], P["conv_b"], P["w_gate"], P["b_gate"], P["lam"])
    x1, u2w, logits_t = _out_proj(att, lru, x, mod, P["ga"], P["gr"], P["w_out"], P["ln1_g"], P["ln1_b"], P["wr_split"])
    w_t, eid, rank, cnt = _route(logits_t, P["router_bias"])
    yg = _routed_experts(u2w, eid, rank, cnt, P["w_exp_in"], P["w_exp_out"])
    return _moe_out(u2w, x1, yg, w_t, mod, P["w_sh_in"], P["w_sh_out"], P["ln2_g"], P["ln2_b"])


def kernel(x_prompt, x_sample, c_prompt, c_sample, w_ada, b_ada, w_in, q_norm_g, k_norm_g, conv_w, conv_b, lru_wa, lru_ba, lru_wx, lru_bx, lru_lambda, attn_out_g, lru_out_g, w_out, ln1_g, ln1_b, w_router, router_bias, w_exp_in, w_exp_out, w_sh_in, w_sh_out, ln2_g, ln2_b):
    l = 0
    D = x_prompt.shape[-1]
    d_lru = lru_out_g.shape[-1]
    lru_half = 2 * LANES
    w_gate, b_gate = _gate_weights(lru_wa[l], lru_ba[l], lru_wx[l], lru_bx[l], lru_half)
    idx = jnp.arange(2 * LANES)
    ones_blk = (idx[:, None] // HEAD_DIM == idx[None, :] // HEAD_DIM).astype(BF16)
    P = {
        "w_ada": w_ada[l], "b_ada": b_ada[l],
        "w_in": w_in[l].astype(BF16),
        "gq": (jnp.tile(q_norm_g[l], N_Q_HEADS) * (HEAD_DIM ** -0.5 * LOG2_E)).reshape(1, D_ATT),
        "gk": jnp.tile(k_norm_g[l], N_KV_HEADS).reshape(1, D_KV),
        "ones_blk": ones_blk, "d_lru": d_lru,
        "conv_w": conv_w[l], "conv_b": conv_b[l].reshape(1, d_lru),
        "w_gate": w_gate, "b_gate": b_gate, "lam": lru_lambda[l],
        "ga": attn_out_g[l].reshape(1, D_ATT), "gr": lru_out_g[l].reshape(1, d_lru),
        "w_out": w_out[l].astype(BF16),
        "ln1_g": ln1_g[l].reshape(1, D), "ln1_b": ln1_b[l].reshape(1, D),
        "wr_split": _split_bf16(w_router[l]), "router_bias": router_bias[l],
        "w_exp_in": w_exp_in[l], "w_exp_out": w_exp_out[l],
        "w_sh_in": w_sh_in[l].astype(BF16), "w_sh_out": w_sh_out[l].astype(BF16),
        "ln2_g": ln2_g[l].reshape(1, D), "ln2_b": ln2_b[l].reshape(1, D),
    }
    return (_encoder(x_prompt, c_prompt, P), _encoder(x_sample, c_sample, P))
```

```python
import functools

import jax
import jax.numpy as jnp
from jax import lax
from jax.experimental import pallas as pl
from jax.experimental.pallas import tpu as pltpu
from jax.experimental.pallas import tpu_sc as plsc

F32 = jnp.float32
BF16 = jnp.bfloat16
U32 = jnp.uint32
HIGHEST = lax.Precision.HIGHEST

HEAD_DIM = 64
N_Q_HEADS = 8
N_KV_HEADS = 2
Q_PER_KV = N_Q_HEADS // N_KV_HEADS
D_ATT = N_Q_HEADS * HEAD_DIM
D_KV = N_KV_HEADS * HEAD_DIM
N_LRU_BLOCKS = 8
CONV_W = 4
LRU_C = 8.0
GRID_W = 64
ROPE_THETA = 10000.0
ROPE_FREQS = HEAD_DIM // 4
N_EXPERTS = 64
TOP_K = 6
N_GROUPS = 8
GROUP_SIZE = N_EXPERTS // N_GROUPS
TOPK_GROUPS = 4
ROUTED_SCALE = 2.5
N_MOD = 6
EPS = 1e-6
DEPTH = 1
DN_ALPHA = (2.0 * DEPTH) ** 0.25
LOG2_E = 1.4426950408889634
LRU_PITCH_PAD = 4
SQRT_FLOOR = 1e-30

LANES = 128
SUBLANES = 8
VMEM_LIMIT_BYTES = 56 * 1024 * 1024

ROW_SPLIT = 2
SC_WINDOW = 128
ATTN_BLOCK = 1024
ATTN_SUB = 512
ROUTE_TILE = 512
EXPERT_BLOCK = 1024
IN_PROJ_TILE = 512
PROJ_TILE = 1024
MOE_OUT_TILE = 512


def _tile(n, pref):
    if n <= pref:
        return n
    for t in range(pref, 0, -1):
        if n % t == 0 and t % SUBLANES == 0:
            return t
    return n


def _params(*sem):
    return pltpu.CompilerParams(dimension_semantics=sem, vmem_limit_bytes=VMEM_LIMIT_BYTES)


def _layer_norm(x):
    mu = jnp.mean(x, axis=-1, keepdims=True)
    xc = x - mu
    var = jnp.mean(xc * xc, axis=-1, keepdims=True)
    return xc * lax.rsqrt(var + EPS)


def _rms(x):
    return x * lax.rsqrt(jnp.mean(x * x, axis=-1, keepdims=True) + EPS)


def _pack_pair(a, b):
    ah = lax.bitcast_convert_type(a.astype(BF16).astype(F32), U32)
    bh = lax.bitcast_convert_type(b.astype(BF16).astype(F32), U32)
    return ah | (bh >> 16)


def _unpack_pair(w):
    a = lax.bitcast_convert_type(w & jnp.uint32(0xFFFF0000), F32)
    b = lax.bitcast_convert_type(w << 16, F32)
    return a, b


def _store_packed(ref, v):
    q = v.shape[-1] // (2 * ROW_SPLIT)
    for h in range(ROW_SPLIT):
        ref[h] = _pack_pair(v[:, 2 * h * q:(2 * h + 1) * q], v[:, (2 * h + 1) * q:(2 * h + 2) * q])


def _load_packed(ref):
    parts = []
    for h in range(ROW_SPLIT):
        parts.extend(_unpack_pair(ref[h]))
    return jnp.concatenate(parts, axis=-1)


def _ada_kernel(c_ref, w_ref, b_ref, o_ref):
    c = c_ref[...]
    sc = c * jax.nn.sigmoid(c)
    o_ref[...] = jnp.dot(sc, w_ref[...], precision=HIGHEST, preferred_element_type=F32) + b_ref[...]


def _ada(c, w_ada, b_ada):
    B, D = c.shape
    N = w_ada.shape[1]
    tn = _tile(N, 1024)
    return pl.pallas_call(
        _ada_kernel,
        grid=(N // tn,),
        in_specs=[
            pl.BlockSpec((B, D), lambda j: (0, 0)),
            pl.BlockSpec((D, tn), lambda j: (0, j)),
            pl.BlockSpec((1, tn), lambda j: (0, j)),
        ],
        out_specs=pl.BlockSpec((B, tn), lambda j: (0, j)),
        out_shape=jax.ShapeDtypeStruct((B, N), F32),
        compiler_params=_params("arbitrary"),
        name="ada",
    )(c, w_ada, b_ada.reshape(1, N))


def _group_sumsq(t, ones_blk):
    sq = t * t
    hi = sq.astype(BF16)
    lo = (sq - hi.astype(F32)).astype(BF16)
    return (jnp.dot(hi, ones_blk, preferred_element_type=F32)
            + jnp.dot(lo, ones_blk, preferred_element_type=F32))


def _rope(t, cos, sin_signed, first_half):
    fwd = pltpu.roll(t, LANES - ROPE_FREQS, axis=1)
    bwd = pltpu.roll(t, ROPE_FREQS, axis=1)
    return t * cos + jnp.where(first_half, fwd, bwd) * sin_signed


def _in_proj_kernel(x_ref, mod_ref, w_ref, gq_ref, gk_ref, cos_ref, sin_ref, ones_ref,
                    qt_ref, k_ref, vt_ref, xl_ref, gl_ref):
    x = x_ref[...]
    shift1 = mod_ref[0:1, :]
    scale1 = mod_ref[1:2, :]
    u = _layer_norm(x) * (1.0 + scale1) + shift1
    proj = jnp.dot(u.astype(BF16), w_ref[...], preferred_element_type=F32)

    cos = cos_ref[...]
    sin_signed = sin_ref[...]
    lane = lax.broadcasted_iota(jnp.int32, cos.shape, 1)
    first_half = (lane % (2 * ROPE_FREQS)) < ROPE_FREQS
    ones2 = ones_ref[...]
    ones1 = ones2[0:LANES, 0:LANES]

    def norm_rope(t, g, ones_blk):
        ss = _group_sumsq(t, ones_blk)
        tn = t * lax.rsqrt(ss * (1.0 / HEAD_DIM) + EPS) * g
        chunks = [_rope(tn[:, c * LANES:(c + 1) * LANES], cos, sin_signed, first_half)
                  for c in range(t.shape[1] // LANES)]
        return chunks

    w2 = 2 * LANES
    for half in range(D_ATT // w2):
        t = proj[:, half * w2:(half + 1) * w2]
        chunks = norm_rope(t, gq_ref[:, half * w2:(half + 1) * w2], ones2)
        for c, ch in enumerate(chunks):
            lo = half * w2 + c * LANES
            qt_ref[lo:lo + LANES, :] = ch.T.astype(BF16)
    k = proj[:, D_ATT:D_ATT + D_KV]
    (kr,) = norm_rope(k, gk_ref[...], ones1)
    k_ref[...] = kr.astype(BF16)
    vt_ref[...] = proj[:, D_ATT + D_KV:D_ATT + 2 * D_KV].T.astype(BF16)
    o = D_ATT + 2 * D_KV
    d_lru = xl_ref.shape[-1]
    xl_ref[...] = proj[:, o:o + d_lru]
    gl_ref[...] = proj[:, o + d_lru:o + 2 * d_lru]


def _in_proj(x, mod, w_in_bf, gq, gk, cos_t, sin_t, ones_blk, d_lru):
    B, S, D = x.shape
    tm = _tile(S, IN_PROJ_TILE)
    d_in = w_in_bf.shape[1]
    full = lambda shape: pl.BlockSpec(shape, lambda b, i: tuple(0 for _ in shape))
    return pl.pallas_call(
        _in_proj_kernel,
        grid=(B, S // tm),
        in_specs=[
            pl.BlockSpec((None, tm, D), lambda b, i: (b, i, 0)),
            pl.BlockSpec((None, N_MOD, D), lambda b, i: (b, 0, 0)),
            full((D, d_in)),
            full((1, D_ATT)),
            full((1, D_KV)),
            pl.BlockSpec((tm, LANES), lambda b, i: (i, 0)),
            pl.BlockSpec((tm, LANES), lambda b, i: (i, 0)),
            full((2 * LANES, 2 * LANES)),
        ],
        out_specs=[
            pl.BlockSpec((None, D_ATT, tm), lambda b, i: (b, 0, i)),
            pl.BlockSpec((None, tm, D_KV), lambda b, i: (b, i, 0)),
            pl.BlockSpec((None, D_KV, tm), lambda b, i: (b, 0, i)),
            pl.BlockSpec((None, tm, d_lru), lambda b, i: (b, i, 0)),
            pl.BlockSpec((None, tm, d_lru), lambda b, i: (b, i, 0)),
        ],
        out_shape=[
            jax.ShapeDtypeStruct((B, D_ATT, S), BF16),
            jax.ShapeDtypeStruct((B, S, D_KV), BF16),
            jax.ShapeDtypeStruct((B, D_KV, S), BF16),
            jax.ShapeDtypeStruct((B, S, d_lru), F32),
            jax.ShapeDtypeStruct((B, S, d_lru), F32),
        ],
        compiler_params=_params("parallel", "parallel"),
        name="in_proj",
    )(x, mod, w_in_bf, gq, gk, cos_t, sin_t, ones_blk)


def _attn_kernel(qt_ref, k_ref, vt_ref, o_ref, *, sub):
    tq = qt_ref.shape[-1]
    k = k_ref[...]
    zeros = jnp.zeros((HEAD_DIM, sub), BF16)
    ones_rows = jnp.ones((2 * SUBLANES, k.shape[0]), BF16)
    units = [(q0, pair) for q0 in range(0, tq, sub) for pair in range(N_Q_HEADS // 2)]

    def scores(q0, pair):
        j = (2 * pair) // Q_PER_KV
        cols = []
        for h in (2 * pair, 2 * pair + 1):
            qt = qt_ref[h * HEAD_DIM:(h + 1) * HEAD_DIM, q0:q0 + sub]
            cols.append(jnp.concatenate([qt, zeros] if j == 0 else [zeros, qt], axis=0))
        return jnp.dot(k, jnp.concatenate(cols, axis=1), preferred_element_type=F32)

    def finish(q0, pair, st):
        j = (2 * pair) // Q_PER_KV
        vt = jnp.concatenate([vt_ref[j * HEAD_DIM:(j + 1) * HEAD_DIM, :], ones_rows], axis=0)
        m = jnp.max(st, axis=0, keepdims=True)
        p = jnp.exp2(st - m)
        ol = jnp.dot(vt, p.astype(BF16), preferred_element_type=F32)
        ot = ol[:HEAD_DIM, :] / ol[HEAD_DIM:HEAD_DIM + 1, :]
        o2 = jnp.concatenate([ot[:, :sub], ot[:, sub:]], axis=0)
        o_ref[q0:q0 + sub, 2 * pair * HEAD_DIM:(2 * pair + 2) * HEAD_DIM] = o2.T.astype(BF16)

    st = scores(*units[0])
    for n, unit in enumerate(units):
        st_next = scores(*units[n + 1]) if n + 1 < len(units) else None
        finish(*unit, st)
        st = st_next


def _attention(qt, k, vt):
    B, _, S = qt.shape
    tq = _tile(S, ATTN_BLOCK)
    sub = _tile(tq, ATTN_SUB)
    return pl.pallas_call(
        functools.partial(_attn_kernel, sub=sub),
        grid=(B, S // tq),
        in_specs=[
            pl.BlockSpec((None, D_ATT, tq), lambda b, i: (b, 0, i)),
            pl.BlockSpec((None, S, D_KV), lambda b, i: (b, 0, 0)),
            pl.BlockSpec((None, D_KV, S), lambda b, i: (b, 0, 0)),
        ],
        out_specs=pl.BlockSpec((None, tq, D_ATT), lambda b, i: (b, i, 0)),
        out_shape=jax.ShapeDtypeStruct((B, S, D_ATT), BF16),
        compiler_params=_params("parallel", "parallel"),
        name="attn",
    )(qt, k, vt)


def _lru_kernel(xl_ref, gl_ref, cw_ref, cb_ref, w_ref, bias_ref, lam_ref, o_ref,
                xc_ref, af_ref, bf_ref, ab_ref, bb_ref, *, seg, pitch):
    S, C = xl_ref.shape
    n_seg = S // seg
    slabs = C // LANES
    xl = xl_ref[...]
    row = lax.broadcasted_iota(jnp.int32, (S, 1), 0)
    xm2 = jnp.where(row >= 2, pltpu.roll(xl, 2, axis=0), 0.0)
    xm1 = jnp.where(row >= 1, pltpu.roll(xl, 1, axis=0), 0.0)
    xp1 = jnp.where(row < S - 1, pltpu.roll(xl, S - 1, axis=0), 0.0)
    xc_ref[...] = (xm2 * cw_ref[0:1, :] + xm1 * cw_ref[1:2, :] + xl * cw_ref[2:3, :]
                   + xp1 * cw_ref[3:4, :] + cb_ref[...])

    lam = lam_ref[...]
    nlam = -lam
    softplus = jnp.maximum(nlam, 0.0) + jnp.log(1.0 + jnp.exp(-jnp.abs(nlam)))
    decay_log2 = (-LRU_C * LOG2_E) * softplus

    for s in range(n_seg):
        xc = xc_ref[s * seg:(s + 1) * seg, :]
        g = jnp.dot(xc.astype(BF16), w_ref[...], preferred_element_type=F32) + bias_ref[...]
        for d, (a_ref, b_ref) in enumerate(((af_ref, bf_ref), (ab_ref, bb_ref))):
            r = jax.nn.sigmoid(g[:, (2 * d) * C:(2 * d + 1) * C])
            i = jax.nn.sigmoid(g[:, (2 * d + 1) * C:(2 * d + 2) * C])
            a = jnp.exp2(r * decay_log2[d:d + 1, :])
            gain_sq = 1.0 - a * a
            gain = gain_sq * lax.rsqrt(jnp.maximum(gain_sq, SQRT_FLOOR))
            bt = gain * i * xc
            for sl in range(slabs):
                a_ref[sl, s * pitch:s * pitch + seg, :] = a[:, sl * LANES:(sl + 1) * LANES]
                b_ref[sl, s * pitch:s * pitch + seg, :] = bt[:, sl * LANES:(sl + 1) * LANES]

    def strided(ref, sl, i):
        return ref.at[sl][pl.ds(i, n_seg, stride=pitch), :]

    def put(ref, sl, i, v):
        ref.at[sl][pl.ds(i, n_seg, stride=pitch), :] = v

    def step(i, carry):
        ib = seg - 1 - i
        new = []
        for sl in range(slabs):
            hf, cf, hb, cb = carry[sl]
            a = strided(af_ref, sl, i)
            hf = a * hf + strided(bf_ref, sl, i)
            cf = a * cf
            put(bf_ref, sl, i, hf)
            put(af_ref, sl, i, cf)
            a = strided(ab_ref, sl, ib)
            hb = a * hb + strided(bb_ref, sl, ib)
            cb = a * cb
            put(bb_ref, sl, ib, hb)
            put(ab_ref, sl, ib, cb)
            new.append((hf, cf, hb, cb))
        return tuple(new)

    zero = jnp.zeros((n_seg, LANES), F32)
    one = jnp.ones((n_seg, LANES), F32)
    ends = lax.fori_loop(0, seg, step, tuple((zero, one, zero, one) for _ in range(slabs)), unroll=4)

    carry_in = []
    for sl in range(slabs):
        hf, cf, hb, cb = ends[sl]
        c = jnp.zeros((1, LANES), F32)
        rows_f = []
        for s in range(n_seg):
            rows_f.append(c)
            c = hf[s:s + 1, :] + cf[s:s + 1, :] * c
        c = jnp.zeros((1, LANES), F32)
        rows_b = [None] * n_seg
        for s in reversed(range(n_seg)):
            rows_b[s] = c
            c = hb[s:s + 1, :] + cb[s:s + 1, :] * c
        carry_in.append((jnp.concatenate(rows_f, axis=0), jnp.concatenate(rows_b, axis=0)))

    def fix(i, carry):
        for sl in range(slabs):
            cin_f, cin_b = carry_in[sl]
            h = (strided(bf_ref, sl, i) + strided(af_ref, sl, i) * cin_f
                 + strided(bb_ref, sl, i) + strided(ab_ref, sl, i) * cin_b)
            put(bf_ref, sl, i, h)
        return carry

    lax.fori_loop(0, seg, fix, 0, unroll=4)

    for s in range(n_seg):
        h = jnp.concatenate([bf_ref[sl, s * pitch:s * pitch + seg, :] for sl in range(slabs)], axis=1)
        gl = gl_ref[s * seg:(s + 1) * seg, :]
        o_ref[s * seg:(s + 1) * seg, :] = (jax.nn.gelu(gl) * h).astype(BF16)


def _lru(xl, gl, conv_w, conv_b, w_gate, b_gate, lam):
    B, S, d_lru = xl.shape
    nh, C, _ = w_gate.shape
    seg = S // SUBLANES
    pitch = seg + LRU_PITCH_PAD
    col = lambda b, h: (b, 0, h)
    state = pltpu.VMEM((C // LANES, SUBLANES * pitch, LANES), F32)
    return pl.pallas_call(
        functools.partial(_lru_kernel, seg=seg, pitch=pitch),
        grid=(B, nh),
        in_specs=[
            pl.BlockSpec((None, S, C), col),
            pl.BlockSpec((None, S, C), col),
            pl.BlockSpec((CONV_W, C), lambda b, h: (0, h)),
            pl.BlockSpec((1, C), lambda b, h: (0, h)),
            pl.BlockSpec((None, C, 4 * C), lambda b, h: (h, 0, 0)),
            pl.BlockSpec((None, 1, 4 * C), lambda b, h: (h, 0, 0)),
            pl.BlockSpec((2, C), lambda b, h: (0, h)),
        ],
        out_specs=pl.BlockSpec((None, S, C), col),
        out_shape=jax.ShapeDtypeStruct((B, S, d_lru), BF16),
        scratch_shapes=[pltpu.VMEM((S, C), F32), state, state, state, state],
        compiler_params=_params("parallel", "parallel"),
        name="lru",
    )(xl, gl, conv_w, conv_b, w_gate, b_gate, lam)


def _out_proj_kernel(att_ref, lru_ref, x_ref, mod_ref, ga_ref, gr_ref, w_ref, g1_ref, b1_ref, wr_ref,
                     x1_ref, u2_ref, lg_ref):
    d_att = att_ref.shape[-1]
    an = _rms(att_ref[...].astype(F32)) * ga_ref[...]
    rn = _rms(lru_ref[...].astype(F32)) * gr_ref[...]
    mix = (jnp.dot(an.astype(BF16), w_ref[0:d_att, :], preferred_element_type=F32)
           + jnp.dot(rn.astype(BF16), w_ref[d_att:, :], preferred_element_type=F32))
    gate1 = mod_ref[2:3, :]
    shift2 = mod_ref[3:4, :]
    scale2 = mod_ref[4:5, :]
    x1 = _layer_norm(DN_ALPHA * x_ref[...] + gate1 * mix) * g1_ref[...] + b1_ref[...]
    _store_packed(x1_ref, x1)
    u2 = _layer_norm(x1) * (1.0 + scale2) + shift2
    _store_packed(u2_ref, u2)
    E = lg_ref.shape[0]
    u_hi = u2.astype(BF16)
    u_lo = (u2 - u_hi.astype(F32)).astype(BF16)
    parts = (jnp.dot(u_hi, wr_ref[...], preferred_element_type=F32)
             + jnp.dot(u_lo, wr_ref[...], preferred_element_type=F32))
    parts_t = parts.T
    lg_ref[...] = parts_t[:E, :] + parts_t[E:, :]


def _out_proj(att, lru, x, mod, ga, gr, w_out_bf, g1, b1, wr_split):
    B, S, D = x.shape
    tm = _tile(S, PROJ_TILE)
    d_att, d_lru = att.shape[-1], lru.shape[-1]
    full = lambda shape: pl.BlockSpec(shape, lambda b, i: tuple(0 for _ in shape))
    row = lambda w: pl.BlockSpec((None, tm, w), lambda b, i: (b, i, 0))
    packed = pl.BlockSpec((ROW_SPLIT, None, tm, D // (2 * ROW_SPLIT)), lambda b, i: (0, b, i, 0))
    return pl.pallas_call(
        _out_proj_kernel,
        grid=(B, S // tm),
        in_specs=[
            row(d_att), row(d_lru), row(D),
            pl.BlockSpec((None, N_MOD, D), lambda b, i: (b, 0, 0)),
            full((1, d_att)), full((1, d_lru)), full((d_att + d_lru, D)),
            full((1, D)), full((1, D)), full((D, 2 * N_EXPERTS)),
        ],
        out_specs=[packed, packed, pl.BlockSpec((None, N_EXPERTS, tm), lambda b, i: (b, 0, i))],
        out_shape=[
            jax.ShapeDtypeStruct((ROW_SPLIT, B, S, D // (2 * ROW_SPLIT)), U32),
            jax.ShapeDtypeStruct((ROW_SPLIT, B, S, D // (2 * ROW_SPLIT)), U32),
            jax.ShapeDtypeStruct((B, N_EXPERTS, S), F32),
        ],
        compiler_params=_params("parallel", "parallel"),
        name="out_proj",
    )(att, lru, x, mod, ga, gr, w_out_bf, g1, b1, wr_split)


def _route_kernel(lg_ref, bias_ref, before_ref, w_ref, eid_ref, rank_ref, cnt_ref):
    E, T = lg_ref.shape
    s = jax.nn.sigmoid(lg_ref[...])
    choice = s + bias_ref[...]
    neg = -jnp.inf

    c3 = choice.reshape(N_GROUPS, GROUP_SIZE, T)
    mem = lax.broadcasted_iota(jnp.int32, c3.shape, 1)
    m1 = jnp.max(c3, axis=1, keepdims=True)
    first = jnp.min(jnp.where(c3 == m1, mem, GROUP_SIZE), axis=1, keepdims=True)
    m2 = jnp.max(jnp.where(mem == first, neg, c3), axis=1, keepdims=True)
    gs = (m1 + m2).reshape(N_GROUPS, T)

    gi = lax.broadcasted_iota(jnp.int32, gs.shape, 0)
    rank = jnp.zeros(gs.shape, jnp.int32)
    for g2 in range(N_GROUPS):
        other = gs[g2:g2 + 1, :]
        beats = (other > gs) | ((other == gs) & (gi > g2))
        rank = rank + beats.astype(jnp.int32)
    gmask = (rank < TOPK_GROUPS).astype(F32).reshape(N_GROUPS, 1, T)
    emask = jnp.broadcast_to(gmask, (N_GROUPS, GROUP_SIZE, T)).reshape(E, T) > 0.5
    cur = jnp.where(emask, choice, neg)

    ei = lax.broadcasted_iota(jnp.int32, (E, T), 0)
    sel = jnp.zeros((E, T), F32)
    eids, ws = [], []
    for _ in range(TOP_K):
        m = jnp.max(cur, axis=0, keepdims=True)
        first_e = jnp.min(jnp.where(cur == m, ei, E), axis=0, keepdims=True)
        hit = ei == first_e
        eids.append(first_e)
        ws.append(jnp.sum(jnp.where(hit, s, 0.0), axis=0, keepdims=True))
        sel = jnp.where(hit, 1.0, sel)
        cur = jnp.where(hit, neg, cur)

    denom = ws[0]
    for w in ws[1:]:
        denom = denom + w
    wk = [w / denom * ROUTED_SCALE for w in ws]
    w_ref[...] = jnp.concatenate(wk + [jnp.zeros((LANES - TOP_K, T), F32)], axis=0).T

    sel_bf = sel.astype(BF16)
    cum = jnp.dot(sel_bf, before_ref[...], preferred_element_type=F32)
    ranks = [jnp.sum(jnp.where(ei == e, cum, 0.0), axis=0, keepdims=True).astype(jnp.int32) for e in eids]
    pad = [jnp.zeros((SUBLANES - TOP_K, T), jnp.int32)]
    eid_ref[...] = jnp.concatenate(eids + pad, axis=0)
    rank_ref[...] = jnp.concatenate(ranks + pad, axis=0)
    cnt_ref[...] = lax.dot_general(jnp.ones((SUBLANES, T), BF16), sel_bf, (((1,), (1,)), ((), ())),
                                   preferred_element_type=F32)


def _route(logits_t, router_bias):
    B, E, S = logits_t.shape
    tr = _tile(S, ROUTE_TILE)
    nt = S // tr
    tok = lambda b, i: (b, 0, i)
    t_idx = jnp.arange(tr)
    before = (t_idx[:, None] < t_idx[None, :]).astype(BF16)
    return pl.pallas_call(
        _route_kernel,
        grid=(B, nt),
        in_specs=[
            pl.BlockSpec((None, E, tr), tok),
            pl.BlockSpec((E, 1), lambda b, i: (0, 0)),
            pl.BlockSpec((tr, tr), lambda b, i: (0, 0)),
        ],
        out_specs=[
            pl.BlockSpec((None, tr, LANES), lambda b, i: (b, i, 0)),
            pl.BlockSpec((None, SUBLANES, tr), tok),
            pl.BlockSpec((None, SUBLANES, tr), tok),
            pl.BlockSpec((None, None, SUBLANES, E), lambda b, i: (b, i, 0, 0)),
        ],
        out_shape=[
            jax.ShapeDtypeStruct((B, S, LANES), F32),
            jax.ShapeDtypeStruct((B, SUBLANES, S), jnp.int32),
            jax.ShapeDtypeStruct((B, SUBLANES, S), jnp.int32),
            jax.ShapeDtypeStruct((B, nt, SUBLANES, E), F32),
        ],
        compiler_params=_params("parallel", "parallel"),
        name="route",
    )(logits_t, router_bias.reshape(E, 1), before)


def _slots_kernel(eid_ref, rank_ref, base_ref, pos_ref):
    nt, E, _ = base_ref.shape
    T = eid_ref.shape[-1] // nt
    ei = lax.broadcasted_iota(jnp.int32, (E, T), 0)
    pad = jnp.zeros((SUBLANES - TOP_K, T), jnp.int32)
    for t in range(nt):
        base = base_ref[t]
        eid = eid_ref[:, t * T:(t + 1) * T]
        rows = [jnp.sum(jnp.where(ei == eid[k:k + 1, :], base, 0), axis=0, keepdims=True)
                for k in range(TOP_K)]
        pos_ref[:, t * T:(t + 1) * T] = jnp.concatenate(rows + [pad], axis=0) + rank_ref[:, t * T:(t + 1) * T]


def _slots(eid, rank, base):
    B, _, S = eid.shape
    nt, E = base.shape[1], base.shape[2]
    seq = lambda b: (b, 0, 0)
    return pl.pallas_call(
        _slots_kernel,
        grid=(B,),
        in_specs=[
            pl.BlockSpec((None, SUBLANES, S), seq),
            pl.BlockSpec((None, SUBLANES, S), seq),
            pl.BlockSpec((None, nt, E, 1), lambda b: (b, 0, 0, 0)),
        ],
        out_specs=pl.BlockSpec((None, SUBLANES, S), seq),
        out_shape=jax.ShapeDtypeStruct((B, SUBLANES, S), jnp.int32),
        compiler_params=_params("parallel"),
        name="slots",
    )(eid, rank, base.reshape(B, nt, E, 1))


def _sc_mesh():
    return plsc.VectorSubcoreMesh(core_axis_name="core", subcore_axis_name="subcore")


def _sc_scatter_rows(src, idx2d, n_out):
    R, W = src.shape
    K = idx2d.shape[0]

    @functools.partial(pl.kernel, out_type=jax.ShapeDtypeStruct((n_out, W), src.dtype), mesh=_sc_mesh(),
                       scratch_types=[], name="sc_dispatch")
    def scatter(x_hbm, i_hbm, o_hbm):
        def body(x_vmem, i_vmem):
            pltpu.sync_copy(x_vmem, o_hbm.at[i_vmem.at[0]])

        pltpu.emit_pipeline(
            body, grid=(R // SC_WINDOW, K),
            in_specs=[pl.BlockSpec((SC_WINDOW, W), lambda i, k: (i, 0)),
                      pl.BlockSpec((1, SC_WINDOW), lambda i, k: (k, i))],
            out_specs=[],
            core_axis_name=("core", "subcore"),
            dimension_semantics=(pltpu.PARALLEL, pltpu.ARBITRARY),
        )(x_hbm, i_hbm)

    return scatter(src, idx2d)


def _sc_gather_rows(table, idx):
    N = idx.shape[0]
    W = table.shape[1]

    @functools.partial(pl.kernel, out_type=jax.ShapeDtypeStruct((N, W), table.dtype), mesh=_sc_mesh(),
                       scratch_types=[], name="sc_combine")
    def gather(t_hbm, i_hbm, o_hbm):
        def body(i_vmem, o_vmem):
            pltpu.sync_copy(t_hbm.at[i_vmem.at[0]], o_vmem)

        pltpu.emit_pipeline(
            body, grid=(N // SC_WINDOW,),
            in_specs=[pl.BlockSpec((1, SC_WINDOW), lambda i: (0, i))],
            out_specs=[pl.BlockSpec((SC_WINDOW, W), lambda i: (i, 0))],
            core_axis_name=("core", "subcore"),
            dimension_semantics=(pltpu.PARALLEL,),
        )(i_hbm, o_hbm)

    return gather(table, idx.reshape(1, N))


def _swiglu_hidden(u, w_in):
    gu = jnp.dot(u, w_in, preferred_element_type=F32)
    d = gu.shape[-1] // 2
    g, up = gu[:, :d], gu[:, d:]
    return g * jax.nn.sigmoid(g) * up


def _experts_kernel(blk_ref, used_ref, xs_ref, wi_ref, wo_ref, ys_ref, wi_bf, wo_bf):
    nb = pl.program_id(0)

    @pl.when((nb == 0) | (blk_ref[nb] != blk_ref[jnp.maximum(nb - 1, 0)]))
    def _():
        wi_bf[...] = wi_ref[...].astype(BF16)
        wo_bf[...] = wo_ref[...].astype(BF16)

    @pl.when(nb < used_ref[0])
    def _():
        x = _load_packed(xs_ref).astype(BF16)
        h = _swiglu_hidden(x, wi_bf[...])
        _store_packed(ys_ref, jnp.dot(h.astype(BF16), wo_bf[...], preferred_element_type=F32))


def _experts(xs, blk_expert, n_used, wei, weo, bm):
    _, P, Wd = xs.shape
    E, D, d2 = wei.shape
    de = weo.shape[1]
    rows = pl.BlockSpec((ROW_SPLIT, bm, Wd), lambda nb, blk, used: (0, jnp.minimum(nb, used[0] - 1), 0))
    return pl.pallas_call(
        _experts_kernel,
        grid_spec=pltpu.PrefetchScalarGridSpec(
            num_scalar_prefetch=2,
            grid=(P // bm,),
            in_specs=[
                rows,
                pl.BlockSpec((None, D, d2), lambda nb, blk, used: (blk[nb], 0, 0)),
                pl.BlockSpec((None, de, D), lambda nb, blk, used: (blk[nb], 0, 0)),
            ],
            out_specs=rows,
            scratch_shapes=[pltpu.VMEM((D, d2), BF16), pltpu.VMEM((de, D), BF16)],
        ),
        out_shape=jax.ShapeDtypeStruct(xs.shape, U32),
        compiler_params=_params("arbitrary"),
        name="experts",
    )(blk_expert, n_used, xs, wei, weo)


def _moe_out_kernel(u_ref, x1_ref, yg_ref, w_ref, mod_ref, wsi_ref, wso_ref, g2_ref, b2_ref, o_ref):
    u = _load_packed(u_ref).astype(BF16)
    hs = _swiglu_hidden(u, wsi_ref[...])
    acc = jnp.dot(hs.astype(BF16), wso_ref[...], preferred_element_type=F32)
    w = w_ref[...]
    for k in range(TOP_K):
        acc = acc + w[:, k:k + 1] * _load_packed(yg_ref.at[k])
    gate2 = mod_ref[5:6, :]
    y = DN_ALPHA * _load_packed(x1_ref) + gate2 * acc
    o_ref[...] = _layer_norm(y) * g2_ref[...] + b2_ref[...]


def _moe_out(u2w, x1w, yg, w_t, mod, wsi, wso, g2, b2):
    _, B, S, Wd = u2w.shape
    D = 2 * ROW_SPLIT * Wd
    tm = _tile(S, MOE_OUT_TILE)
    full = lambda shape: pl.BlockSpec(shape, lambda b, i: tuple(0 for _ in shape))
    packed = pl.BlockSpec((ROW_SPLIT, None, tm, Wd), lambda b, i: (0, b, i, 0))
    return pl.pallas_call(
        _moe_out_kernel,
        grid=(B, S // tm),
        in_specs=[
            packed,
            packed,
            pl.BlockSpec((TOP_K, ROW_SPLIT, None, tm, Wd), lambda b, i: (0, 0, b, i, 0)),
            pl.BlockSpec((None, tm, LANES), lambda b, i: (b, i, 0)),
            pl.BlockSpec((None, N_MOD, D), lambda b, i: (b, 0, 0)),
            full(wsi.shape), full(wso.shape), full((1, D)), full((1, D)),
        ],
        out_specs=pl.BlockSpec((None, tm, D), lambda b, i: (b, i, 0)),
        out_shape=jax.ShapeDtypeStruct((B, S, D), F32),
        compiler_params=_params("parallel", "parallel"),
        name="moe_out",
    )(u2w, x1w, yg, w_t, mod, wsi, wso, g2, b2)


def _routed_experts(u2w, eid, rank, cnt, wei, weo):
    _, B, S, Wd = u2w.shape
    T = B * S
    E = wei.shape[0]
    bm = _tile(S, EXPERT_BLOCK)
    n_blocks = (T * TOP_K) // bm + E
    P = n_blocks * bm

    cnt_te = cnt[:, :, 0, :].astype(jnp.int32).reshape(-1, E)
    total = jnp.sum(cnt_te, axis=0)
    padded = (total + bm - 1) // bm * bm
    region_end = jnp.cumsum(padded)
    region_start = region_end - padded
    base = region_start[None, :] + jnp.cumsum(cnt_te, axis=0) - cnt_te
    blk_start = jnp.arange(n_blocks, dtype=jnp.int32) * bm
    blk_expert = jnp.minimum(jnp.sum(region_end[None, :] <= blk_start[:, None], axis=1), E - 1).astype(jnp.int32)
    n_used = (region_end[-1] // bm).astype(jnp.int32).reshape(1)

    pos = _slots(eid, rank, base.reshape(B, -1, E))
    pos = jnp.transpose(pos[:, :TOP_K, :], (1, 0, 2)).reshape(TOP_K, T)
    halves = jnp.stack([pos + h * P for h in range(ROW_SPLIT)], axis=1)

    xs = _sc_scatter_rows(u2w.reshape(ROW_SPLIT * T, Wd), halves.reshape(TOP_K, ROW_SPLIT * T), ROW_SPLIT * P)
    ys = _experts(xs.reshape(ROW_SPLIT, P, Wd), blk_expert, n_used, wei, weo, bm)
    yg = _sc_gather_rows(ys.reshape(ROW_SPLIT * P, Wd), halves.reshape(-1))
    return yg.reshape(TOP_K, ROW_SPLIT, B, S, Wd)


def _rope_tables(S):
    rows = S // GRID_W
    row_idx = jnp.repeat(jnp.arange(rows, dtype=F32), GRID_W)
    col_idx = jnp.tile(jnp.arange(GRID_W, dtype=F32), rows)
    inv_freq = ROPE_THETA ** (-jnp.arange(ROPE_FREQS, dtype=F32) / ROPE_FREQS)
    ang = jnp.stack([row_idx[:, None] * inv_freq, col_idx[:, None] * inv_freq], axis=1)
    cos, sin = jnp.cos(ang), jnp.sin(ang)
    cos_h = jnp.stack([cos, cos], axis=2).reshape(S, HEAD_DIM)
    sin_h = jnp.stack([-sin, sin], axis=2).reshape(S, HEAD_DIM)
    reps = LANES // HEAD_DIM
    return jnp.tile(cos_h, (1, reps)), jnp.tile(sin_h, (1, reps))


def _gate_weights(lru_wa, lru_ba, lru_wx, lru_bx, C):
    nb, blk, _ = lru_wa.shape[1:]
    d_lru = nb * blk
    per = C // blk

    def dense(w):
        w = w.reshape(d_lru // C, per, blk, blk)
        eye = jnp.eye(per, dtype=w.dtype)
        return jnp.einsum("hpcd,pq->hpcqd", w, eye).reshape(d_lru // C, C, C)

    w = jnp.concatenate([dense(lru_wa[0]), dense(lru_wx[0]), dense(lru_wa[1]), dense(lru_wx[1])], axis=-1)
    halves = lambda v: v.reshape(d_lru // C, 1, C)
    b = jnp.concatenate([halves(lru_ba[0]), halves(lru_bx[0]), halves(lru_ba[1]), halves(lru_bx[1])], axis=-1)
    return w.astype(BF16), b


def _split_bf16(w):
    hi = w.astype(BF16)
    lo = (w - hi.astype(F32)).astype(BF16)
    return jnp.concatenate([hi, lo], axis=1)


def _encoder(x, c, P):
    B, S, D = x.shape
    mod = _ada(c, P["w_ada"], P["b_ada"]).reshape(B, N_MOD, D)
    cos_t, sin_t = _rope_tables(S)
    qt, k, vt, xl, gl = _in_proj(x, mod, P["w_in"], P["gq"], P["gk"], cos_t, sin_t, P["ones_blk"], P["d_lru"])
    att = _attention(qt, k, vt)
    lru = _lru(xl, gl, P["conv_w"], P["conv_b"], P["w_gate"], P["b_gate"], P["lam"])
    x1w, u2w, logits_t = _out_proj(att, lru, x, mod, P["ga"], P["gr"], P["w_out"], P["ln1_g"], P["ln1_b"], P["wr_split"])
    w_t, eid, rank, cnt = _route(logits_t, P["router_bias"])
    yg = _routed_experts(u2w, eid, rank, cnt, P["w_exp_in"], P["w_exp_out"])
    return _moe_out(u2w, x1w, yg, w_t, mod, P["w_sh_in"], P["w_sh_out"], P["ln2_g"], P["ln2_b"])


def kernel(x_prompt, x_sample, c_prompt, c_sample, w_ada, b_ada, w_in, q_norm_g, k_norm_g, conv_w, conv_b, lru_wa, lru_ba, lru_wx, lru_bx, lru_lambda, attn_out_g, lru_out_g, w_out, ln1_g, ln1_b, w_router, router_bias, w_exp_in, w_exp_out, w_sh_in, w_sh_out, ln2_g, ln2_b):
    l = 0
    D = x_prompt.shape[-1]
    d_lru = lru_out_g.shape[-1]
    lru_half = 2 * LANES
    w_gate, b_gate = _gate_weights(lru_wa[l], lru_ba[l], lru_wx[l], lru_bx[l], lru_half)
    idx = jnp.arange(2 * LANES)
    ones_blk = (idx[:, None] // HEAD_DIM == idx[None, :] // HEAD_DIM).astype(BF16)
    P = {
        "w_ada": w_ada[l], "b_ada": b_ada[l],
        "w_in": w_in[l].astype(BF16),
        "gq": (jnp.tile(q_norm_g[l], N_Q_HEADS) * (HEAD_DIM ** -0.5 * LOG2_E)).reshape(1, D_ATT),
        "gk": jnp.tile(k_norm_g[l], N_KV_HEADS).reshape(1, D_KV),
        "ones_blk": ones_blk, "d_lru": d_lru,
        "conv_w": conv_w[l], "conv_b": conv_b[l].reshape(1, d_lru),
        "w_gate": w_gate, "b_gate": b_gate, "lam": lru_lambda[l],
        "ga": attn_out_g[l].reshape(1, D_ATT), "gr": lru_out_g[l].reshape(1, d_lru),
        "w_out": w_out[l].astype(BF16),
        "ln1_g": ln1_g[l].reshape(1, D), "ln1_b": ln1_b[l].reshape(1, D),
        "wr_split": _split_bf16(w_router[l]), "router_bias": router_bias[l],
        "w_exp_in": w_exp_in[l], "w_exp_out": w_exp_out[l],
        "w_sh_in": w_sh_in[l].astype(BF16), "w_sh_out": w_sh_out[l].astype(BF16),
        "ln2_g": ln2_g[l].reshape(1, D), "ln2_b": ln2_b[l].reshape(1, D),
    }
    return (_encoder(x_prompt, c_prompt, P), _encoder(x_sample, c_sample, P))
```

```python
import functools

import jax
import jax.numpy as jnp
from jax import lax
from jax.experimental import pallas as pl
from jax.experimental.pallas import tpu as pltpu
from jax.experimental.pallas import tpu_sc as plsc

F32 = jnp.float32
BF16 = jnp.bfloat16
U32 = jnp.uint32
HIGHEST = lax.Precision.HIGHEST

HEAD_DIM = 64
N_Q_HEADS = 8
N_KV_HEADS = 2
Q_PER_KV = N_Q_HEADS // N_KV_HEADS
D_ATT = N_Q_HEADS * HEAD_DIM
D_KV = N_KV_HEADS * HEAD_DIM
N_LRU_BLOCKS = 8
CONV_W = 4
LRU_C = 8.0
GRID_W = 64
ROPE_THETA = 10000.0
ROPE_FREQS = HEAD_DIM // 4
N_EXPERTS = 64
TOP_K = 6
N_GROUPS = 8
GROUP_SIZE = N_EXPERTS // N_GROUPS
TOPK_GROUPS = 4
ROUTED_SCALE = 2.5
N_MOD = 6
EPS = 1e-6
DEPTH = 1
DN_ALPHA = (2.0 * DEPTH) ** 0.25
LOG2_E = 1.4426950408889634
LRU_PITCH_PAD = 4
SQRT_FLOOR = 1e-30

LANES = 128
SUBLANES = 8
VMEM_LIMIT_BYTES = 56 * 1024 * 1024

ROW_SPLIT = 2
SC_WINDOW = 128
ATTN_BLOCK = 1024
ATTN_SUB = 512
ROUTE_TILE = 512
EXPERT_BLOCK = 1024
IN_PROJ_TILE = 512
PROJ_TILE = 1024
MOE_OUT_TILE = 512


def _tile(n, pref):
    if n <= pref:
        return n
    for t in range(pref, 0, -1):
        if n % t == 0 and t % SUBLANES == 0:
            return t
    return n


def _params(*sem):
    return pltpu.CompilerParams(dimension_semantics=sem, vmem_limit_bytes=VMEM_LIMIT_BYTES)


def _layer_norm(x):
    mu = jnp.mean(x, axis=-1, keepdims=True)
    xc = x - mu
    var = jnp.mean(xc * xc, axis=-1, keepdims=True)
    return xc * lax.rsqrt(var + EPS)


def _rms(x):
    return x * lax.rsqrt(jnp.mean(x * x, axis=-1, keepdims=True) + EPS)


def _pack_pair(a, b):
    ah = lax.bitcast_convert_type(a.astype(BF16).astype(F32), U32)
    bh = lax.bitcast_convert_type(b.astype(BF16).astype(F32), U32)
    return ah | (bh >> 16)


def _unpack_pair(w):
    a = lax.bitcast_convert_type(w & jnp.uint32(0xFFFF0000), F32)
    b = lax.bitcast_convert_type(w << 16, F32)
    return a, b


def _store_packed(ref, v):
    q = v.shape[-1] // (2 * ROW_SPLIT)
    for h in range(ROW_SPLIT):
        ref[h] = _pack_pair(v[:, 2 * h * q:(2 * h + 1) * q], v[:, (2 * h + 1) * q:(2 * h + 2) * q])


def _load_packed(ref):
    parts = []
    for h in range(ROW_SPLIT):
        parts.extend(_unpack_pair(ref[h]))
    return jnp.concatenate(parts, axis=-1)


def _ada_kernel(c_ref, w_ref, b_ref, o_ref):
    c = c_ref[...]
    sc = c * jax.nn.sigmoid(c)
    o_ref[...] = jnp.dot(sc, w_ref[...], precision=HIGHEST, preferred_element_type=F32) + b_ref[...]


def _ada(c, w_ada, b_ada):
    B, D = c.shape
    N = w_ada.shape[1]
    tn = _tile(N, 1024)
    return pl.pallas_call(
        _ada_kernel,
        grid=(N // tn,),
        in_specs=[
            pl.BlockSpec((B, D), lambda j: (0, 0)),
            pl.BlockSpec((D, tn), lambda j: (0, j)),
            pl.BlockSpec((1, tn), lambda j: (0, j)),
        ],
        out_specs=pl.BlockSpec((B, tn), lambda j: (0, j)),
        out_shape=jax.ShapeDtypeStruct((B, N), F32),
        compiler_params=_params("arbitrary"),
        name="ada",
    )(c, w_ada, b_ada.reshape(1, N))


def _group_sumsq(t, ones_blk):
    sq = t * t
    hi = sq.astype(BF16)
    lo = (sq - hi.astype(F32)).astype(BF16)
    return (jnp.dot(hi, ones_blk, preferred_element_type=F32)
            + jnp.dot(lo, ones_blk, preferred_element_type=F32))


def _rope(t, cos, sin_signed, first_half):
    fwd = pltpu.roll(t, LANES - ROPE_FREQS, axis=1)
    bwd = pltpu.roll(t, ROPE_FREQS, axis=1)
    return t * cos + jnp.where(first_half, fwd, bwd) * sin_signed


def _in_proj_kernel(x_ref, mod_ref, w_ref, gq_ref, gk_ref, cos_ref, sin_ref, ones_ref,
                    qt_ref, k_ref, vt_ref, xl_ref, gl_ref):
    x = x_ref[...]
    shift1 = mod_ref[0:1, :]
    scale1 = mod_ref[1:2, :]
    u = _layer_norm(x) * (1.0 + scale1) + shift1
    proj = jnp.dot(u.astype(BF16), w_ref[...], preferred_element_type=F32)

    cos = cos_ref[...]
    sin_signed = sin_ref[...]
    lane = lax.broadcasted_iota(jnp.int32, cos.shape, 1)
    first_half = (lane % (2 * ROPE_FREQS)) < ROPE_FREQS
    ones2 = ones_ref[...]
    ones1 = ones2[0:LANES, 0:LANES]

    def norm_rope(t, g, ones_blk):
        ss = _group_sumsq(t, ones_blk)
        tn = t * lax.rsqrt(ss * (1.0 / HEAD_DIM) + EPS) * g
        chunks = [_rope(tn[:, c * LANES:(c + 1) * LANES], cos, sin_signed, first_half)
                  for c in range(t.shape[1] // LANES)]
        return chunks

    w2 = 2 * LANES
    for half in range(D_ATT // w2):
        t = proj[:, half * w2:(half + 1) * w2]
        chunks = norm_rope(t, gq_ref[:, half * w2:(half + 1) * w2], ones2)
        for c, ch in enumerate(chunks):
            lo = half * w2 + c * LANES
            qt_ref[lo:lo + LANES, :] = ch.T.astype(BF16)
    k = proj[:, D_ATT:D_ATT + D_KV]
    (kr,) = norm_rope(k, gk_ref[...], ones1)
    k_ref[...] = kr.astype(BF16)
    vt_ref[...] = proj[:, D_ATT + D_KV:D_ATT + 2 * D_KV].T.astype(BF16)
    o = D_ATT + 2 * D_KV
    d_lru = xl_ref.shape[-1]
    xl_ref[...] = proj[:, o:o + d_lru]
    gl_ref[...] = proj[:, o + d_lru:o + 2 * d_lru]


def _in_proj(x, mod, w_in_bf, gq, gk, cos_t, sin_t, ones_blk, d_lru):
    B, S, D = x.shape
    tm = _tile(S, IN_PROJ_TILE)
    d_in = w_in_bf.shape[1]
    full = lambda shape: pl.BlockSpec(shape, lambda b, i: tuple(0 for _ in shape))
    return pl.pallas_call(
        _in_proj_kernel,
        grid=(B, S // tm),
        in_specs=[
            pl.BlockSpec((None, tm, D), lambda b, i: (b, i, 0)),
            pl.BlockSpec((None, N_MOD, D), lambda b, i: (b, 0, 0)),
            full((D, d_in)),
            full((1, D_ATT)),
            full((1, D_KV)),
            pl.BlockSpec((tm, LANES), lambda b, i: (i, 0)),
            pl.BlockSpec((tm, LANES), lambda b, i: (i, 0)),
            full((2 * LANES, 2 * LANES)),
        ],
        out_specs=[
            pl.BlockSpec((None, D_ATT, tm), lambda b, i: (b, 0, i)),
            pl.BlockSpec((None, tm, D_KV), lambda b, i: (b, i, 0)),
            pl.BlockSpec((None, D_KV, tm), lambda b, i: (b, 0, i)),
            pl.BlockSpec((None, tm, d_lru), lambda b, i: (b, i, 0)),
            pl.BlockSpec((None, tm, d_lru), lambda b, i: (b, i, 0)),
        ],
        out_shape=[
            jax.ShapeDtypeStruct((B, D_ATT, S), BF16),
            jax.ShapeDtypeStruct((B, S, D_KV), BF16),
            jax.ShapeDtypeStruct((B, D_KV, S), BF16),
            jax.ShapeDtypeStruct((B, S, d_lru), F32),
            jax.ShapeDtypeStruct((B, S, d_lru), F32),
        ],
        compiler_params=_params("parallel", "parallel"),
        name="in_proj",
    )(x, mod, w_in_bf, gq, gk, cos_t, sin_t, ones_blk)


def _attn_kernel(qt_ref, k_ref, vt_ref, o_ref, *, sub):
    tq = qt_ref.shape[-1]
    k = k_ref[...]
    zeros = jnp.zeros((HEAD_DIM, sub), BF16)
    ones_rows = jnp.ones((2 * SUBLANES, k.shape[0]), BF16)
    units = [(q0, pair) for q0 in range(0, tq, sub) for pair in range(N_Q_HEADS // 2)]

    def scores(q0, pair):
        j = (2 * pair) // Q_PER_KV
        cols = []
        for h in (2 * pair, 2 * pair + 1):
            qt = qt_ref[h * HEAD_DIM:(h + 1) * HEAD_DIM, q0:q0 + sub]
            cols.append(jnp.concatenate([qt, zeros] if j == 0 else [zeros, qt], axis=0))
        return jnp.dot(k, jnp.concatenate(cols, axis=1), preferred_element_type=F32)

    def finish(q0, pair, st):
        j = (2 * pair) // Q_PER_KV
        vt = jnp.concatenate([vt_ref[j * HEAD_DIM:(j + 1) * HEAD_DIM, :], ones_rows], axis=0)
        m = jnp.max(st, axis=0, keepdims=True)
        p = jnp.exp2(st - m)
        ol = jnp.dot(vt, p.astype(BF16), preferred_element_type=F32)
        ot = ol[:HEAD_DIM, :] / ol[HEAD_DIM:HEAD_DIM + 1, :]
        o2 = jnp.concatenate([ot[:, :sub], ot[:, sub:]], axis=0)
        o_ref[q0:q0 + sub, 2 * pair * HEAD_DIM:(2 * pair + 2) * HEAD_DIM] = o2.T.astype(BF16)

    st = scores(*units[0])
    for n, unit in enumerate(units):
        st_next = scores(*units[n + 1]) if n + 1 < len(units) else None
        finish(*unit, st)
        st = st_next


def _attention(qt, k, vt):
    B, _, S = qt.shape
    tq = _tile(S, ATTN_BLOCK)
    sub = _tile(tq, ATTN_SUB)
    return pl.pallas_call(
        functools.partial(_attn_kernel, sub=sub),
        grid=(B, S // tq),
        in_specs=[
            pl.BlockSpec((None, D_ATT, tq), lambda b, i: (b, 0, i)),
            pl.BlockSpec((None, S, D_KV), lambda b, i: (b, 0, 0)),
            pl.BlockSpec((None, D_KV, S), lambda b, i: (b, 0, 0)),
        ],
        out_specs=pl.BlockSpec((None, tq, D_ATT), lambda b, i: (b, i, 0)),
        out_shape=jax.ShapeDtypeStruct((B, S, D_ATT), BF16),
        compiler_params=_params("parallel", "parallel"),
        name="attn",
    )(qt, k, vt)


def _lru_kernel(xl_ref, gl_ref, cw_ref, cb_ref, w_ref, bias_ref, lam_ref, o_ref,
                xc_ref, af_ref, bf_ref, ab_ref, bb_ref, *, seg, pitch):
    S, C = xl_ref.shape
    n_seg = S // seg
    slabs = C // LANES
    xl = xl_ref[...]
    row = lax.broadcasted_iota(jnp.int32, (S, 1), 0)
    xm2 = jnp.where(row >= 2, pltpu.roll(xl, 2, axis=0), 0.0)
    xm1 = jnp.where(row >= 1, pltpu.roll(xl, 1, axis=0), 0.0)
    xp1 = jnp.where(row < S - 1, pltpu.roll(xl, S - 1, axis=0), 0.0)
    xc_ref[...] = (xm2 * cw_ref[0:1, :] + xm1 * cw_ref[1:2, :] + xl * cw_ref[2:3, :]
                   + xp1 * cw_ref[3:4, :] + cb_ref[...])

    lam = lam_ref[...]
    nlam = -lam
    softplus = jnp.maximum(nlam, 0.0) + jnp.log(1.0 + jnp.exp(-jnp.abs(nlam)))
    decay_log2 = (-LRU_C * LOG2_E) * softplus

    for s in range(n_seg):
        xc = xc_ref[s * seg:(s + 1) * seg, :]
        g = jnp.dot(xc.astype(BF16), w_ref[...], preferred_element_type=F32) + bias_ref[...]
        for d, (a_ref, b_ref) in enumerate(((af_ref, bf_ref), (ab_ref, bb_ref))):
            r = jax.nn.sigmoid(g[:, (2 * d) * C:(2 * d + 1) * C])
            i = jax.nn.sigmoid(g[:, (2 * d + 1) * C:(2 * d + 2) * C])
            a = jnp.exp2(r * decay_log2[d:d + 1, :])
            gain_sq = 1.0 - a * a
            gain = gain_sq * lax.rsqrt(jnp.maximum(gain_sq, SQRT_FLOOR))
            bt = gain * i * xc
            for sl in range(slabs):
                a_ref[sl, s * pitch:s * pitch + seg, :] = a[:, sl * LANES:(sl + 1) * LANES]
                b_ref[sl, s * pitch:s * pitch + seg, :] = bt[:, sl * LANES:(sl + 1) * LANES]

    def strided(ref, sl, i):
        return ref.at[sl][pl.ds(i, n_seg, stride=pitch), :]

    def put(ref, sl, i, v):
        ref.at[sl][pl.ds(i, n_seg, stride=pitch), :] = v

    def step(i, carry):
        ib = seg - 1 - i
        new = []
        for sl in range(slabs):
            hf, cf, hb, cb = carry[sl]
            a = strided(af_ref, sl, i)
            hf = a * hf + strided(bf_ref, sl, i)
            cf = a * cf
            put(bf_ref, sl, i, hf)
            put(af_ref, sl, i, cf)
            a = strided(ab_ref, sl, ib)
            hb = a * hb + strided(bb_ref, sl, ib)
            cb = a * cb
            put(bb_ref, sl, ib, hb)
            put(ab_ref, sl, ib, cb)
            new.append((hf, cf, hb, cb))
        return tuple(new)

    zero = jnp.zeros((n_seg, LANES), F32)
    one = jnp.ones((n_seg, LANES), F32)
    ends = lax.fori_loop(0, seg, step, tuple((zero, one, zero, one) for _ in range(slabs)), unroll=4)

    carry_in = []
    for sl in range(slabs):
        hf, cf, hb, cb = ends[sl]
        c = jnp.zeros((1, LANES), F32)
        rows_f = []
        for s in range(n_seg):
            rows_f.append(c)
            c = hf[s:s + 1, :] + cf[s:s + 1, :] * c
        c = jnp.zeros((1, LANES), F32)
        rows_b = [None] * n_seg
        for s in reversed(range(n_seg)):
            rows_b[s] = c
            c = hb[s:s + 1, :] + cb[s:s + 1, :] * c
        carry_in.append((jnp.concatenate(rows_f, axis=0), jnp.concatenate(rows_b, axis=0)))

    def fix(i, carry):
        for sl in range(slabs):
            cin_f, cin_b = carry_in[sl]
            h = (strided(bf_ref, sl, i) + strided(af_ref, sl, i) * cin_f
                 + strided(bb_ref, sl, i) + strided(ab_ref, sl, i) * cin_b)
            put(bf_ref, sl, i, h)
        return carry

    lax.fori_loop(0, seg, fix, 0, unroll=4)

    for s in range(n_seg):
        h = jnp.concatenate([bf_ref[sl, s * pitch:s * pitch + seg, :] for sl in range(slabs)], axis=1)
        gl = gl_ref[s * seg:(s + 1) * seg, :]
        o_ref[s * seg:(s + 1) * seg, :] = (jax.nn.gelu(gl) * h).astype(BF16)


def _lru(xl, gl, conv_w, conv_b, w_gate, b_gate, lam):
    B, S, d_lru = xl.shape
    nh, C, _ = w_gate.shape
    seg = S // SUBLANES
    pitch = seg + LRU_PITCH_PAD
    col = lambda b, h: (b, 0, h)
    state = pltpu.VMEM((C // LANES, SUBLANES * pitch, LANES), F32)
    return pl.pallas_call(
        functools.partial(_lru_kernel, seg=seg, pitch=pitch),
        grid=(B, nh),
        in_specs=[
            pl.BlockSpec((None, S, C), col),
            pl.BlockSpec((None, S, C), col),
            pl.BlockSpec((CONV_W, C), lambda b, h: (0, h)),
            pl.BlockSpec((1, C), lambda b, h: (0, h)),
            pl.BlockSpec((None, C, 4 * C), lambda b, h: (h, 0, 0)),
            pl.BlockSpec((None, 1, 4 * C), lambda b, h: (h, 0, 0)),
            pl.BlockSpec((2, C), lambda b, h: (0, h)),
        ],
        out_specs=pl.BlockSpec((None, S, C), col),
        out_shape=jax.ShapeDtypeStruct((B, S, d_lru), BF16),
        scratch_shapes=[pltpu.VMEM((S, C), F32), state, state, state, state],
        compiler_params=_params("parallel", "parallel"),
        name="lru",
    )(xl, gl, conv_w, conv_b, w_gate, b_gate, lam)


def _out_proj_kernel(att_ref, lru_ref, x_ref, mod_ref, ga_ref, gr_ref, w_ref, g1_ref, b1_ref, wr_ref,
                     x1_ref, u2_ref, lg_ref):
    d_att = att_ref.shape[-1]
    an = _rms(att_ref[...].astype(F32)) * ga_ref[...]
    rn = _rms(lru_ref[...].astype(F32)) * gr_ref[...]
    mix = (jnp.dot(an.astype(BF16), w_ref[0:d_att, :], preferred_element_type=F32)
           + jnp.dot(rn.astype(BF16), w_ref[d_att:, :], preferred_element_type=F32))
    gate1 = mod_ref[2:3, :]
    shift2 = mod_ref[3:4, :]
    scale2 = mod_ref[4:5, :]
    x1 = _layer_norm(DN_ALPHA * x_ref[...] + gate1 * mix) * g1_ref[...] + b1_ref[...]
    x1_ref[...] = x1
    u2 = _layer_norm(x1) * (1.0 + scale2) + shift2
    _store_packed(u2_ref, u2)
    E = lg_ref.shape[0]
    u_hi = u2.astype(BF16)
    u_lo = (u2 - u_hi.astype(F32)).astype(BF16)
    parts = (jnp.dot(u_hi, wr_ref[...], preferred_element_type=F32)
             + jnp.dot(u_lo, wr_ref[...], preferred_element_type=F32))
    parts_t = parts.T
    lg_ref[...] = parts_t[:E, :] + parts_t[E:, :]


def _out_proj(att, lru, x, mod, ga, gr, w_out_bf, g1, b1, wr_split):
    B, S, D = x.shape
    tm = _tile(S, PROJ_TILE)
    d_att, d_lru = att.shape[-1], lru.shape[-1]
    full = lambda shape: pl.BlockSpec(shape, lambda b, i: tuple(0 for _ in shape))
    row = lambda w: pl.BlockSpec((None, tm, w), lambda b, i: (b, i, 0))
    packed = pl.BlockSpec((ROW_SPLIT, None, tm, D // (2 * ROW_SPLIT)), lambda b, i: (0, b, i, 0))
    return pl.pallas_call(
        _out_proj_kernel,
        grid=(B, S // tm),
        in_specs=[
            row(d_att), row(d_lru), row(D),
            pl.BlockSpec((None, N_MOD, D), lambda b, i: (b, 0, 0)),
            full((1, d_att)), full((1, d_lru)), full((d_att + d_lru, D)),
            full((1, D)), full((1, D)), full((D, 2 * N_EXPERTS)),
        ],
        out_specs=[row(D), packed, pl.BlockSpec((None, N_EXPERTS, tm), lambda b, i: (b, 0, i))],
        out_shape=[
            jax.ShapeDtypeStruct((B, S, D), F32),
            jax.ShapeDtypeStruct((ROW_SPLIT, B, S, D // (2 * ROW_SPLIT)), U32),
            jax.ShapeDtypeStruct((B, N_EXPERTS, S), F32),
        ],
        compiler_params=_params("parallel", "parallel"),
        name="out_proj",
    )(att, lru, x, mod, ga, gr, w_out_bf, g1, b1, wr_split)


def _route_kernel(lg_ref, bias_ref, w_ref, eid_ref, rank_ref, cnt_ref):
    E, T = lg_ref.shape
    s = jax.nn.sigmoid(lg_ref[...])
    choice = s + bias_ref[...]
    neg = -jnp.inf

    c3 = choice.reshape(N_GROUPS, GROUP_SIZE, T)
    mem = lax.broadcasted_iota(jnp.int32, c3.shape, 1)
    m1 = jnp.max(c3, axis=1, keepdims=True)
    first = jnp.min(jnp.where(c3 == m1, mem, GROUP_SIZE), axis=1, keepdims=True)
    m2 = jnp.max(jnp.where(mem == first, neg, c3), axis=1, keepdims=True)
    gs = (m1 + m2).reshape(N_GROUPS, T)

    gi = lax.broadcasted_iota(jnp.int32, gs.shape, 0)
    rank = jnp.zeros(gs.shape, jnp.int32)
    for g2 in range(N_GROUPS):
        other = gs[g2:g2 + 1, :]
        beats = (other > gs) | ((other == gs) & (gi > g2))
        rank = rank + beats.astype(jnp.int32)
    gmask = (rank < TOPK_GROUPS).astype(F32).reshape(N_GROUPS, 1, T)
    emask = jnp.broadcast_to(gmask, (N_GROUPS, GROUP_SIZE, T)).reshape(E, T) > 0.5
    cur = jnp.where(emask, choice, neg)

    ei = lax.broadcasted_iota(jnp.int32, (E, T), 0)
    sel = jnp.zeros((E, T), F32)
    eids, ws = [], []
    for _ in range(TOP_K):
        m = jnp.max(cur, axis=0, keepdims=True)
        first_e = jnp.min(jnp.where(cur == m, ei, E), axis=0, keepdims=True)
        hit = ei == first_e
        eids.append(first_e)
        ws.append(jnp.sum(jnp.where(hit, s, 0.0), axis=0, keepdims=True))
        sel = jnp.where(hit, 1.0, sel)
        cur = jnp.where(hit, neg, cur)

    denom = ws[0]
    for w in ws[1:]:
        denom = denom + w
    wk = [w / denom * ROUTED_SCALE for w in ws]
    w_ref[...] = jnp.concatenate(wk + [jnp.zeros((LANES - TOP_K, T), F32)], axis=0).T

    sel_bf = sel.astype(BF16)
    t_row = lax.broadcasted_iota(jnp.int32, (T, T), 0)
    t_col = lax.broadcasted_iota(jnp.int32, (T, T), 1)
    before = (t_row < t_col).astype(BF16)
    cum = jnp.dot(sel_bf, before, preferred_element_type=F32)
    ranks = [jnp.sum(jnp.where(ei == e, cum, 0.0), axis=0, keepdims=True).astype(jnp.int32) for e in eids]
    pad = [jnp.zeros((SUBLANES - TOP_K, T), jnp.int32)]
    eid_ref[...] = jnp.concatenate(eids + pad, axis=0)
    rank_ref[...] = jnp.concatenate(ranks + pad, axis=0)
    cnt_ref[...] = lax.dot_general(jnp.ones((SUBLANES, T), BF16), sel_bf, (((1,), (1,)), ((), ())),
                                   preferred_element_type=F32)


def _route(logits_t, router_bias):
    B, E, S = logits_t.shape
    tr = _tile(S, ROUTE_TILE)
    nt = S // tr
    tok = lambda b, i: (b, 0, i)
    return pl.pallas_call(
        _route_kernel,
        grid=(B, nt),
        in_specs=[
            pl.BlockSpec((None, E, tr), tok),
            pl.BlockSpec((E, 1), lambda b, i: (0, 0)),
        ],
        out_specs=[
            pl.BlockSpec((None, tr, LANES), lambda b, i: (b, i, 0)),
            pl.BlockSpec((None, SUBLANES, tr), tok),
            pl.BlockSpec((None, SUBLANES, tr), tok),
            pl.BlockSpec((None, None, SUBLANES, E), lambda b, i: (b, i, 0, 0)),
        ],
        out_shape=[
            jax.ShapeDtypeStruct((B, S, LANES), F32),
            jax.ShapeDtypeStruct((B, SUBLANES, S), jnp.int32),
            jax.ShapeDtypeStruct((B, SUBLANES, S), jnp.int32),
            jax.ShapeDtypeStruct((B, nt, SUBLANES, E), F32),
        ],
        compiler_params=_params("parallel", "parallel"),
        name="route",
    )(logits_t, router_bias.reshape(E, 1))


def _slots_kernel(eid_ref, rank_ref, base_ref, pos_ref):
    nt, E, _ = base_ref.shape
    T = eid_ref.shape[-1] // nt
    ei = lax.broadcasted_iota(jnp.int32, (E, T), 0)
    pad = jnp.zeros((SUBLANES - TOP_K, T), jnp.int32)
    for t in range(nt):
        base = base_ref[t]
        eid = eid_ref[:, t * T:(t + 1) * T]
        rows = [jnp.sum(jnp.where(ei == eid[k:k + 1, :], base, 0), axis=0, keepdims=True)
                for k in range(TOP_K)]
        pos_ref[:, t * T:(t + 1) * T] = jnp.concatenate(rows + [pad], axis=0) + rank_ref[:, t * T:(t + 1) * T]


def _slots(eid, rank, base):
    B, _, S = eid.shape
    nt, E = base.shape[1], base.shape[2]
    seq = lambda b: (b, 0, 0)
    return pl.pallas_call(
        _slots_kernel,
        grid=(B,),
        in_specs=[
            pl.BlockSpec((None, SUBLANES, S), seq),
            pl.BlockSpec((None, SUBLANES, S), seq),
            pl.BlockSpec((None, nt, E, 1), lambda b: (b, 0, 0, 0)),
        ],
        out_specs=pl.BlockSpec((None, SUBLANES, S), seq),
        out_shape=jax.ShapeDtypeStruct((B, SUBLANES, S), jnp.int32),
        compiler_params=_params("parallel"),
        name="slots",
    )(eid, rank, base.reshape(B, nt, E, 1))


def _sc_mesh():
    return plsc.VectorSubcoreMesh(core_axis_name="core", subcore_axis_name="subcore")


def _sc_scatter_rows(src, idx2d, n_out):
    R, W = src.shape
    K = idx2d.shape[0]

    @functools.partial(pl.kernel, out_type=jax.ShapeDtypeStruct((n_out, W), src.dtype), mesh=_sc_mesh(),
                       scratch_types=[], name="sc_dispatch")
    def scatter(x_hbm, i_hbm, o_hbm):
        def body(x_vmem, i_vmem):
            pltpu.sync_copy(x_vmem, o_hbm.at[i_vmem.at[0]])

        pltpu.emit_pipeline(
            body, grid=(R // SC_WINDOW, K),
            in_specs=[pl.BlockSpec((SC_WINDOW, W), lambda i, k: (i, 0)),
                      pl.BlockSpec((1, SC_WINDOW), lambda i, k: (k, i))],
            out_specs=[],
            core_axis_name=("core", "subcore"),
            dimension_semantics=(pltpu.PARALLEL, pltpu.ARBITRARY),
        )(x_hbm, i_hbm)

    return scatter(src, idx2d)


def _sc_gather_rows(table, idx):
    N = idx.shape[0]
    W = table.shape[1]

    @functools.partial(pl.kernel, out_type=jax.ShapeDtypeStruct((N, W), table.dtype), mesh=_sc_mesh(),
                       scratch_types=[], name="sc_combine")
    def gather(t_hbm, i_hbm, o_hbm):
        def body(i_vmem, o_vmem):
            pltpu.sync_copy(t_hbm.at[i_vmem.at[0]], o_vmem)

        pltpu.emit_pipeline(
            body, grid=(N // SC_WINDOW,),
            in_specs=[pl.BlockSpec((1, SC_WINDOW), lambda i: (0, i))],
            out_specs=[pl.BlockSpec((SC_WINDOW, W), lambda i: (i, 0))],
            core_axis_name=("core", "subcore"),
            dimension_semantics=(pltpu.PARALLEL,),
        )(i_hbm, o_hbm)

    return gather(table, idx.reshape(1, N))


def _swiglu_hidden(u, w_in):
    gu = jnp.dot(u, w_in, preferred_element_type=F32)
    d = gu.shape[-1] // 2
    g, up = gu[:, :d], gu[:, d:]
    return g * jax.nn.sigmoid(g) * up


def _experts_kernel(blk_ref, used_ref, xs_ref, wi_ref, wo_ref, ys_ref, wi_bf, wo_bf):
    nb = pl.program_id(0)

    @pl.when((nb == 0) | (blk_ref[nb] != blk_ref[jnp.maximum(nb - 1, 0)]))
    def _():
        wi_bf[...] = wi_ref[...].astype(BF16)
        wo_bf[...] = wo_ref[...].astype(BF16)

    @pl.when(nb < used_ref[0])
    def _():
        x = _load_packed(xs_ref).astype(BF16)
        h = _swiglu_hidden(x, wi_bf[...])
        _store_packed(ys_ref, jnp.dot(h.astype(BF16), wo_bf[...], preferred_element_type=F32))


def _experts(xs, blk_expert, n_used, wei, weo, bm):
    _, P, Wd = xs.shape
    E, D, d2 = wei.shape
    de = weo.shape[1]
    rows = pl.BlockSpec((ROW_SPLIT, bm, Wd), lambda nb, blk, used: (0, jnp.minimum(nb, used[0] - 1), 0))
    return pl.pallas_call(
        _experts_kernel,
        grid_spec=pltpu.PrefetchScalarGridSpec(
            num_scalar_prefetch=2,
            grid=(P // bm,),
            in_specs=[
                rows,
                pl.BlockSpec((None, D, d2), lambda nb, blk, used: (blk[nb], 0, 0)),
                pl.BlockSpec((None, de, D), lambda nb, blk, used: (blk[nb], 0, 0)),
            ],
            out_specs=rows,
            scratch_shapes=[pltpu.VMEM((D, d2), BF16), pltpu.VMEM((de, D), BF16)],
        ),
        out_shape=jax.ShapeDtypeStruct(xs.shape, U32),
        compiler_params=_params("arbitrary"),
        name="experts",
    )(blk_expert, n_used, xs, wei, weo)


def _moe_out_kernel(u_ref, x1_ref, yg_ref, w_ref, mod_ref, wsi_ref, wso_ref, g2_ref, b2_ref, o_ref):
    u = _load_packed(u_ref).astype(BF16)
    hs = _swiglu_hidden(u, wsi_ref[...])
    acc = jnp.dot(hs.astype(BF16), wso_ref[...], preferred_element_type=F32)
    w = w_ref[...]
    for k in range(TOP_K):
        acc = acc + w[:, k:k + 1] * _load_packed(yg_ref.at[k])
    gate2 = mod_ref[5:6, :]
    y = DN_ALPHA * x1_ref[...] + gate2 * acc
    o_ref[...] = _layer_norm(y) * g2_ref[...] + b2_ref[...]


def _moe_out(u2w, x1, yg, w_t, mod, wsi, wso, g2, b2):
    B, S, D = x1.shape
    Wd = u2w.shape[-1]
    tm = _tile(S, MOE_OUT_TILE)
    full = lambda shape: pl.BlockSpec(shape, lambda b, i: tuple(0 for _ in shape))
    return pl.pallas_call(
        _moe_out_kernel,
        grid=(B, S // tm),
        in_specs=[
            pl.BlockSpec((ROW_SPLIT, None, tm, Wd), lambda b, i: (0, b, i, 0)),
            pl.BlockSpec((None, tm, D), lambda b, i: (b, i, 0)),
            pl.BlockSpec((TOP_K, ROW_SPLIT, None, tm, Wd), lambda b, i: (0, 0, b, i, 0)),
            pl.BlockSpec((None, tm, LANES), lambda b, i: (b, i, 0)),
            pl.BlockSpec((None, N_MOD, D), lambda b, i: (b, 0, 0)),
            full(wsi.shape), full(wso.shape), full((1, D)), full((1, D)),
        ],
        out_specs=pl.BlockSpec((None, tm, D), lambda b, i: (b, i, 0)),
        out_shape=jax.ShapeDtypeStruct((B, S, D), F32),
        compiler_params=_params("parallel", "parallel"),
        name="moe_out",
    )(u2w, x1, yg, w_t, mod, wsi, wso, g2, b2)


def _routed_experts(u2w, eid, rank, cnt, wei, weo):
    _, B, S, Wd = u2w.shape
    T = B * S
    E = wei.shape[0]
    bm = _tile(S, EXPERT_BLOCK)
    n_blocks = (T * TOP_K) // bm + E
    P = n_blocks * bm

    cnt_te = cnt[:, :, 0, :].astype(jnp.int32).reshape(-1, E)
    total = jnp.sum(cnt_te, axis=0)
    padded = (total + bm - 1) // bm * bm
    region_end = jnp.cumsum(padded)
    region_start = region_end - padded
    base = region_start[None, :] + jnp.cumsum(cnt_te, axis=0) - cnt_te
    blk_start = jnp.arange(n_blocks, dtype=jnp.int32) * bm
    blk_expert = jnp.minimum(jnp.sum(region_end[None, :] <= blk_start[:, None], axis=1), E - 1).astype(jnp.int32)
    n_used = (region_end[-1] // bm).astype(jnp.int32).reshape(1)

    pos = _slots(eid, rank, base.reshape(B, -1, E))
    pos = jnp.transpose(pos[:, :TOP_K, :], (1, 0, 2)).reshape(TOP_K, T)
    halves = jnp.stack([pos + h * P for h in range(ROW_SPLIT)], axis=1)

    xs = _sc_scatter_rows(u2w.reshape(ROW_SPLIT * T, Wd), halves.reshape(TOP_K, ROW_SPLIT * T), ROW_SPLIT * P)
    ys = _experts(xs.reshape(ROW_SPLIT, P, Wd), blk_expert, n_used, wei, weo, bm)
    yg = _sc_gather_rows(ys.reshape(ROW_SPLIT * P, Wd), halves.reshape(-1))
    return yg.reshape(TOP_K, ROW_SPLIT, B, S, Wd)


def _rope_tables(S):
    rows = S // GRID_W
    row_idx = jnp.repeat(jnp.arange(rows, dtype=F32), GRID_W)
    col_idx = jnp.tile(jnp.arange(GRID_W, dtype=F32), rows)
    inv_freq = ROPE_THETA ** (-jnp.arange(ROPE_FREQS, dtype=F32) / ROPE_FREQS)
    ang = jnp.stack([row_idx[:, None] * inv_freq, col_idx[:, None] * inv_freq], axis=1)
    cos, sin = jnp.cos(ang), jnp.sin(ang)
    cos_h = jnp.stack([cos, cos], axis=2).reshape(S, HEAD_DIM)
    sin_h = jnp.stack([-sin, sin], axis=2).reshape(S, HEAD_DIM)
    reps = LANES // HEAD_DIM
    return jnp.tile(cos_h, (1, reps)), jnp.tile(sin_h, (1, reps))


def _gate_weights(lru_wa, lru_ba, lru_wx, lru_bx, C):
    nb, blk, _ = lru_wa.shape[1:]
    d_lru = nb * blk
    per = C // blk

    def dense(w):
        w = w.reshape(d_lru // C, per, blk, blk)
        eye = jnp.eye(per, dtype=w.dtype)
        return jnp.einsum("hpcd,pq->hpcqd", w, eye).reshape(d_lru // C, C, C)

    w = jnp.concatenate([dense(lru_wa[0]), dense(lru_wx[0]), dense(lru_wa[1]), dense(lru_wx[1])], axis=-1)
    halves = lambda v: v.reshape(d_lru // C, 1, C)
    b = jnp.concatenate([halves(lru_ba[0]), halves(lru_bx[0]), halves(lru_ba[1]), halves(lru_bx[1])], axis=-1)
    return w.astype(BF16), b


def _split_bf16(w):
    hi = w.astype(BF16)
    lo = (w - hi.astype(F32)).astype(BF16)
    return jnp.concatenate([hi, lo], axis=1)


def _encoder(x, c, P):
    B, S, D = x.shape
    mod = _ada(c, P["w_ada"], P["b_ada"]).reshape(B, N_MOD, D)
    cos_t, sin_t = _rope_tables(S)
    qt, k, vt, xl, gl = _in_proj(x, mod, P["w_in"], P["gq"], P["gk"], cos_t, sin_t, P["ones_blk"], P["d_lru"])
    att = _attention(qt, k, vt)
    lru = _lru(xl, gl, P["conv_w"], P["conv_b"], P["w_gate"], P["b_gate"], P["lam"])
    x1, u2w, logits_t = _out_proj(att, lru, x, mod, P["ga"], P["gr"], P["w_out"], P["ln1_g"], P["ln1_b"], P["wr_split"])
    w_t, eid, rank, cnt = _route(logits_t, P["router_bias"])
    yg = _routed_experts(u2w, eid, rank, cnt, P["w_exp_in"], P["w_exp_out"])
    return _moe_out(u2w, x1, yg, w_t, mod, P["w_sh_in"], P["w_sh_out"], P["ln2_g"], P["ln2_b"])


def kernel(x_prompt, x_sample, c_prompt, c_sample, w_ada, b_ada, w_in, q_norm_g, k_norm_g, conv_w, conv_b, lru_wa, lru_ba, lru_wx, lru_bx, lru_lambda, attn_out_g, lru_out_g, w_out, ln1_g, ln1_b, w_router, router_bias, w_exp_in, w_exp_out, w_sh_in, w_sh_out, ln2_g, ln2_b):
    l = 0
    D = x_prompt.shape[-1]
    d_lru = lru_out_g.shape[-1]
    lru_half = 2 * LANES
    w_gate, b_gate = _gate_weights(lru_wa[l], lru_ba[l], lru_wx[l], lru_bx[l], lru_half)
    idx = jnp.arange(2 * LANES)
    ones_blk = (idx[:, None] // HEAD_DIM == idx[None, :] // HEAD_DIM).astype(BF16)
    P = {
        "w_ada": w_ada[l], "b_ada": b_ada[l],
        "w_in": w_in[l].astype(BF16),
        "gq": (jnp.tile(q_norm_g[l], N_Q_HEADS) * (HEAD_DIM ** -0.5 * LOG2_E)).reshape(1, D_ATT),
        "gk": jnp.tile(k_norm_g[l], N_KV_HEADS).reshape(1, D_KV),
        "ones_blk": ones_blk, "d_lru": d_lru,
        "conv_w": conv_w[l], "conv_b": conv_b[l].reshape(1, d_lru),
        "w_gate": w_gate, "b_gate": b_gate, "lam": lru_lambda[l],
        "ga": attn_out_g[l].reshape(1, D_ATT), "gr": lru_out_g[l].reshape(1, d_lru),
        "w_out": w_out[l].astype(BF16),
        "ln1_g": ln1_g[l].reshape(1, D), "ln1_b": ln1_b[l].reshape(1, D),
        "wr_split": _split_bf16(w_router[l]), "router_bias": router_bias[l],
        "w_exp_in": w_exp_in[l], "w_exp_out": w_exp_out[l],
        "w_sh_in": w_sh_in[l].astype(BF16), "w_sh_out": w_sh_out[l].astype(BF16),
        "ln2_g": ln2_g[l].reshape(1, D), "ln2_b": ln2_b[l].reshape(1, D),
    }
    return (_encoder(x_prompt, c_prompt, P), _encoder(x_sample, c_sample, P))
```

```python
import functools

import jax
import jax.numpy as jnp
from jax import lax
from jax.experimental import pallas as pl
from jax.experimental.pallas import tpu as pltpu
from jax.experimental.pallas import tpu_sc as plsc

F32 = jnp.float32
BF16 = jnp.bfloat16
U32 = jnp.uint32
HIGHEST = lax.Precision.HIGHEST

HEAD_DIM = 64
N_Q_HEADS = 8
N_KV_HEADS = 2
Q_PER_KV = N_Q_HEADS // N_KV_HEADS
D_ATT = N_Q_HEADS * HEAD_DIM
D_KV = N_KV_HEADS * HEAD_DIM
CONV_W = 4
LRU_C = 8.0
GRID_W = 64
ROPE_THETA = 10000.0
ROPE_FREQS = HEAD_DIM // 4
N_EXPERTS = 64
TOP_K = 6
N_GROUPS = 8
GROUP_SIZE = N_EXPERTS // N_GROUPS
TOPK_GROUPS = 4
ROUTED_SCALE = 2.5
N_MOD = 6
EPS = 1e-6
DEPTH = 1
DN_ALPHA = (2.0 * DEPTH) ** 0.25
LOG2_E = 1.4426950408889634
LRU_PITCH_PAD = 4
SQRT_FLOOR = 1e-30

LANES = 128
SUBLANES = 8
VMEM_LIMIT_BYTES = 56 * 1024 * 1024

ROW_SPLIT = 2
SC_WINDOW = 128
ATTN_BLOCK = 2048
ATTN_SUB = 512
ROUTE_TILE = 512
EXPERT_BLOCK = 1024
IN_PROJ_TILE = 512
PROJ_TILE = 1024
MOE_OUT_TILE = 512


def _tile(n, pref):
    if n <= pref:
        return n
    for t in range(pref, 0, -1):
        if n % t == 0 and t % SUBLANES == 0:
            return t
    return n


def _params(*sem):
    return pltpu.CompilerParams(dimension_semantics=sem, vmem_limit_bytes=VMEM_LIMIT_BYTES)


def _layer_norm(x):
    mu = jnp.mean(x, axis=-1, keepdims=True)
    xc = x - mu
    var = jnp.mean(xc * xc, axis=-1, keepdims=True)
    return xc * lax.rsqrt(var + EPS)


def _rms(x):
    return x * lax.rsqrt(jnp.mean(x * x, axis=-1, keepdims=True) + EPS)


def _pack_pair(a, b):
    ah = lax.bitcast_convert_type(a.astype(BF16).astype(F32), U32)
    bh = lax.bitcast_convert_type(b.astype(BF16).astype(F32), U32)
    return ah | (bh >> 16)


def _unpack_pair(w):
    a = lax.bitcast_convert_type(w & jnp.uint32(0xFFFF0000), F32)
    b = lax.bitcast_convert_type(w << 16, F32)
    return a, b


def _store_packed(ref, v):
    q = v.shape[-1] // (2 * ROW_SPLIT)
    for h in range(ROW_SPLIT):
        ref[h] = _pack_pair(v[:, 2 * h * q:(2 * h + 1) * q], v[:, (2 * h + 1) * q:(2 * h + 2) * q])


def _load_packed(ref):
    parts = []
    for h in range(ROW_SPLIT):
        parts.extend(_unpack_pair(ref[h]))
    return jnp.concatenate(parts, axis=-1)


def _ada_kernel(c_ref, w_ref, b_ref, o_ref):
    c = c_ref[...]
    sc = c * jax.nn.sigmoid(c)
    o_ref[...] = jnp.dot(sc, w_ref[...], precision=HIGHEST, preferred_element_type=F32) + b_ref[...]


def _ada(c, w_ada, b_ada):
    B, D = c.shape
    N = w_ada.shape[1]
    tn = _tile(N, 1024)
    return pl.pallas_call(
        _ada_kernel,
        grid=(N // tn,),
        in_specs=[
            pl.BlockSpec((B, D), lambda j: (0, 0)),
            pl.BlockSpec((D, tn), lambda j: (0, j)),
            pl.BlockSpec((1, tn), lambda j: (0, j)),
        ],
        out_specs=pl.BlockSpec((B, tn), lambda j: (0, j)),
        out_shape=jax.ShapeDtypeStruct((B, N), F32),
        compiler_params=_params("arbitrary"),
        name="ada",
    )(c, w_ada, b_ada.reshape(1, N))


def _group_sumsq(t, ones_blk):
    sq = t * t
    hi = sq.astype(BF16)
    lo = (sq - hi.astype(F32)).astype(BF16)
    return (jnp.dot(hi, ones_blk, preferred_element_type=F32)
            + jnp.dot(lo, ones_blk, preferred_element_type=F32))


def _rope(t, cos, sin_signed, first_half):
    fwd = pltpu.roll(t, LANES - ROPE_FREQS, axis=1)
    bwd = pltpu.roll(t, ROPE_FREQS, axis=1)
    return t * cos + jnp.where(first_half, fwd, bwd) * sin_signed


def _in_proj_kernel(x_ref, mod_ref, w_ref, gq_ref, gk_ref, cos_ref, sin_ref, ones_ref,
                    qt_ref, k_ref, vt_ref, xl_ref, gl_ref):
    x = x_ref[...]
    shift1 = mod_ref[0:1, :]
    scale1 = mod_ref[1:2, :]
    u = _layer_norm(x) * (1.0 + scale1) + shift1
    proj = jnp.dot(u.astype(BF16), w_ref[...], preferred_element_type=F32)

    cos = cos_ref[...]
    sin_signed = sin_ref[...]
    lane = lax.broadcasted_iota(jnp.int32, cos.shape, 1)
    first_half = (lane % (2 * ROPE_FREQS)) < ROPE_FREQS
    ones2 = ones_ref[...]
    ones1 = ones2[0:LANES, 0:LANES]

    def norm_rope(t, g, ones_blk):
        ss = _group_sumsq(t, ones_blk)
        tn = t * lax.rsqrt(ss * (1.0 / HEAD_DIM) + EPS) * g
        chunks = [_rope(tn[:, c * LANES:(c + 1) * LANES], cos, sin_signed, first_half)
                  for c in range(t.shape[1] // LANES)]
        return chunks

    w2 = 2 * LANES
    for half in range(D_ATT // w2):
        t = proj[:, half * w2:(half + 1) * w2]
        chunks = norm_rope(t, gq_ref[:, half * w2:(half + 1) * w2], ones2)
        for c, ch in enumerate(chunks):
            lo = half * w2 + c * LANES
            qt_ref[lo:lo + LANES, :] = ch.T.astype(BF16)
    k = proj[:, D_ATT:D_ATT + D_KV]
    (kr,) = norm_rope(k, gk_ref[...], ones1)
    k_ref[...] = kr.astype(BF16)
    vt_ref[...] = proj[:, D_ATT + D_KV:D_ATT + 2 * D_KV].T.astype(BF16)
    o = D_ATT + 2 * D_KV
    d_lru = xl_ref.shape[-1]
    xl_ref[...] = proj[:, o:o + d_lru]
    gl_ref[...] = proj[:, o + d_lru:o + 2 * d_lru]


def _in_proj(x, mod, w_in_bf, gq, gk, cos_t, sin_t, ones_blk, d_lru):
    B, S, D = x.shape
    tm = _tile(S, IN_PROJ_TILE)
    d_in = w_in_bf.shape[1]
    full = lambda shape: pl.BlockSpec(shape, lambda b, i: tuple(0 for _ in shape))
    return pl.pallas_call(
        _in_proj_kernel,
        grid=(B, S // tm),
        in_specs=[
            pl.BlockSpec((None, tm, D), lambda b, i: (b, i, 0)),
            pl.BlockSpec((None, N_MOD, D), lambda b, i: (b, 0, 0)),
            full((D, d_in)),
            full((1, D_ATT)),
            full((1, D_KV)),
            pl.BlockSpec((tm, LANES), lambda b, i: (i, 0)),
            pl.BlockSpec((tm, LANES), lambda b, i: (i, 0)),
            full((2 * LANES, 2 * LANES)),
        ],
        out_specs=[
            pl.BlockSpec((None, D_ATT, tm), lambda b, i: (b, 0, i)),
            pl.BlockSpec((None, tm, D_KV), lambda b, i: (b, i, 0)),
            pl.BlockSpec((None, D_KV, tm), lambda b, i: (b, 0, i)),
            pl.BlockSpec((None, tm, d_lru), lambda b, i: (b, i, 0)),
            pl.BlockSpec((None, tm, d_lru), lambda b, i: (b, i, 0)),
        ],
        out_shape=[
            jax.ShapeDtypeStruct((B, D_ATT, S), BF16),
            jax.ShapeDtypeStruct((B, S, D_KV), BF16),
            jax.ShapeDtypeStruct((B, D_KV, S), BF16),
            jax.ShapeDtypeStruct((B, S, d_lru), F32),
            jax.ShapeDtypeStruct((B, S, d_lru), F32),
        ],
        compiler_params=_params("parallel", "parallel"),
        name="in_proj",
    )(x, mod, w_in_bf, gq, gk, cos_t, sin_t, ones_blk)


def _attn_kernel(qt_ref, k_ref, vt_ref, o_ref, *, sub):
    tq = qt_ref.shape[-1]
    k = k_ref[...]
    zeros = jnp.zeros((HEAD_DIM, sub), BF16)
    ones_rows = jnp.ones((2 * SUBLANES, k.shape[0]), BF16)
    units = [(q0, pair) for q0 in range(0, tq, sub) for pair in range(N_Q_HEADS // 2)]

    def scores(q0, pair):
        j = (2 * pair) // Q_PER_KV
        cols = []
        for h in (2 * pair, 2 * pair + 1):
            qt = qt_ref[h * HEAD_DIM:(h + 1) * HEAD_DIM, q0:q0 + sub]
            cols.append(jnp.concatenate([qt, zeros] if j == 0 else [zeros, qt], axis=0))
        return jnp.dot(k, jnp.concatenate(cols, axis=1), preferred_element_type=F32)

    def finish(q0, pair, st):
        j = (2 * pair) // Q_PER_KV
        vt = jnp.concatenate([vt_ref[j * HEAD_DIM:(j + 1) * HEAD_DIM, :], ones_rows], axis=0)
        m = jnp.max(st, axis=0, keepdims=True)
        p = jnp.exp2(st - m)
        ol = jnp.dot(vt, p.astype(BF16), preferred_element_type=F32)
        ot = ol[:HEAD_DIM, :] / ol[HEAD_DIM:HEAD_DIM + 1, :]
        o2 = jnp.concatenate([ot[:, :sub], ot[:, sub:]], axis=0)
        o_ref[q0:q0 + sub, 2 * pair * HEAD_DIM:(2 * pair + 2) * HEAD_DIM] = o2.T.astype(BF16)

    st = scores(*units[0])
    for n, unit in enumerate(units):
        st_next = scores(*units[n + 1]) if n + 1 < len(units) else None
        finish(*unit, st)
        st = st_next


def _attention(qt, k, vt):
    B, _, S = qt.shape
    tq = _tile(S, ATTN_BLOCK)
    sub = _tile(tq, ATTN_SUB)
    return pl.pallas_call(
        functools.partial(_attn_kernel, sub=sub),
        grid=(B, S // tq),
        in_specs=[
            pl.BlockSpec((None, D_ATT, tq), lambda b, i: (b, 0, i)),
            pl.BlockSpec((None, S, D_KV), lambda b, i: (b, 0, 0)),
            pl.BlockSpec((None, D_KV, S), lambda b, i: (b, 0, 0)),
        ],
        out_specs=pl.BlockSpec((None, tq, D_ATT), lambda b, i: (b, i, 0)),
        out_shape=jax.ShapeDtypeStruct((B, S, D_ATT), BF16),
        compiler_params=_params("parallel", "parallel"),
        name="attn",
    )(qt, k, vt)


def _lru_kernel(xl_ref, gl_ref, cw_ref, cb_ref, w_ref, bias_ref, lam_ref, o_ref,
                xc_ref, af_ref, bf_ref, ab_ref, bb_ref, *, seg, pitch):
    S, C = xl_ref.shape
    n_seg = S // seg
    slabs = C // LANES
    xl = xl_ref[...]
    row = lax.broadcasted_iota(jnp.int32, (S, 1), 0)
    xm2 = jnp.where(row >= 2, pltpu.roll(xl, 2, axis=0), 0.0)
    xm1 = jnp.where(row >= 1, pltpu.roll(xl, 1, axis=0), 0.0)
    xp1 = jnp.where(row < S - 1, pltpu.roll(xl, S - 1, axis=0), 0.0)
    xc_ref[...] = (xm2 * cw_ref[0:1, :] + xm1 * cw_ref[1:2, :] + xl * cw_ref[2:3, :]
                   + xp1 * cw_ref[3:4, :] + cb_ref[...])

    lam = lam_ref[...]
    nlam = -lam
    softplus = jnp.maximum(nlam, 0.0) + jnp.log(1.0 + jnp.exp(-jnp.abs(nlam)))
    decay_log2 = (-LRU_C * LOG2_E) * softplus

    for s in range(n_seg):
        xc = xc_ref[s * seg:(s + 1) * seg, :]
        g = jnp.dot(xc.astype(BF16), w_ref[...], preferred_element_type=F32) + bias_ref[...]
        for d, (a_ref, b_ref) in enumerate(((af_ref, bf_ref), (ab_ref, bb_ref))):
            r = jax.nn.sigmoid(g[:, (2 * d) * C:(2 * d + 1) * C])
            i = jax.nn.sigmoid(g[:, (2 * d + 1) * C:(2 * d + 2) * C])
            a = jnp.exp2(r * decay_log2[d:d + 1, :])
            gain_sq = 1.0 - a * a
            gain = gain_sq * lax.rsqrt(jnp.maximum(gain_sq, SQRT_FLOOR))
            bt = gain * i * xc
            for sl in range(slabs):
                a_ref[sl, s * pitch:s * pitch + seg, :] = a[:, sl * LANES:(sl + 1) * LANES]
                b_ref[sl, s * pitch:s * pitch + seg, :] = bt[:, sl * LANES:(sl + 1) * LANES]

    def strided(ref, sl, i):
        return ref.at[sl][pl.ds(i, n_seg, stride=pitch), :]

    def put(ref, sl, i, v):
        ref.at[sl][pl.ds(i, n_seg, stride=pitch), :] = v

    def step(i, carry):
        ib = seg - 1 - i
        new = []
        for sl in range(slabs):
            hf, cf, hb, cb = carry[sl]
            a = strided(af_ref, sl, i)
            hf = a * hf + strided(bf_ref, sl, i)
            cf = a * cf
            put(bf_ref, sl, i, hf)
            put(af_ref, sl, i, cf)
            a = strided(ab_ref, sl, ib)
            hb = a * hb + strided(bb_ref, sl, ib)
            cb = a * cb
            put(bb_ref, sl, ib, hb)
            put(ab_ref, sl, ib, cb)
            new.append((hf, cf, hb, cb))
        return tuple(new)

    zero = jnp.zeros((n_seg, LANES), F32)
    one = jnp.ones((n_seg, LANES), F32)
    ends = lax.fori_loop(0, seg, step, tuple((zero, one, zero, one) for _ in range(slabs)), unroll=4)

    carry_in = []
    for sl in range(slabs):
        hf, cf, hb, cb = ends[sl]
        c = jnp.zeros((1, LANES), F32)
        rows_f = []
        for s in range(n_seg):
            rows_f.append(c)
            c = hf[s:s + 1, :] + cf[s:s + 1, :] * c
        c = jnp.zeros((1, LANES), F32)
        rows_b = [None] * n_seg
        for s in reversed(range(n_seg)):
            rows_b[s] = c
            c = hb[s:s + 1, :] + cb[s:s + 1, :] * c
        carry_in.append((jnp.concatenate(rows_f, axis=0), jnp.concatenate(rows_b, axis=0)))

    def fix(i, carry):
        for sl in range(slabs):
            cin_f, cin_b = carry_in[sl]
            h = (strided(bf_ref, sl, i) + strided(af_ref, sl, i) * cin_f
                 + strided(bb_ref, sl, i) + strided(ab_ref, sl, i) * cin_b)
            put(bf_ref, sl, i, h)
        return carry

    lax.fori_loop(0, seg, fix, 0, unroll=4)

    for s in range(n_seg):
        h = jnp.concatenate([bf_ref[sl, s * pitch:s * pitch + seg, :] for sl in range(slabs)], axis=1)
        gl = gl_ref[s * seg:(s + 1) * seg, :]
        o_ref[s * seg:(s + 1) * seg, :] = (jax.nn.gelu(gl) * h).astype(BF16)


def _lru(xl, gl, conv_w, conv_b, w_gate, b_gate, lam):
    B, S, d_lru = xl.shape
    nh, C, _ = w_gate.shape
    seg = S // SUBLANES
    pitch = seg + LRU_PITCH_PAD
    col = lambda b, h: (b, 0, h)
    state = pltpu.VMEM((C // LANES, SUBLANES * pitch, LANES), F32)
    return pl.pallas_call(
        functools.partial(_lru_kernel, seg=seg, pitch=pitch),
        grid=(B, nh),
        in_specs=[
            pl.BlockSpec((None, S, C), col),
            pl.BlockSpec((None, S, C), col),
            pl.BlockSpec((CONV_W, C), lambda b, h: (0, h)),
            pl.BlockSpec((1, C), lambda b, h: (0, h)),
            pl.BlockSpec((None, C, 4 * C), lambda b, h: (h, 0, 0)),
            pl.BlockSpec((None, 1, 4 * C), lambda b, h: (h, 0, 0)),
            pl.BlockSpec((2, C), lambda b, h: (0, h)),
        ],
        out_specs=pl.BlockSpec((None, S, C), col),
        out_shape=jax.ShapeDtypeStruct((B, S, d_lru), BF16),
        scratch_shapes=[pltpu.VMEM((S, C), F32), state, state, state, state],
        compiler_params=_params("parallel", "parallel"),
        name="lru",
    )(xl, gl, conv_w, conv_b, w_gate, b_gate, lam)


def _out_proj_kernel(att_ref, lru_ref, x_ref, mod_ref, ga_ref, gr_ref, w_ref, g1_ref, b1_ref, wr_ref,
                     x1_ref, u2_ref, lg_ref):
    d_att = att_ref.shape[-1]
    an = _rms(att_ref[...].astype(F32)) * ga_ref[...]
    rn = _rms(lru_ref[...].astype(F32)) * gr_ref[...]
    mix = (jnp.dot(an.astype(BF16), w_ref[0:d_att, :], preferred_element_type=F32)
           + jnp.dot(rn.astype(BF16), w_ref[d_att:, :], preferred_element_type=F32))
    gate1 = mod_ref[2:3, :]
    shift2 = mod_ref[3:4, :]
    scale2 = mod_ref[4:5, :]
    x1 = _layer_norm(DN_ALPHA * x_ref[...] + gate1 * mix) * g1_ref[...] + b1_ref[...]
    x1_ref[...] = x1
    u2 = _layer_norm(x1) * (1.0 + scale2) + shift2
    _store_packed(u2_ref, u2)
    E = lg_ref.shape[0]
    u_hi = u2.astype(BF16)
    u_lo = (u2 - u_hi.astype(F32)).astype(BF16)
    parts = (jnp.dot(u_hi, wr_ref[...], preferred_element_type=F32)
             + jnp.dot(u_lo, wr_ref[...], preferred_element_type=F32))
    parts_t = parts.T
    lg_ref[...] = parts_t[:E, :] + parts_t[E:, :]


def _out_proj(att, lru, x, mod, ga, gr, w_out_bf, g1, b1, wr_split):
    B, S, D = x.shape
    tm = _tile(S, PROJ_TILE)
    d_att, d_lru = att.shape[-1], lru.shape[-1]
    full = lambda shape: pl.BlockSpec(shape, lambda b, i: tuple(0 for _ in shape))
    row = lambda w: pl.BlockSpec((None, tm, w), lambda b, i: (b, i, 0))
    packed = pl.BlockSpec((ROW_SPLIT, None, tm, D // (2 * ROW_SPLIT)), lambda b, i: (0, b, i, 0))
    return pl.pallas_call(
        _out_proj_kernel,
        grid=(B, S // tm),
        in_specs=[
            row(d_att), row(d_lru), row(D),
            pl.BlockSpec((None, N_MOD, D), lambda b, i: (b, 0, 0)),
            full((1, d_att)), full((1, d_lru)), full((d_att + d_lru, D)),
            full((1, D)), full((1, D)), full((D, 2 * N_EXPERTS)),
        ],
        out_specs=[row(D), packed, pl.BlockSpec((None, N_EXPERTS, tm), lambda b, i: (b, 0, i))],
        out_shape=[
            jax.ShapeDtypeStruct((B, S, D), F32),
            jax.ShapeDtypeStruct((ROW_SPLIT, B, S, D // (2 * ROW_SPLIT)), U32),
            jax.ShapeDtypeStruct((B, N_EXPERTS, S), F32),
        ],
        compiler_params=_params("parallel", "parallel"),
        name="out_proj",
    )(att, lru, x, mod, ga, gr, w_out_bf, g1, b1, wr_split)


def _route_kernel(lg_ref, bias_ref, w_ref, eid_ref, rank_ref, cnt_ref):
    E, T = lg_ref.shape
    s = jax.nn.sigmoid(lg_ref[...])
    choice = s + bias_ref[...]
    neg = -jnp.inf

    c3 = choice.reshape(N_GROUPS, GROUP_SIZE, T)
    mem = lax.broadcasted_iota(jnp.int32, c3.shape, 1)
    m1 = jnp.max(c3, axis=1, keepdims=True)
    first = jnp.min(jnp.where(c3 == m1, mem, GROUP_SIZE), axis=1, keepdims=True)
    m2 = jnp.max(jnp.where(mem == first, neg, c3), axis=1, keepdims=True)
    gs = (m1 + m2).reshape(N_GROUPS, T)

    gi = lax.broadcasted_iota(jnp.int32, gs.shape, 0)
    rank = jnp.zeros(gs.shape, jnp.int32)
    for g2 in range(N_GROUPS):
        other = gs[g2:g2 + 1, :]
        beats = (other > gs) | ((other == gs) & (gi > g2))
        rank = rank + beats.astype(jnp.int32)
    gmask = (rank < TOPK_GROUPS).astype(F32).reshape(N_GROUPS, 1, T)
    emask = jnp.broadcast_to(gmask, (N_GROUPS, GROUP_SIZE, T)).reshape(E, T) > 0.5
    cur = jnp.where(emask, choice, neg)

    ei = lax.broadcasted_iota(jnp.int32, (E, T), 0)
    sel = jnp.zeros((E, T), F32)
    eids, ws = [], []
    for _ in range(TOP_K):
        m = jnp.max(cur, axis=0, keepdims=True)
        first_e = jnp.min(jnp.where(cur == m, ei, E), axis=0, keepdims=True)
        hit = ei == first_e
        eids.append(first_e)
        ws.append(jnp.sum(jnp.where(hit, s, 0.0), axis=0, keepdims=True))
        sel = jnp.where(hit, 1.0, sel)
        cur = jnp.where(hit, neg, cur)

    denom = ws[0]
    for w in ws[1:]:
        denom = denom + w
    wk = [w / denom * ROUTED_SCALE for w in ws]
    w_ref[...] = jnp.concatenate(wk + [jnp.zeros((LANES - TOP_K, T), F32)], axis=0).T

    sel_bf = sel.astype(BF16)
    t_row = lax.broadcasted_iota(jnp.int32, (T, T), 0)
    t_col = lax.broadcasted_iota(jnp.int32, (T, T), 1)
    before = (t_row < t_col).astype(BF16)
    cum = jnp.dot(sel_bf, before, preferred_element_type=F32)
    ranks = [jnp.sum(jnp.where(ei == e, cum, 0.0), axis=0, keepdims=True).astype(jnp.int32) for e in eids]
    pad = [jnp.zeros((SUBLANES - TOP_K, T), jnp.int32)]
    eid_ref[...] = jnp.concatenate(eids + pad, axis=0)
    rank_ref[...] = jnp.concatenate(ranks + pad, axis=0)
    cnt_ref[...] = lax.dot_general(jnp.ones((SUBLANES, T), BF16), sel_bf, (((1,), (1,)), ((), ())),
                                   preferred_element_type=F32)


def _route(logits_t, router_bias):
    B, E, S = logits_t.shape
    tr = _tile(S, ROUTE_TILE)
    nt = S // tr
    tok = lambda b, i: (b, 0, i)
    return pl.pallas_call(
        _route_kernel,
        grid=(B, nt),
        in_specs=[
            pl.BlockSpec((None, E, tr), tok),
            pl.BlockSpec((E, 1), lambda b, i: (0, 0)),
        ],
        out_specs=[
            pl.BlockSpec((None, tr, LANES), lambda b, i: (b, i, 0)),
            pl.BlockSpec((None, SUBLANES, tr), tok),
            pl.BlockSpec((None, SUBLANES, tr), tok),
            pl.BlockSpec((None, None, SUBLANES, E), lambda b, i: (b, i, 0, 0)),
        ],
        out_shape=[
            jax.ShapeDtypeStruct((B, S, LANES), F32),
            jax.ShapeDtypeStruct((B, SUBLANES, S), jnp.int32),
            jax.ShapeDtypeStruct((B, SUBLANES, S), jnp.int32),
            jax.ShapeDtypeStruct((B, nt, SUBLANES, E), F32),
        ],
        compiler_params=_params("parallel", "parallel"),
        name="route",
    )(logits_t, router_bias.reshape(E, 1))


def _slots_kernel(eid_ref, rank_ref, base_ref, pos_ref):
    nt, E, _ = base_ref.shape
    T = eid_ref.shape[-1] // nt
    ei = lax.broadcasted_iota(jnp.int32, (E, T), 0)
    pad = jnp.zeros((SUBLANES - TOP_K, T), jnp.int32)
    for t in range(nt):
        base = base_ref[t]
        eid = eid_ref[:, t * T:(t + 1) * T]
        rows = [jnp.sum(jnp.where(ei == eid[k:k + 1, :], base, 0), axis=0, keepdims=True)
                for k in range(TOP_K)]
        pos_ref[:, t * T:(t + 1) * T] = jnp.concatenate(rows + [pad], axis=0) + rank_ref[:, t * T:(t + 1) * T]


def _slots(eid, rank, base):
    B, _, S = eid.shape
    nt, E = base.shape[1], base.shape[2]
    seq = lambda b: (b, 0, 0)
    return pl.pallas_call(
        _slots_kernel,
        grid=(B,),
        in_specs=[
            pl.BlockSpec((None, SUBLANES, S), seq),
            pl.BlockSpec((None, SUBLANES, S), seq),
            pl.BlockSpec((None, nt, E, 1), lambda b: (b, 0, 0, 0)),
        ],
        out_specs=pl.BlockSpec((None, SUBLANES, S), seq),
        out_shape=jax.ShapeDtypeStruct((B, SUBLANES, S), jnp.int32),
        compiler_params=_params("parallel"),
        name="slots",
    )(eid, rank, base.reshape(B, nt, E, 1))


def _sc_mesh():
    return plsc.VectorSubcoreMesh(core_axis_name="core", subcore_axis_name="subcore")


def _sc_scatter_rows(src, idx2d, n_out):
    R, W = src.shape
    K = idx2d.shape[0]

    @functools.partial(pl.kernel, out_type=jax.ShapeDtypeStruct((n_out, W), src.dtype), mesh=_sc_mesh(),
                       scratch_types=[], name="sc_dispatch")
    def scatter(x_hbm, i_hbm, o_hbm):
        def body(x_vmem, i_vmem):
            pltpu.sync_copy(x_vmem, o_hbm.at[i_vmem.at[0]])

        pltpu.emit_pipeline(
            body, grid=(R // SC_WINDOW, K),
            in_specs=[pl.BlockSpec((SC_WINDOW, W), lambda i, k: (i, 0)),
                      pl.BlockSpec((1, SC_WINDOW), lambda i, k: (k, i))],
            out_specs=[],
            core_axis_name=("core", "subcore"),
            dimension_semantics=(pltpu.PARALLEL, pltpu.ARBITRARY),
        )(x_hbm, i_hbm)

    return scatter(src, idx2d)


def _sc_gather_rows(table, idx):
    N = idx.shape[0]
    W = table.shape[1]

    @functools.partial(pl.kernel, out_type=jax.ShapeDtypeStruct((N, W), table.dtype), mesh=_sc_mesh(),
                       scratch_types=[], name="sc_combine")
    def gather(t_hbm, i_hbm, o_hbm):
        def body(i_vmem, o_vmem):
            pltpu.sync_copy(t_hbm.at[i_vmem.at[0]], o_vmem)

        pltpu.emit_pipeline(
            body, grid=(N // SC_WINDOW,),
            in_specs=[pl.BlockSpec((1, SC_WINDOW), lambda i: (0, i))],
            out_specs=[pl.BlockSpec((SC_WINDOW, W), lambda i: (i, 0))],
            core_axis_name=("core", "subcore"),
            dimension_semantics=(pltpu.PARALLEL,),
        )(i_hbm, o_hbm)

    return gather(table, idx.reshape(1, N))


def _swiglu_hidden(u, w_in):
    gu = jnp.dot(u, w_in, preferred_element_type=F32)
    d = gu.shape[-1] // 2
    g, up = gu[:, :d], gu[:, d:]
    return g * jax.nn.sigmoid(g) * up


def _experts_kernel(blk_ref, used_ref, xs_ref, wi_ref, wo_ref, ys_ref, wi_bf, wo_bf):
    nb = pl.program_id(0)

    @pl.when((nb == 0) | (blk_ref[nb] != blk_ref[jnp.maximum(nb - 1, 0)]))
    def _():
        wi_bf[...] = wi_ref[...].astype(BF16)
        wo_bf[...] = wo_ref[...].astype(BF16)

    @pl.when(nb < used_ref[0])
    def _():
        x = _load_packed(xs_ref).astype(BF16)
        h = _swiglu_hidden(x, wi_bf[...])
        _store_packed(ys_ref, jnp.dot(h.astype(BF16), wo_bf[...], preferred_element_type=F32))


def _experts(xs, blk_expert, n_used, wei, weo, bm):
    _, P, Wd = xs.shape
    E, D, d2 = wei.shape
    de = weo.shape[1]
    rows = pl.BlockSpec((ROW_SPLIT, bm, Wd), lambda nb, blk, used: (0, jnp.minimum(nb, used[0] - 1), 0))
    return pl.pallas_call(
        _experts_kernel,
        grid_spec=pltpu.PrefetchScalarGridSpec(
            num_scalar_prefetch=2,
            grid=(P // bm,),
            in_specs=[
                rows,
                pl.BlockSpec((None, D, d2), lambda nb, blk, used: (blk[nb], 0, 0)),
                pl.BlockSpec((None, de, D), lambda nb, blk, used: (blk[nb], 0, 0)),
            ],
            out_specs=rows,
            scratch_shapes=[pltpu.VMEM((D, d2), BF16), pltpu.VMEM((de, D), BF16)],
        ),
        out_shape=jax.ShapeDtypeStruct(xs.shape, U32),
        compiler_params=_params("arbitrary"),
        name="experts",
    )(blk_expert, n_used, xs, wei, weo)


def _moe_out_kernel(u_ref, x1_ref, yg_ref, w_ref, mod_ref, wsi_ref, wso_ref, g2_ref, b2_ref, o_ref):
    u = _load_packed(u_ref).astype(BF16)
    hs = _swiglu_hidden(u, wsi_ref[...])
    acc = jnp.dot(hs.astype(BF16), wso_ref[...], preferred_element_type=F32)
    w = w_ref[...]
    for k in range(TOP_K):
        acc = acc + w[:, k:k + 1] * _load_packed(yg_ref.at[k])
    gate2 = mod_ref[5:6, :]
    y = DN_ALPHA * x1_ref[...] + gate2 * acc
    o_ref[...] = _layer_norm(y) * g2_ref[...] + b2_ref[...]


def _moe_out(u2w, x1, yg, w_t, mod, wsi, wso, g2, b2):
    B, S, D = x1.shape
    Wd = u2w.shape[-1]
    tm = _tile(S, MOE_OUT_TILE)
    full = lambda shape: pl.BlockSpec(shape, lambda b, i: tuple(0 for _ in shape))
    return pl.pallas_call(
        _moe_out_kernel,
        grid=(B, S // tm),
        in_specs=[
            pl.BlockSpec((ROW_SPLIT, None, tm, Wd), lambda b, i: (0, b, i, 0)),
            pl.BlockSpec((None, tm, D), lambda b, i: (b, i, 0)),
            pl.BlockSpec((TOP_K, ROW_SPLIT, None, tm, Wd), lambda b, i: (0, 0, b, i, 0)),
            pl.BlockSpec((None, tm, LANES), lambda b, i: (b, i, 0)),
            pl.BlockSpec((None, N_MOD, D), lambda b, i: (b, 0, 0)),
            full(wsi.shape), full(wso.shape), full((1, D)), full((1, D)),
        ],
        out_specs=pl.BlockSpec((None, tm, D), lambda b, i: (b, i, 0)),
        out_shape=jax.ShapeDtypeStruct((B, S, D), F32),
        compiler_params=_params("parallel", "parallel"),
        name="moe_out",
    )(u2w, x1, yg, w_t, mod, wsi, wso, g2, b2)


def _routed_experts(u2w, eid, rank, cnt, wei, weo):
    _, B, S, Wd = u2w.shape
    T = B * S
    E = wei.shape[0]
    bm = _tile(S, EXPERT_BLOCK)
    n_blocks = (T * TOP_K) // bm + E
    P = n_blocks * bm

    cnt_te = cnt[:, :, 0, :].astype(jnp.int32).reshape(-1, E)
    total = jnp.sum(cnt_te, axis=0)
    padded = (total + bm - 1) // bm * bm
    region_end = jnp.cumsum(padded)
    region_start = region_end - padded
    base = region_start[None, :] + jnp.cumsum(cnt_te, axis=0) - cnt_te
    blk_start = jnp.arange(n_blocks, dtype=jnp.int32) * bm
    blk_expert = jnp.minimum(jnp.sum(region_end[None, :] <= blk_start[:, None], axis=1), E - 1).astype(jnp.int32)
    n_used = (region_end[-1] // bm).astype(jnp.int32).reshape(1)

    pos = _slots(eid, rank, base.reshape(B, -1, E))
    pos = jnp.transpose(pos[:, :TOP_K, :], (1, 0, 2)).reshape(TOP_K, T)
    halves = jnp.stack([pos + h * P for h in range(ROW_SPLIT)], axis=1)

    xs = _sc_scatter_rows(u2w.reshape(ROW_SPLIT * T, Wd), halves.reshape(TOP_K, ROW_SPLIT * T), ROW_SPLIT * P)
    ys = _experts(xs.reshape(ROW_SPLIT, P, Wd), blk_expert, n_used, wei, weo, bm)
    yg = _sc_gather_rows(ys.reshape(ROW_SPLIT * P, Wd), halves.reshape(-1))
    return yg.reshape(TOP_K, ROW_SPLIT, B, S, Wd)


def _rope_tables(S):
    rows = S // GRID_W
    row_idx = jnp.repeat(jnp.arange(rows, dtype=F32), GRID_W)
    col_idx = jnp.tile(jnp.arange(GRID_W, dtype=F32), rows)
    inv_freq = ROPE_THETA ** (-jnp.arange(ROPE_FREQS, dtype=F32) / ROPE_FREQS)
    ang = jnp.stack([row_idx[:, None] * inv_freq, col_idx[:, None] * inv_freq], axis=1)
    cos, sin = jnp.cos(ang), jnp.sin(ang)
    cos_h = jnp.stack([cos, cos], axis=2).reshape(S, HEAD_DIM)
    sin_h = jnp.stack([-sin, sin], axis=2).reshape(S, HEAD_DIM)
    reps = LANES // HEAD_DIM
    return jnp.tile(cos_h, (1, reps)), jnp.tile(sin_h, (1, reps))


def _gate_weights(lru_wa, lru_ba, lru_wx, lru_bx, C):
    nb, blk, _ = lru_wa.shape[1:]
    d_lru = nb * blk
    per = C // blk

    def dense(w):
        w = w.reshape(d_lru // C, per, blk, blk)
        eye = jnp.eye(per, dtype=w.dtype)
        return jnp.einsum("hpcd,pq->hpcqd", w, eye).reshape(d_lru // C, C, C)

    w = jnp.concatenate([dense(lru_wa[0]), dense(lru_wx[0]), dense(lru_wa[1]), dense(lru_wx[1])], axis=-1)
    halves = lambda v: v.reshape(d_lru // C, 1, C)
    b = jnp.concatenate([halves(lru_ba[0]), halves(lru_bx[0]), halves(lru_ba[1]), halves(lru_bx[1])], axis=-1)
    return w.astype(BF16), b


def _split_bf16(w):
    hi = w.astype(BF16)
    lo = (w - hi.astype(F32)).astype(BF16)
    return jnp.concatenate([hi, lo], axis=1)


def _encoder(x, c, P):
    B, S, D = x.shape
    mod = _ada(c, P["w_ada"], P["b_ada"]).reshape(B, N_MOD, D)
    cos_t, sin_t = _rope_tables(S)
    qt, k, vt, xl, gl = _in_proj(x, mod, P["w_in"], P["gq"], P["gk"], cos_t, sin_t, P["ones_blk"], P["d_lru"])
    att = _attention(qt, k, vt)
    lru = _lru(xl, gl, P["conv_w"], P["conv_b"], P["w_gate"], P["b_gate"], P["lam"])
    x1, u2w, logits_t = _out_proj(att, lru, x, mod, P["ga"], P["gr"], P["w_out"], P["ln1_g"], P["ln1_b"], P["wr_split"])
    w_t, eid, rank, cnt = _route(logits_t, P["router_bias"])
    yg = _routed_experts(u2w, eid, rank, cnt, P["w_exp_in"], P["w_exp_out"])
    return _moe_out(u2w, x1, yg, w_t, mod, P["w_sh_in"], P["w_sh_out"], P["ln2_g"], P["ln2_b"])


def kernel(x_prompt, x_sample, c_prompt, c_sample, w_ada, b_ada, w_in, q_norm_g, k_norm_g, conv_w, conv_b, lru_wa, lru_ba, lru_wx, lru_bx, lru_lambda, attn_out_g, lru_out_g, w_out, ln1_g, ln1_b, w_router, router_bias, w_exp_in, w_exp_out, w_sh_in, w_sh_out, ln2_g, ln2_b):
    l = 0
    D = x_prompt.shape[-1]
    d_lru = lru_out_g.shape[-1]
    lru_half = 2 * LANES
    w_gate, b_gate = _gate_weights(lru_wa[l], lru_ba[l], lru_wx[l], lru_bx[l], lru_half)
    idx = jnp.arange(2 * LANES)
    ones_blk = (idx[:, None] // HEAD_DIM == idx[None, :] // HEAD_DIM).astype(BF16)
    P = {
        "w_ada": w_ada[l], "b_ada": b_ada[l],
        "w_in": w_in[l].astype(BF16),
        "gq": (jnp.tile(q_norm_g[l], N_Q_HEADS) * (HEAD_DIM ** -0.5 * LOG2_E)).reshape(1, D_ATT),
        "gk": jnp.tile(k_norm_g[l], N_KV_HEADS).reshape(1, D_KV),
        "ones_blk": ones_blk, "d_lru": d_lru,
        "conv_w": conv_w[l], "conv_b": conv_b[l].reshape(1, d_lru),
        "w_gate": w_gate, "b_gate": b_gate, "lam": lru_lambda[l],
        "ga": attn_out_g[l].reshape(1, D_ATT), "gr": lru_out_g[l].reshape(1, d_lru),
        "w_out": w_out[l].astype(BF16),
        "ln1_g": ln1_g[l].reshape(1, D), "ln1_b": ln1_b[l].reshape(1, D),
        "wr_split": _split_bf16(w_router[l]), "router_bias": router_bias[l],
        "w_exp_in": w_exp_in[l], "w_exp_out": w_exp_out[l],
        "w_sh_in": w_sh_in[l].astype(BF16), "w_sh_out": w_sh_out[l].astype(BF16),
        "ln2_g": ln2_g[l].reshape(1, D), "ln2_b": ln2_b[l].reshape(1, D),
    }
    return (_encoder(x_prompt, c_prompt, P), _encoder(x_sample, c_sample, P))
```

```python
import functools

import jax
import jax.numpy as jnp
from jax import lax
from jax.experimental import pallas as pl
from jax.experimental.pallas import tpu as pltpu
from jax.experimental.pallas import tpu_sc as plsc

F32 = jnp.float32
BF16 = jnp.bfloat16
U32 = jnp.uint32
HIGHEST = lax.Precision.HIGHEST

HEAD_DIM = 64
N_Q_HEADS = 8
N_KV_HEADS = 2
Q_PER_KV = N_Q_HEADS // N_KV_HEADS
D_ATT = N_Q_HEADS * HEAD_DIM
D_KV = N_KV_HEADS * HEAD_DIM
N_LRU_BLOCKS = 8
CONV_W = 4
LRU_C = 8.0
GRID_W = 64
ROPE_THETA = 10000.0
ROPE_FREQS = HEAD_DIM // 4
N_EXPERTS = 64
TOP_K = 6
N_GROUPS = 8
GROUP_SIZE = N_EXPERTS // N_GROUPS
TOPK_GROUPS = 4
ROUTED_SCALE = 2.5
N_MOD = 6
EPS = 1e-6
DEPTH = 1
DN_ALPHA = (2.0 * DEPTH) ** 0.25
LOG2_E = 1.4426950408889634
LRU_PITCH_PAD = 4
SQRT_FLOOR = 1e-30

LANES = 128
SUBLANES = 8
VMEM_LIMIT_BYTES = 56 * 1024 * 1024

ROW_SPLIT = 2
SC_WINDOW = 128
ATTN_BLOCK = 1024
ATTN_SUB = 512
ROUTE_TILE = 512
EXPERT_BLOCK = 1024
EXPERT_TAIL_SPLIT = 4
IN_PROJ_TILE = 512
PROJ_TILE = 1024
MOE_OUT_TILE = 512


def _tile(n, pref):
    if n <= pref:
        return n
    for t in range(pref, 0, -1):
        if n % t == 0 and t % SUBLANES == 0:
            return t
    return n


def _params(*sem):
    return pltpu.CompilerParams(dimension_semantics=sem, vmem_limit_bytes=VMEM_LIMIT_BYTES)


def _layer_norm(x):
    mu = jnp.mean(x, axis=-1, keepdims=True)
    xc = x - mu
    var = jnp.mean(xc * xc, axis=-1, keepdims=True)
    return xc * lax.rsqrt(var + EPS)


def _rms(x):
    return x * lax.rsqrt(jnp.mean(x * x, axis=-1, keepdims=True) + EPS)


def _pack_pair(a, b):
    ah = lax.bitcast_convert_type(a.astype(BF16).astype(F32), U32)
    bh = lax.bitcast_convert_type(b.astype(BF16).astype(F32), U32)
    return ah | (bh >> 16)


def _unpack_pair(w):
    a = lax.bitcast_convert_type(w & jnp.uint32(0xFFFF0000), F32)
    b = lax.bitcast_convert_type(w << 16, F32)
    return a, b


def _store_packed(ref, v):
    q = v.shape[-1] // (2 * ROW_SPLIT)
    for h in range(ROW_SPLIT):
        ref[h] = _pack_pair(v[:, 2 * h * q:(2 * h + 1) * q], v[:, (2 * h + 1) * q:(2 * h + 2) * q])


def _load_packed(ref):
    parts = []
    for h in range(ROW_SPLIT):
        parts.extend(_unpack_pair(ref[h]))
    return jnp.concatenate(parts, axis=-1)


def _ada_kernel(c_ref, w_ref, b_ref, o_ref):
    c = c_ref[...]
    sc = c * jax.nn.sigmoid(c)
    o_ref[...] = jnp.dot(sc, w_ref[...], precision=HIGHEST, preferred_element_type=F32) + b_ref[...]


def _ada(c, w_ada, b_ada):
    B, D = c.shape
    N = w_ada.shape[1]
    tn = _tile(N, 1024)
    return pl.pallas_call(
        _ada_kernel,
        grid=(N // tn,),
        in_specs=[
            pl.BlockSpec((B, D), lambda j: (0, 0)),
            pl.BlockSpec((D, tn), lambda j: (0, j)),
            pl.BlockSpec((1, tn), lambda j: (0, j)),
        ],
        out_specs=pl.BlockSpec((B, tn), lambda j: (0, j)),
        out_shape=jax.ShapeDtypeStruct((B, N), F32),
        compiler_params=_params("arbitrary"),
        name="ada",
    )(c, w_ada, b_ada.reshape(1, N))


def _group_sumsq(t, ones_blk):
    sq = t * t
    hi = sq.astype(BF16)
    lo = (sq - hi.astype(F32)).astype(BF16)
    return (jnp.dot(hi, ones_blk, preferred_element_type=F32)
            + jnp.dot(lo, ones_blk, preferred_element_type=F32))


def _rope(t, cos, sin_signed, first_half):
    fwd = pltpu.roll(t, LANES - ROPE_FREQS, axis=1)
    bwd = pltpu.roll(t, ROPE_FREQS, axis=1)
    return t * cos + jnp.where(first_half, fwd, bwd) * sin_signed


def _in_proj_kernel(x_ref, mod_ref, w_ref, gq_ref, gk_ref, cos_ref, sin_ref, ones_ref,
                    qt_ref, k_ref, vt_ref, xl_ref, gl_ref):
    x = x_ref[...]
    shift1 = mod_ref[0:1, :]
    scale1 = mod_ref[1:2, :]
    u = _layer_norm(x) * (1.0 + scale1) + shift1
    proj = jnp.dot(u.astype(BF16), w_ref[...], preferred_element_type=F32)

    cos = cos_ref[...]
    sin_signed = sin_ref[...]
    lane = lax.broadcasted_iota(jnp.int32, cos.shape, 1)
    first_half = (lane % (2 * ROPE_FREQS)) < ROPE_FREQS
    ones2 = ones_ref[...]
    ones1 = ones2[0:LANES, 0:LANES]

    def norm_rope(t, g, ones_blk):
        ss = _group_sumsq(t, ones_blk)
        tn = t * lax.rsqrt(ss * (1.0 / HEAD_DIM) + EPS) * g
        chunks = [_rope(tn[:, c * LANES:(c + 1) * LANES], cos, sin_signed, first_half)
                  for c in range(t.shape[1] // LANES)]
        return chunks

    w2 = 2 * LANES
    for half in range(D_ATT // w2):
        t = proj[:, half * w2:(half + 1) * w2]
        chunks = norm_rope(t, gq_ref[:, half * w2:(half + 1) * w2], ones2)
        for c, ch in enumerate(chunks):
            lo = half * w2 + c * LANES
            qt_ref[lo:lo + LANES, :] = ch.T.astype(BF16)
    k = proj[:, D_ATT:D_ATT + D_KV]
    (kr,) = norm_rope(k, gk_ref[...], ones1)
    k_ref[...] = kr.astype(BF16)
    vt_ref[...] = proj[:, D_ATT + D_KV:D_ATT + 2 * D_KV].T.astype(BF16)
    o = D_ATT + 2 * D_KV
    d_lru = xl_ref.shape[-1]
    xl_ref[...] = proj[:, o:o + d_lru]
    gl_ref[...] = proj[:, o + d_lru:o + 2 * d_lru]


def _in_proj(x, mod, w_in_bf, gq, gk, cos_t, sin_t, ones_blk, d_lru):
    B, S, D = x.shape
    tm = _tile(S, IN_PROJ_TILE)
    d_in = w_in_bf.shape[1]
    full = lambda shape: pl.BlockSpec(shape, lambda b, i: tuple(0 for _ in shape))
    return pl.pallas_call(
        _in_proj_kernel,
        grid=(B, S // tm),
        in_specs=[
            pl.BlockSpec((None, tm, D), lambda b, i: (b, i, 0)),
            pl.BlockSpec((None, N_MOD, D), lambda b, i: (b, 0, 0)),
            full((D, d_in)),
            full((1, D_ATT)),
            full((1, D_KV)),
            pl.BlockSpec((tm, LANES), lambda b, i: (i, 0)),
            pl.BlockSpec((tm, LANES), lambda b, i: (i, 0)),
            full((2 * LANES, 2 * LANES)),
        ],
        out_specs=[
            pl.BlockSpec((None, D_ATT, tm), lambda b, i: (b, 0, i)),
            pl.BlockSpec((None, tm, D_KV), lambda b, i: (b, i, 0)),
            pl.BlockSpec((None, D_KV, tm), lambda b, i: (b, 0, i)),
            pl.BlockSpec((None, tm, d_lru), lambda b, i: (b, i, 0)),
            pl.BlockSpec((None, tm, d_lru), lambda b, i: (b, i, 0)),
        ],
        out_shape=[
            jax.ShapeDtypeStruct((B, D_ATT, S), BF16),
            jax.ShapeDtypeStruct((B, S, D_KV), BF16),
            jax.ShapeDtypeStruct((B, D_KV, S), BF16),
            jax.ShapeDtypeStruct((B, S, d_lru), F32),
            jax.ShapeDtypeStruct((B, S, d_lru), F32),
        ],
        compiler_params=_params("parallel", "parallel"),
        name="in_proj",
    )(x, mod, w_in_bf, gq, gk, cos_t, sin_t, ones_blk)


def _attn_kernel(qt_ref, k_ref, vt_ref, o_ref, *, sub):
    tq = qt_ref.shape[-1]
    k = k_ref[...]
    zeros = jnp.zeros((HEAD_DIM, sub), BF16)
    ones_rows = jnp.ones((2 * SUBLANES, k.shape[0]), BF16)
    units = [(q0, pair) for q0 in range(0, tq, sub) for pair in range(N_Q_HEADS // 2)]

    def scores(q0, pair):
        j = (2 * pair) // Q_PER_KV
        cols = []
        for h in (2 * pair, 2 * pair + 1):
            qt = qt_ref[h * HEAD_DIM:(h + 1) * HEAD_DIM, q0:q0 + sub]
            cols.append(jnp.concatenate([qt, zeros] if j == 0 else [zeros, qt], axis=0))
        return jnp.dot(k, jnp.concatenate(cols, axis=1), preferred_element_type=F32)

    def finish(q0, pair, st):
        j = (2 * pair) // Q_PER_KV
        vt = jnp.concatenate([vt_ref[j * HEAD_DIM:(j + 1) * HEAD_DIM, :], ones_rows], axis=0)
        m = jnp.max(st, axis=0, keepdims=True)
        p = jnp.exp2(st - m)
        ol = jnp.dot(vt, p.astype(BF16), preferred_element_type=F32)
        ot = ol[:HEAD_DIM, :] / ol[HEAD_DIM:HEAD_DIM + 1, :]
        o2 = jnp.concatenate([ot[:, :sub], ot[:, sub:]], axis=0)
        o_ref[q0:q0 + sub, 2 * pair * HEAD_DIM:(2 * pair + 2) * HEAD_DIM] = o2.T.astype(BF16)

    st = scores(*units[0])
    for n, unit in enumerate(units):
        st_next = scores(*units[n + 1]) if n + 1 < len(units) else None
        finish(*unit, st)
        st = st_next


def _attention(qt, k, vt):
    B, _, S = qt.shape
    tq = _tile(S, ATTN_BLOCK)
    sub = _tile(tq, ATTN_SUB)
    return pl.pallas_call(
        functools.partial(_attn_kernel, sub=sub),
        grid=(B, S // tq),
        in_specs=[
            pl.BlockSpec((None, D_ATT, tq), lambda b, i: (b, 0, i)),
            pl.BlockSpec((None, S, D_KV), lambda b, i: (b, 0, 0)),
            pl.BlockSpec((None, D_KV, S), lambda b, i: (b, 0, 0)),
        ],
        out_specs=pl.BlockSpec((None, tq, D_ATT), lambda b, i: (b, i, 0)),
        out_shape=jax.ShapeDtypeStruct((B, S, D_ATT), BF16),
        compiler_params=_params("parallel", "parallel"),
        name="attn",
    )(qt, k, vt)


def _lru_kernel(xl_ref, gl_ref, cw_ref, cb_ref, w_ref, bias_ref, lam_ref, o_ref,
                xc_ref, af_ref, bf_ref, ab_ref, bb_ref, *, seg, pitch):
    S, C = xl_ref.shape
    n_seg = S // seg
    slabs = C // LANES
    xl = xl_ref[...]
    row = lax.broadcasted_iota(jnp.int32, (S, 1), 0)
    xm2 = jnp.where(row >= 2, pltpu.roll(xl, 2, axis=0), 0.0)
    xm1 = jnp.where(row >= 1, pltpu.roll(xl, 1, axis=0), 0.0)
    xp1 = jnp.where(row < S - 1, pltpu.roll(xl, S - 1, axis=0), 0.0)
    xc_ref[...] = (xm2 * cw_ref[0:1, :] + xm1 * cw_ref[1:2, :] + xl * cw_ref[2:3, :]
                   + xp1 * cw_ref[3:4, :] + cb_ref[...])

    lam = lam_ref[...]
    nlam = -lam
    softplus = jnp.maximum(nlam, 0.0) + jnp.log(1.0 + jnp.exp(-jnp.abs(nlam)))
    decay_log2 = (-LRU_C * LOG2_E) * softplus

    for s in range(n_seg):
        xc = xc_ref[s * seg:(s + 1) * seg, :]
        g = jnp.dot(xc.astype(BF16), w_ref[...], preferred_element_type=F32) + bias_ref[...]
        for d, (a_ref, b_ref) in enumerate(((af_ref, bf_ref), (ab_ref, bb_ref))):
            r = jax.nn.sigmoid(g[:, (2 * d) * C:(2 * d + 1) * C])
            i = jax.nn.sigmoid(g[:, (2 * d + 1) * C:(2 * d + 2) * C])
            a = jnp.exp2(r * decay_log2[d:d + 1, :])
            gain_sq = 1.0 - a * a
            gain = gain_sq * lax.rsqrt(jnp.maximum(gain_sq, SQRT_FLOOR))
            bt = gain * i * xc
            for sl in range(slabs):
                a_ref[sl, s * pitch:s * pitch + seg, :] = a[:, sl * LANES:(sl + 1) * LANES]
                b_ref[sl, s * pitch:s * pitch + seg, :] = bt[:, sl * LANES:(sl + 1) * LANES]

    def strided(ref, sl, i):
        return ref.at[sl][pl.ds(i, n_seg, stride=pitch), :]

    def put(ref, sl, i, v):
        ref.at[sl][pl.ds(i, n_seg, stride=pitch), :] = v

    def step(i, carry):
        ib = seg - 1 - i
        new = []
        for sl in range(slabs):
            hf, cf, hb, cb = carry[sl]
            a = strided(af_ref, sl, i)
            hf = a * hf + strided(bf_ref, sl, i)
            cf = a * cf
            put(bf_ref, sl, i, hf)
            put(af_ref, sl, i, cf)
            a = strided(ab_ref, sl, ib)
            hb = a * hb + strided(bb_ref, sl, ib)
            cb = a * cb
            put(bb_ref, sl, ib, hb)
            put(ab_ref, sl, ib, cb)
            new.append((hf, cf, hb, cb))
        return tuple(new)

    zero = jnp.zeros((n_seg, LANES), F32)
    one = jnp.ones((n_seg, LANES), F32)
    ends = lax.fori_loop(0, seg, step, tuple((zero, one, zero, one) for _ in range(slabs)), unroll=4)

    carry_in = []
    for sl in range(slabs):
        hf, cf, hb, cb = ends[sl]
        c = jnp.zeros((1, LANES), F32)
        rows_f = []
        for s in range(n_seg):
            rows_f.append(c)
            c = hf[s:s + 1, :] + cf[s:s + 1, :] * c
        c = jnp.zeros((1, LANES), F32)
        rows_b = [None] * n_seg
        for s in reversed(range(n_seg)):
            rows_b[s] = c
            c = hb[s:s + 1, :] + cb[s:s + 1, :] * c
        carry_in.append((jnp.concatenate(rows_f, axis=0), jnp.concatenate(rows_b, axis=0)))

    def fix(i, carry):
        for sl in range(slabs):
            cin_f, cin_b = carry_in[sl]
            h = (strided(bf_ref, sl, i) + strided(af_ref, sl, i) * cin_f
                 + strided(bb_ref, sl, i) + strided(ab_ref, sl, i) * cin_b)
            put(bf_ref, sl, i, h)
        return carry

    lax.fori_loop(0, seg, fix, 0, unroll=4)

    for s in range(n_seg):
        h = jnp.concatenate([bf_ref[sl, s * pitch:s * pitch + seg, :] for sl in range(slabs)], axis=1)
        gl = gl_ref[s * seg:(s + 1) * seg, :]
        o_ref[s * seg:(s + 1) * seg, :] = (jax.nn.gelu(gl) * h).astype(BF16)


def _lru(xl, gl, conv_w, conv_b, w_gate, b_gate, lam):
    B, S, d_lru = xl.shape
    nh, C, _ = w_gate.shape
    seg = S // SUBLANES
    pitch = seg + LRU_PITCH_PAD
    col = lambda b, h: (b, 0, h)
    state = pltpu.VMEM((C // LANES, SUBLANES * pitch, LANES), F32)
    return pl.pallas_call(
        functools.partial(_lru_kernel, seg=seg, pitch=pitch),
        grid=(B, nh),
        in_specs=[
            pl.BlockSpec((None, S, C), col),
            pl.BlockSpec((None, S, C), col),
            pl.BlockSpec((CONV_W, C), lambda b, h: (0, h)),
            pl.BlockSpec((1, C), lambda b, h: (0, h)),
            pl.BlockSpec((None, C, 4 * C), lambda b, h: (h, 0, 0)),
            pl.BlockSpec((None, 1, 4 * C), lambda b, h: (h, 0, 0)),
            pl.BlockSpec((2, C), lambda b, h: (0, h)),
        ],
        out_specs=pl.BlockSpec((None, S, C), col),
        out_shape=jax.ShapeDtypeStruct((B, S, d_lru), BF16),
        scratch_shapes=[pltpu.VMEM((S, C), F32), state, state, state, state],
        compiler_params=_params("parallel", "parallel"),
        name="lru",
    )(xl, gl, conv_w, conv_b, w_gate, b_gate, lam)


def _out_proj_kernel(att_ref, lru_ref, x_ref, mod_ref, ga_ref, gr_ref, w_ref, g1_ref, b1_ref, wr_ref,
                     x1_ref, u2_ref, lg_ref):
    d_att = att_ref.shape[-1]
    an = _rms(att_ref[...].astype(F32)) * ga_ref[...]
    rn = _rms(lru_ref[...].astype(F32)) * gr_ref[...]
    mix = (jnp.dot(an.astype(BF16), w_ref[0:d_att, :], preferred_element_type=F32)
           + jnp.dot(rn.astype(BF16), w_ref[d_att:, :], preferred_element_type=F32))
    gate1 = mod_ref[2:3, :]
    shift2 = mod_ref[3:4, :]
    scale2 = mod_ref[4:5, :]
    x1 = _layer_norm(DN_ALPHA * x_ref[...] + gate1 * mix) * g1_ref[...] + b1_ref[...]
    x1_ref[...] = x1
    u2 = _layer_norm(x1) * (1.0 + scale2) + shift2
    _store_packed(u2_ref, u2)
    E = lg_ref.shape[0]
    u_hi = u2.astype(BF16)
    u_lo = (u2 - u_hi.astype(F32)).astype(BF16)
    parts = (jnp.dot(u_hi, wr_ref[...], preferred_element_type=F32)
             + jnp.dot(u_lo, wr_ref[...], preferred_element_type=F32))
    parts_t = parts.T
    lg_ref[...] = parts_t[:E, :] + parts_t[E:, :]


def _out_proj(att, lru, x, mod, ga, gr, w_out_bf, g1, b1, wr_split):
    B, S, D = x.shape
    tm = _tile(S, PROJ_TILE)
    d_att, d_lru = att.shape[-1], lru.shape[-1]
    full = lambda shape: pl.BlockSpec(shape, lambda b, i: tuple(0 for _ in shape))
    row = lambda w: pl.BlockSpec((None, tm, w), lambda b, i: (b, i, 0))
    packed = pl.BlockSpec((ROW_SPLIT, None, tm, D // (2 * ROW_SPLIT)), lambda b, i: (0, b, i, 0))
    return pl.pallas_call(
        _out_proj_kernel,
        grid=(B, S // tm),
        in_specs=[
            row(d_att), row(d_lru), row(D),
            pl.BlockSpec((None, N_MOD, D), lambda b, i: (b, 0, 0)),
            full((1, d_att)), full((1, d_lru)), full((d_att + d_lru, D)),
            full((1, D)), full((1, D)), full((D, 2 * N_EXPERTS)),
        ],
        out_specs=[row(D), packed, pl.BlockSpec((None, N_EXPERTS, tm), lambda b, i: (b, 0, i))],
        out_shape=[
            jax.ShapeDtypeStruct((B, S, D), F32),
            jax.ShapeDtypeStruct((ROW_SPLIT, B, S, D // (2 * ROW_SPLIT)), U32),
            jax.ShapeDtypeStruct((B, N_EXPERTS, S), F32),
        ],
        compiler_params=_params("parallel", "parallel"),
        name="out_proj",
    )(att, lru, x, mod, ga, gr, w_out_bf, g1, b1, wr_split)


def _route_kernel(lg_ref, bias_ref, w_ref, eid_ref, rank_ref, cnt_ref):
    E, T = lg_ref.shape
    s = jax.nn.sigmoid(lg_ref[...])
    choice = s + bias_ref[...]
    neg = -jnp.inf

    c3 = choice.reshape(N_GROUPS, GROUP_SIZE, T)
    mem = lax.broadcasted_iota(jnp.int32, c3.shape, 1)
    m1 = jnp.max(c3, axis=1, keepdims=True)
    first = jnp.min(jnp.where(c3 == m1, mem, GROUP_SIZE), axis=1, keepdims=True)
    m2 = jnp.max(jnp.where(mem == first, neg, c3), axis=1, keepdims=True)
    gs = (m1 + m2).reshape(N_GROUPS, T)

    gi = lax.broadcasted_iota(jnp.int32, gs.shape, 0)
    rank = jnp.zeros(gs.shape, jnp.int32)
    for g2 in range(N_GROUPS):
        other = gs[g2:g2 + 1, :]
        beats = (other > gs) | ((other == gs) & (gi > g2))
        rank = rank + beats.astype(jnp.int32)
    gmask = (rank < TOPK_GROUPS).astype(F32).reshape(N_GROUPS, 1, T)
    emask = jnp.broadcast_to(gmask, (N_GROUPS, GROUP_SIZE, T)).reshape(E, T) > 0.5
    cur = jnp.where(emask, choice, neg)

    ei = lax.broadcasted_iota(jnp.int32, (E, T), 0)
    sel = jnp.zeros((E, T), F32)
    eids, ws = [], []
    for _ in range(TOP_K):
        m = jnp.max(cur, axis=0, keepdims=True)
        first_e = jnp.min(jnp.where(cur == m, ei, E), axis=0, keepdims=True)
        hit = ei == first_e
        eids.append(first_e)
        ws.append(jnp.sum(jnp.where(hit, s, 0.0), axis=0, keepdims=True))
        sel = jnp.where(hit, 1.0, sel)
        cur = jnp.where(hit, neg, cur)

    denom = ws[0]
    for w in ws[1:]:
        denom = denom + w
    wk = [w / denom * ROUTED_SCALE for w in ws]
    w_ref[...] = jnp.concatenate(wk + [jnp.zeros((LANES - TOP_K, T), F32)], axis=0).T

    sel_bf = sel.astype(BF16)
    t_row = lax.broadcasted_iota(jnp.int32, (T, T), 0)
    t_col = lax.broadcasted_iota(jnp.int32, (T, T), 1)
    before = (t_row < t_col).astype(BF16)
    cum = jnp.dot(sel_bf, before, preferred_element_type=F32)
    ranks = [jnp.sum(jnp.where(ei == e, cum, 0.0), axis=0, keepdims=True).astype(jnp.int32) for e in eids]
    pad = [jnp.zeros((SUBLANES - TOP_K, T), jnp.int32)]
    eid_ref[...] = jnp.concatenate(eids + pad, axis=0)
    rank_ref[...] = jnp.concatenate(ranks + pad, axis=0)
    cnt_ref[...] = lax.dot_general(jnp.ones((SUBLANES, T), BF16), sel_bf, (((1,), (1,)), ((), ())),
                                   preferred_element_type=F32)


def _route(logits_t, router_bias):
    B, E, S = logits_t.shape
    tr = _tile(S, ROUTE_TILE)
    nt = S // tr
    tok = lambda b, i: (b, 0, i)
    return pl.pallas_call(
        _route_kernel,
        grid=(B, nt),
        in_specs=[
            pl.BlockSpec((None, E, tr), tok),
            pl.BlockSpec((E, 1), lambda b, i: (0, 0)),
        ],
        out_specs=[
            pl.BlockSpec((None, tr, LANES), lambda b, i: (b, i, 0)),
            pl.BlockSpec((None, SUBLANES, tr), tok),
            pl.BlockSpec((None, SUBLANES, tr), tok),
            pl.BlockSpec((None, None, SUBLANES, E), lambda b, i: (b, i, 0, 0)),
        ],
        out_shape=[
            jax.ShapeDtypeStruct((B, S, LANES), F32),
            jax.ShapeDtypeStruct((B, SUBLANES, S), jnp.int32),
            jax.ShapeDtypeStruct((B, SUBLANES, S), jnp.int32),
            jax.ShapeDtypeStruct((B, nt, SUBLANES, E), F32),
        ],
        compiler_params=_params("parallel", "parallel"),
        name="route",
    )(logits_t, router_bias.reshape(E, 1))


def _slots_kernel(eid_ref, rank_ref, base_ref, pos_ref):
    nt, E, _ = base_ref.shape
    T = eid_ref.shape[-1] // nt
    ei = lax.broadcasted_iota(jnp.int32, (E, T), 0)
    pad = jnp.zeros((SUBLANES - TOP_K, T), jnp.int32)
    for t in range(nt):
        base = base_ref[t]
        eid = eid_ref[:, t * T:(t + 1) * T]
        rows = [jnp.sum(jnp.where(ei == eid[k:k + 1, :], base, 0), axis=0, keepdims=True)
                for k in range(TOP_K)]
        pos_ref[:, t * T:(t + 1) * T] = jnp.concatenate(rows + [pad], axis=0) + rank_ref[:, t * T:(t + 1) * T]


def _slots(eid, rank, base):
    B, _, S = eid.shape
    nt, E = base.shape[1], base.shape[2]
    seq = lambda b: (b, 0, 0)
    return pl.pallas_call(
        _slots_kernel,
        grid=(B,),
        in_specs=[
            pl.BlockSpec((None, SUBLANES, S), seq),
            pl.BlockSpec((None, SUBLANES, S), seq),
            pl.BlockSpec((None, nt, E, 1), lambda b: (b, 0, 0, 0)),
        ],
        out_specs=pl.BlockSpec((None, SUBLANES, S), seq),
        out_shape=jax.ShapeDtypeStruct((B, SUBLANES, S), jnp.int32),
        compiler_params=_params("parallel"),
        name="slots",
    )(eid, rank, base.reshape(B, nt, E, 1))


def _sc_mesh():
    return plsc.VectorSubcoreMesh(core_axis_name="core", subcore_axis_name="subcore")


def _sc_scatter_rows(src, idx2d, n_out):
    R, W = src.shape
    K = idx2d.shape[0]

    @functools.partial(pl.kernel, out_type=jax.ShapeDtypeStruct((n_out, W), src.dtype), mesh=_sc_mesh(),
                       scratch_types=[], name="sc_dispatch")
    def scatter(x_hbm, i_hbm, o_hbm):
        def body(x_vmem, i_vmem):
            pltpu.sync_copy(x_vmem, o_hbm.at[i_vmem.at[0]])

        pltpu.emit_pipeline(
            body, grid=(R // SC_WINDOW, K),
            in_specs=[pl.BlockSpec((SC_WINDOW, W), lambda i, k: (i, 0)),
                      pl.BlockSpec((1, SC_WINDOW), lambda i, k: (k, i))],
            out_specs=[],
            core_axis_name=("core", "subcore"),
            dimension_semantics=(pltpu.PARALLEL, pltpu.ARBITRARY),
        )(x_hbm, i_hbm)

    return scatter(src, idx2d)


def _sc_gather_rows(table, idx):
    N = idx.shape[0]
    W = table.shape[1]

    @functools.partial(pl.kernel, out_type=jax.ShapeDtypeStruct((N, W), table.dtype), mesh=_sc_mesh(),
                       scratch_types=[], name="sc_combine")
    def gather(t_hbm, i_hbm, o_hbm):
        def body(i_vmem, o_vmem):
            pltpu.sync_copy(t_hbm.at[i_vmem.at[0]], o_vmem)

        pltpu.emit_pipeline(
            body, grid=(N // SC_WINDOW,),
            in_specs=[pl.BlockSpec((1, SC_WINDOW), lambda i: (0, i))],
            out_specs=[pl.BlockSpec((SC_WINDOW, W), lambda i: (i, 0))],
            core_axis_name=("core", "subcore"),
            dimension_semantics=(pltpu.PARALLEL,),
        )(i_hbm, o_hbm)

    return gather(table, idx.reshape(1, N))


def _swiglu_hidden(u, w_in):
    gu = jnp.dot(u, w_in, preferred_element_type=F32)
    d = gu.shape[-1] // 2
    g, up = gu[:, :d], gu[:, d:]
    return g * jax.nn.sigmoid(g) * up


def _experts_kernel(blk_ref, valid_ref, used_ref, xs_ref, wi_ref, wo_ref, ys_ref, wi_bf, wo_bf):
    del used_ref
    nb = pl.program_id(0)
    bm = xs_ref.shape[1]
    valid = valid_ref[nb]

    @pl.when((nb == 0) | (blk_ref[nb] != blk_ref[jnp.maximum(nb - 1, 0)]))
    def _():
        wi_bf[...] = wi_ref[...].astype(BF16)
        wo_bf[...] = wo_ref[...].astype(BF16)

    def swiglu_rows(r0, n):
        parts = []
        for h in range(ROW_SPLIT):
            parts.extend(_unpack_pair(xs_ref[h, r0:r0 + n, :]))
        x = jnp.concatenate(parts, axis=-1).astype(BF16)
        hid = _swiglu_hidden(x, wi_bf[...])
        y = jnp.dot(hid.astype(BF16), wo_bf[...], preferred_element_type=F32)
        q = y.shape[-1] // (2 * ROW_SPLIT)
        for h in range(ROW_SPLIT):
            ys_ref[h, r0:r0 + n, :] = _pack_pair(y[:, 2 * h * q:(2 * h + 1) * q], y[:, (2 * h + 1) * q:(2 * h + 2) * q])

    @pl.when(valid == bm)
    def _():
        swiglu_rows(0, bm)

    sub = bm // EXPERT_TAIL_SPLIT
    for s in range(EXPERT_TAIL_SPLIT):
        @pl.when((valid < bm) & (valid > s * sub))
        def _(s=s):
            swiglu_rows(s * sub, sub)


def _experts(xs, blk_expert, blk_valid, n_used, wei, weo, bm):
    _, P, Wd = xs.shape
    E, D, d2 = wei.shape
    de = weo.shape[1]
    rows = pl.BlockSpec((ROW_SPLIT, bm, Wd), lambda nb, blk, valid, used: (0, jnp.minimum(nb, used[0] - 1), 0))
    return pl.pallas_call(
        _experts_kernel,
        grid_spec=pltpu.PrefetchScalarGridSpec(
            num_scalar_prefetch=3,
            grid=(P // bm,),
            in_specs=[
                rows,
                pl.BlockSpec((None, D, d2), lambda nb, blk, valid, used: (blk[nb], 0, 0)),
                pl.BlockSpec((None, de, D), lambda nb, blk, valid, used: (blk[nb], 0, 0)),
            ],
            out_specs=rows,
            scratch_shapes=[pltpu.VMEM((D, d2), BF16), pltpu.VMEM((de, D), BF16)],
        ),
        out_shape=jax.ShapeDtypeStruct(xs.shape, U32),
        compiler_params=_params("arbitrary"),
        name="experts",
    )(blk_expert, blk_valid, n_used, xs, wei, weo)


def _moe_out_kernel(u_ref, x1_ref, yg_ref, w_ref, mod_ref, wsi_ref, wso_ref, g2_ref, b2_ref, o_ref):
    u = _load_packed(u_ref).astype(BF16)
    hs = _swiglu_hidden(u, wsi_ref[...])
    acc = jnp.dot(hs.astype(BF16), wso_ref[...], preferred_element_type=F32)
    w = w_ref[...]
    for k in range(TOP_K):
        acc = acc + w[:, k:k + 1] * _load_packed(yg_ref.at[k])
    gate2 = mod_ref[5:6, :]
    y = DN_ALPHA * x1_ref[...] + gate2 * acc
    o_ref[...] = _layer_norm(y) * g2_ref[...] + b2_ref[...]


def _moe_out(u2w, x1, yg, w_t, mod, wsi, wso, g2, b2):
    B, S, D = x1.shape
    Wd = u2w.shape[-1]
    tm = _tile(S, MOE_OUT_TILE)
    full = lambda shape: pl.BlockSpec(shape, lambda b, i: tuple(0 for _ in shape))
    return pl.pallas_call(
        _moe_out_kernel,
        grid=(B, S // tm),
        in_specs=[
            pl.BlockSpec((ROW_SPLIT, None, tm, Wd), lambda b, i: (0, b, i, 0)),
            pl.BlockSpec((None, tm, D), lambda b, i: (b, i, 0)),
            pl.BlockSpec((TOP_K, ROW_SPLIT, None, tm, Wd), lambda b, i: (0, 0, b, i, 0)),
            pl.BlockSpec((None, tm, LANES), lambda b, i: (b, i, 0)),
            pl.BlockSpec((None, N_MOD, D), lambda b, i: (b, 0, 0)),
            full(wsi.shape), full(wso.shape), full((1, D)), full((1, D)),
        ],
        out_specs=pl.BlockSpec((None, tm, D), lambda b, i: (b, i, 0)),
        out_shape=jax.ShapeDtypeStruct((B, S, D), F32),
        compiler_params=_params("parallel", "parallel"),
        name="moe_out",
    )(u2w, x1, yg, w_t, mod, wsi, wso, g2, b2)


def _routed_experts(u2w, eid, rank, cnt, wei, weo):
    _, B, S, Wd = u2w.shape
    T = B * S
    E = wei.shape[0]
    bm = _tile(S, EXPERT_BLOCK)
    n_blocks = (T * TOP_K) // bm + E
    P = n_blocks * bm

    cnt_te = cnt[:, :, 0, :].astype(jnp.int32).reshape(-1, E)
    total = jnp.sum(cnt_te, axis=0)
    padded = (total + bm - 1) // bm * bm
    region_end = jnp.cumsum(padded)
    region_start = region_end - padded
    base = region_start[None, :] + jnp.cumsum(cnt_te, axis=0) - cnt_te
    blk_start = jnp.arange(n_blocks, dtype=jnp.int32) * bm
    blk_expert = jnp.minimum(jnp.sum(region_end[None, :] <= blk_start[:, None], axis=1), E - 1).astype(jnp.int32)
    n_used = (region_end[-1] // bm).astype(jnp.int32).reshape(1)
    real_end = (region_start + total)[blk_expert]
    blk_valid = jnp.where(blk_start < region_end[-1], jnp.clip(real_end - blk_start, 0, bm), 0).astype(jnp.int32)

    pos = _slots(eid, rank, base.reshape(B, -1, E))
    pos = jnp.transpose(pos[:, :TOP_K, :], (1, 0, 2)).reshape(TOP_K, T)
    halves = jnp.stack([pos + h * P for h in range(ROW_SPLIT)], axis=1)

    xs = _sc_scatter_rows(u2w.reshape(ROW_SPLIT * T, Wd), halves.reshape(TOP_K, ROW_SPLIT * T), ROW_SPLIT * P)
    ys = _experts(xs.reshape(ROW_SPLIT, P, Wd), blk_expert, blk_valid, n_used, wei, weo, bm)
    yg = _sc_gather_rows(ys.reshape(ROW_SPLIT * P, Wd), halves.reshape(-1))
    return yg.reshape(TOP_K, ROW_SPLIT, B, S, Wd)


def _rope_tables(S):
    rows = S // GRID_W
    row_idx = jnp.repeat(jnp.arange(rows, dtype=F32), GRID_W)
    col_idx = jnp.tile(jnp.arange(GRID_W, dtype=F32), rows)
    inv_freq = ROPE_THETA ** (-jnp.arange(ROPE_FREQS, dtype=F32) / ROPE_FREQS)
    ang = jnp.stack([row_idx[:, None] * inv_freq, col_idx[:, None] * inv_freq], axis=1)
    cos, sin = jnp.cos(ang), jnp.sin(ang)
    cos_h = jnp.stack([cos, cos], axis=2).reshape(S, HEAD_DIM)
    sin_h = jnp.stack([-sin, sin], axis=2).reshape(S, HEAD_DIM)
    reps = LANES // HEAD_DIM
    return jnp.tile(cos_h, (1, reps)), jnp.tile(sin_h, (1, reps))


def _gate_weights(lru_wa, lru_ba, lru_wx, lru_bx, C):
    nb, blk, _ = lru_wa.shape[1:]
    d_lru = nb * blk
    per = C // blk

    def dense(w):
        w = w.reshape(d_lru // C, per, blk, blk)
        eye = jnp.eye(per, dtype=w.dtype)
        return jnp.einsum("hpcd,pq->hpcqd", w, eye).reshape(d_lru // C, C, C)

    w = jnp.concatenate([dense(lru_wa[0]), dense(lru_wx[0]), dense(lru_wa[1]), dense(lru_wx[1])], axis=-1)
    halves = lambda v: v.reshape(d_lru // C, 1, C)
    b = jnp.concatenate([halves(lru_ba[0]), halves(lru_bx[0]), halves(lru_ba[1]), halves(lru_bx[1])], axis=-1)
    return w.astype(BF16), b


def _split_bf16(w):
    hi = w.astype(BF16)
    lo = (w - hi.astype(F32)).astype(BF16)
    return jnp.concatenate([hi, lo], axis=1)


def _encoder(x, c, P):
    B, S, D = x.shape
    mod = _ada(c, P["w_ada"], P["b_ada"]).reshape(B, N_MOD, D)
    cos_t, sin_t = _rope_tables(S)
    qt, k, vt, xl, gl = _in_proj(x, mod, P["w_in"], P["gq"], P["gk"], cos_t, sin_t, P["ones_blk"], P["d_lru"])
    att = _attention(qt, k, vt)
    lru = _lru(xl, gl, P["conv_w"], P["conv_b"], P["w_gate"], P["b_gate"], P["lam"])
    x1, u2w, logits_t = _out_proj(att, lru, x, mod, P["ga"], P["gr"], P["w_out"], P["ln1_g"], P["ln1_b"], P["wr_split"])
    w_t, eid, rank, cnt = _route(logits_t, P["router_bias"])
    yg = _routed_experts(u2w, eid, rank, cnt, P["w_exp_in"], P["w_exp_out"])
    return _moe_out(u2w, x1, yg, w_t, mod, P["w_sh_in"], P["w_sh_out"], P["ln2_g"], P["ln2_b"])


def kernel(x_prompt, x_sample, c_prompt, c_sample, w_ada, b_ada, w_in, q_norm_g, k_norm_g, conv_w, conv_b, lru_wa, lru_ba, lru_wx, lru_bx, lru_lambda, attn_out_g, lru_out_g, w_out, ln1_g, ln1_b, w_router, router_bias, w_exp_in, w_exp_out, w_sh_in, w_sh_out, ln2_g, ln2_b):
    l = 0
    D = x_prompt.shape[-1]
    d_lru = lru_out_g.shape[-1]
    lru_half = 2 * LANES
    w_gate, b_gate = _gate_weights(lru_wa[l], lru_ba[l], lru_wx[l], lru_bx[l], lru_half)
    idx = jnp.arange(2 * LANES)
    ones_blk = (idx[:, None] // HEAD_DIM == idx[None, :] // HEAD_DIM).astype(BF16)
    P = {
        "w_ada": w_ada[l], "b_ada": b_ada[l],
        "w_in": w_in[l].astype(BF16),
        "gq": (jnp.tile(q_norm_g[l], N_Q_HEADS) * (HEAD_DIM ** -0.5 * LOG2_E)).reshape(1, D_ATT),
        "gk": jnp.tile(k_norm_g[l], N_KV_HEADS).reshape(1, D_KV),
        "ones_blk": ones_blk, "d_lru": d_lru,
        "conv_w": conv_w[l], "conv_b": conv_b[l].reshape(1, d_lru),
        "w_gate": w_gate, "b_gate": b_gate, "lam": lru_lambda[l],
        "ga": attn_out_g[l].reshape(1, D_ATT), "gr": lru_out_g[l].reshape(1, d_lru),
        "w_out": w_out[l].astype(BF16),
        "ln1_g": ln1_g[l].reshape(1, D), "ln1_b": ln1_b[l].reshape(1, D),
        "wr_split": _split_bf16(w_router[l]), "router_bias": router_bias[l],
        "w_exp_in": w_exp_in[l], "w_exp_out": w_exp_out[l],
        "w_sh_in": w_sh_in[l].astype(BF16), "w_sh_out": w_sh_out[l].astype(BF16),
        "ln2_g": ln2_g[l].reshape(1, D), "ln2_b": ln2_b[l].reshape(1, D),
    }
    return (_encoder(x_prompt, c_prompt, P), _encoder(x_sample, c_sample, P))
```

```python
import functools

import jax
import jax.numpy as jnp
from jax import lax
from jax.experimental import pallas as pl
from jax.experimental.pallas import tpu as pltpu
from jax.experimental.pallas import tpu_sc as plsc

F32 = jnp.float32
BF16 = jnp.bfloat16
U32 = jnp.uint32
HIGHEST = lax.Precision.HIGHEST

HEAD_DIM = 64
N_Q_HEADS = 8
N_KV_HEADS = 2
Q_PER_KV = N_Q_HEADS // N_KV_HEADS
D_ATT = N_Q_HEADS * HEAD_DIM
D_KV = N_KV_HEADS * HEAD_DIM
CONV_W = 4
LRU_C = 8.0
GRID_W = 64
ROPE_THETA = 10000.0
ROPE_FREQS = HEAD_DIM // 4
N_EXPERTS = 64
TOP_K = 6
N_GROUPS = 8
GROUP_SIZE = N_EXPERTS // N_GROUPS
TOPK_GROUPS = 4
ROUTED_SCALE = 2.5
N_MOD = 6
EPS = 1e-6
DEPTH = 1
DN_ALPHA = (2.0 * DEPTH) ** 0.25
LOG2_E = 1.4426950408889634
LRU_PITCH_PAD = 4
SQRT_FLOOR = 1e-30

LANES = 128
SUBLANES = 8
VMEM_LIMIT_BYTES = 56 * 1024 * 1024

ROW_SPLIT = 2
SC_WINDOW = 128
ATTN_BLOCK = 1024
ATTN_SUB = 512
ROUTE_TILE = 512
EXPERT_BLOCK = 1024
IN_PROJ_TILE = 512
PROJ_TILE = 1024
MOE_OUT_TILE = 512


def _tile(n, pref):
    if n <= pref:
        return n
    for t in range(pref, 0, -1):
        if n % t == 0 and t % SUBLANES == 0:
            return t
    return n


def _params(*sem):
    return pltpu.CompilerParams(dimension_semantics=sem, vmem_limit_bytes=VMEM_LIMIT_BYTES)


def _layer_norm(x):
    mu = jnp.mean(x, axis=-1, keepdims=True)
    xc = x - mu
    var = jnp.mean(xc * xc, axis=-1, keepdims=True)
    return xc * lax.rsqrt(var + EPS)


def _rms(x):
    return x * lax.rsqrt(jnp.mean(x * x, axis=-1, keepdims=True) + EPS)


def _pack_pair(a, b):
    ah = lax.bitcast_convert_type(a.astype(BF16).astype(F32), U32)
    bh = lax.bitcast_convert_type(b.astype(BF16).astype(F32), U32)
    return ah | (bh >> 16)


def _unpack_pair(w):
    a = lax.bitcast_convert_type(w & jnp.uint32(0xFFFF0000), F32)
    b = lax.bitcast_convert_type(w << 16, F32)
    return a, b


def _store_packed(ref, v):
    q = v.shape[-1] // (2 * ROW_SPLIT)
    for h in range(ROW_SPLIT):
        ref[h] = _pack_pair(v[:, 2 * h * q:(2 * h + 1) * q], v[:, (2 * h + 1) * q:(2 * h + 2) * q])


def _load_packed(ref):
    parts = []
    for h in range(ROW_SPLIT):
        parts.extend(_unpack_pair(ref[h]))
    return jnp.concatenate(parts, axis=-1)


def _ada_kernel(c_ref, w_ref, b_ref, o_ref):
    c = c_ref[...]
    sc = c * jax.nn.sigmoid(c)
    o_ref[...] = jnp.dot(sc, w_ref[...], precision=HIGHEST, preferred_element_type=F32) + b_ref[...]


def _ada(c, w_ada, b_ada):
    B, D = c.shape
    N = w_ada.shape[1]
    tn = _tile(N, 1024)
    return pl.pallas_call(
        _ada_kernel,
        grid=(N // tn,),
        in_specs=[
            pl.BlockSpec((B, D), lambda j: (0, 0)),
            pl.BlockSpec((D, tn), lambda j: (0, j)),
            pl.BlockSpec((1, tn), lambda j: (0, j)),
        ],
        out_specs=pl.BlockSpec((B, tn), lambda j: (0, j)),
        out_shape=jax.ShapeDtypeStruct((B, N), F32),
        compiler_params=_params("arbitrary"),
        name="ada",
    )(c, w_ada, b_ada.reshape(1, N))


def _group_sumsq(t, ones_blk):
    sq = t * t
    hi = sq.astype(BF16)
    lo = (sq - hi.astype(F32)).astype(BF16)
    return (jnp.dot(hi, ones_blk, preferred_element_type=F32)
            + jnp.dot(lo, ones_blk, preferred_element_type=F32))


def _rope(t, cos, sin_signed, first_half):
    fwd = pltpu.roll(t, LANES - ROPE_FREQS, axis=1)
    bwd = pltpu.roll(t, ROPE_FREQS, axis=1)
    return t * cos + jnp.where(first_half, fwd, bwd) * sin_signed


def _in_proj_kernel(x_ref, mod_ref, w_ref, gq_ref, gk_ref, cos_ref, sin_ref, ones_ref,
                    qt_ref, k_ref, vt_ref, xl_ref, gl_ref):
    x = x_ref[...]
    shift1 = mod_ref[0:1, :]
    scale1 = mod_ref[1:2, :]
    u = _layer_norm(x) * (1.0 + scale1) + shift1
    proj = jnp.dot(u.astype(BF16), w_ref[...], preferred_element_type=F32)

    cos = cos_ref[...]
    sin_signed = sin_ref[...]
    lane = lax.broadcasted_iota(jnp.int32, cos.shape, 1)
    first_half = (lane % (2 * ROPE_FREQS)) < ROPE_FREQS
    ones2 = ones_ref[...]
    ones1 = ones2[0:LANES, 0:LANES]

    def norm_rope(t, g, ones_blk):
        ss = _group_sumsq(t, ones_blk)
        tn = t * lax.rsqrt(ss * (1.0 / HEAD_DIM) + EPS) * g
        chunks = [_rope(tn[:, c * LANES:(c + 1) * LANES], cos, sin_signed, first_half)
                  for c in range(t.shape[1] // LANES)]
        return chunks

    w2 = 2 * LANES
    for half in range(D_ATT // w2):
        t = proj[:, half * w2:(half + 1) * w2]
        chunks = norm_rope(t, gq_ref[:, half * w2:(half + 1) * w2], ones2)
        for c, ch in enumerate(chunks):
            lo = half * w2 + c * LANES
            qt_ref[lo:lo + LANES, :] = ch.T.astype(BF16)
    k = proj[:, D_ATT:D_ATT + D_KV]
    (kr,) = norm_rope(k, gk_ref[...], ones1)
    k_ref[...] = kr.astype(BF16)
    vt_ref[...] = proj[:, D_ATT + D_KV:D_ATT + 2 * D_KV].T.astype(BF16)
    o = D_ATT + 2 * D_KV
    d_lru = xl_ref.shape[-1]
    xl_ref[...] = proj[:, o:o + d_lru]
    gl_ref[...] = proj[:, o + d_lru:o + 2 * d_lru]


def _in_proj(x, mod, w_in_bf, gq, gk, cos_t, sin_t, ones_blk, d_lru):
    B, S, D = x.shape
    tm = _tile(S, IN_PROJ_TILE)
    d_in = w_in_bf.shape[1]
    full = lambda shape: pl.BlockSpec(shape, lambda b, i: tuple(0 for _ in shape))
    return pl.pallas_call(
        _in_proj_kernel,
        grid=(B, S // tm),
        in_specs=[
            pl.BlockSpec((None, tm, D), lambda b, i: (b, i, 0)),
            pl.BlockSpec((None, N_MOD, D), lambda b, i: (b, 0, 0)),
            full((D, d_in)),
            full((1, D_ATT)),
            full((1, D_KV)),
            pl.BlockSpec((tm, LANES), lambda b, i: (i, 0)),
            pl.BlockSpec((tm, LANES), lambda b, i: (i, 0)),
            full((2 * LANES, 2 * LANES)),
        ],
        out_specs=[
            pl.BlockSpec((None, D_ATT, tm), lambda b, i: (b, 0, i)),
            pl.BlockSpec((None, tm, D_KV), lambda b, i: (b, i, 0)),
            pl.BlockSpec((None, D_KV, tm), lambda b, i: (b, 0, i)),
            pl.BlockSpec((None, tm, d_lru), lambda b, i: (b, i, 0)),
            pl.BlockSpec((None, tm, d_lru), lambda b, i: (b, i, 0)),
        ],
        out_shape=[
            jax.ShapeDtypeStruct((B, D_ATT, S), BF16),
            jax.ShapeDtypeStruct((B, S, D_KV), BF16),
            jax.ShapeDtypeStruct((B, D_KV, S), BF16),
            jax.ShapeDtypeStruct((B, S, d_lru), F32),
            jax.ShapeDtypeStruct((B, S, d_lru), F32),
        ],
        compiler_params=_params("parallel", "parallel"),
        name="in_proj",
    )(x, mod, w_in_bf, gq, gk, cos_t, sin_t, ones_blk)


def _attn_steps(qt_ref, k_ref, vt_ref, o_ref, sub):
    tq = qt_ref.shape[-1]
    k = k_ref[...]
    zeros = jnp.zeros((HEAD_DIM, sub), BF16)
    ones_rows = jnp.ones((2 * SUBLANES, k.shape[0]), BF16)
    units = [(q0, pair) for q0 in range(0, tq, sub) for pair in range(N_Q_HEADS // 2)]

    def scores(q0, pair):
        j = (2 * pair) // Q_PER_KV
        cols = []
        for h in (2 * pair, 2 * pair + 1):
            qt = qt_ref[h * HEAD_DIM:(h + 1) * HEAD_DIM, q0:q0 + sub]
            cols.append(jnp.concatenate([qt, zeros] if j == 0 else [zeros, qt], axis=0))
        return jnp.dot(k, jnp.concatenate(cols, axis=1), preferred_element_type=F32)

    def finish(q0, pair, st):
        j = (2 * pair) // Q_PER_KV
        vt = jnp.concatenate([vt_ref[j * HEAD_DIM:(j + 1) * HEAD_DIM, :], ones_rows], axis=0)
        m = jnp.max(st, axis=0, keepdims=True)
        p = jnp.exp2(st - m)
        ol = jnp.dot(vt, p.astype(BF16), preferred_element_type=F32)
        ot = ol[:HEAD_DIM, :] / ol[HEAD_DIM:HEAD_DIM + 1, :]
        o2 = jnp.concatenate([ot[:, :sub], ot[:, sub:]], axis=0)
        o_ref[q0:q0 + sub, 2 * pair * HEAD_DIM:(2 * pair + 2) * HEAD_DIM] = o2.T.astype(BF16)

    st = scores(*units[0])
    for n, unit in enumerate(units):
        st_next = scores(*units[n + 1]) if n + 1 < len(units) else None
        finish(*unit, st)
        st = st_next
        yield


def _attn_kernel(qt_ref, k_ref, vt_ref, o_ref, *, sub):
    for _ in _attn_steps(qt_ref, k_ref, vt_ref, o_ref, sub):
        pass


def _attention(qt, k, vt):
    B, _, S = qt.shape
    tq = _tile(S, ATTN_BLOCK)
    sub = _tile(tq, ATTN_SUB)
    return pl.pallas_call(
        functools.partial(_attn_kernel, sub=sub),
        grid=(B, S // tq),
        in_specs=[
            pl.BlockSpec((None, D_ATT, tq), lambda b, i: (b, 0, i)),
            pl.BlockSpec((None, S, D_KV), lambda b, i: (b, 0, 0)),
            pl.BlockSpec((None, D_KV, S), lambda b, i: (b, 0, 0)),
        ],
        out_specs=pl.BlockSpec((None, tq, D_ATT), lambda b, i: (b, i, 0)),
        out_shape=jax.ShapeDtypeStruct((B, S, D_ATT), BF16),
        compiler_params=_params("parallel", "parallel"),
        name="attn",
    )(qt, k, vt)


def _lru_steps(xl_ref, gl_ref, cw_ref, cb_ref, w_ref, bias_ref, lam_ref, o_ref,
               xc_ref, af_ref, bf_ref, ab_ref, bb_ref, seg, pitch):
    S, C = xl_ref.shape
    n_seg = S // seg
    slabs = C // LANES
    xl = xl_ref[...]
    row = lax.broadcasted_iota(jnp.int32, (S, 1), 0)
    xm2 = jnp.where(row >= 2, pltpu.roll(xl, 2, axis=0), 0.0)
    xm1 = jnp.where(row >= 1, pltpu.roll(xl, 1, axis=0), 0.0)
    xp1 = jnp.where(row < S - 1, pltpu.roll(xl, S - 1, axis=0), 0.0)
    xc_ref[...] = (xm2 * cw_ref[0:1, :] + xm1 * cw_ref[1:2, :] + xl * cw_ref[2:3, :]
                   + xp1 * cw_ref[3:4, :] + cb_ref[...])
    yield

    lam = lam_ref[...]
    nlam = -lam
    softplus = jnp.maximum(nlam, 0.0) + jnp.log(1.0 + jnp.exp(-jnp.abs(nlam)))
    decay_log2 = (-LRU_C * LOG2_E) * softplus

    for s in range(n_seg):
        xc = xc_ref[s * seg:(s + 1) * seg, :]
        g = jnp.dot(xc.astype(BF16), w_ref[...], preferred_element_type=F32) + bias_ref[...]
        for d, (a_ref, b_ref) in enumerate(((af_ref, bf_ref), (ab_ref, bb_ref))):
            r = jax.nn.sigmoid(g[:, (2 * d) * C:(2 * d + 1) * C])
            i = jax.nn.sigmoid(g[:, (2 * d + 1) * C:(2 * d + 2) * C])
            a = jnp.exp2(r * decay_log2[d:d + 1, :])
            gain_sq = 1.0 - a * a
            gain = gain_sq * lax.rsqrt(jnp.maximum(gain_sq, SQRT_FLOOR))
            bt = gain * i * xc
            for sl in range(slabs):
                a_ref[sl, s * pitch:s * pitch + seg, :] = a[:, sl * LANES:(sl + 1) * LANES]
                b_ref[sl, s * pitch:s * pitch + seg, :] = bt[:, sl * LANES:(sl + 1) * LANES]
        yield

    def strided(ref, sl, i):
        return ref.at[sl][pl.ds(i, n_seg, stride=pitch), :]

    def put(ref, sl, i, v):
        ref.at[sl][pl.ds(i, n_seg, stride=pitch), :] = v

    def step(i, carry):
        ib = seg - 1 - i
        new = []
        for sl in range(slabs):
            hf, cf, hb, cb = carry[sl]
            a = strided(af_ref, sl, i)
            hf = a * hf + strided(bf_ref, sl, i)
            cf = a * cf
            put(bf_ref, sl, i, hf)
            put(af_ref, sl, i, cf)
            a = strided(ab_ref, sl, ib)
            hb = a * hb + strided(bb_ref, sl, ib)
            cb = a * cb
            put(bb_ref, sl, ib, hb)
            put(ab_ref, sl, ib, cb)
            new.append((hf, cf, hb, cb))
        return tuple(new)

    zero = jnp.zeros((n_seg, LANES), F32)
    one = jnp.ones((n_seg, LANES), F32)
    ends = lax.fori_loop(0, seg, step, tuple((zero, one, zero, one) for _ in range(slabs)), unroll=4)

    carry_in = []
    for sl in range(slabs):
        hf, cf, hb, cb = ends[sl]
        c = jnp.zeros((1, LANES), F32)
        rows_f = []
        for s in range(n_seg):
            rows_f.append(c)
            c = hf[s:s + 1, :] + cf[s:s + 1, :] * c
        c = jnp.zeros((1, LANES), F32)
        rows_b = [None] * n_seg
        for s in reversed(range(n_seg)):
            rows_b[s] = c
            c = hb[s:s + 1, :] + cb[s:s + 1, :] * c
        carry_in.append((jnp.concatenate(rows_f, axis=0), jnp.concatenate(rows_b, axis=0)))

    def fix(i, carry):
        for sl in range(slabs):
            cin_f, cin_b = carry_in[sl]
            h = (strided(bf_ref, sl, i) + strided(af_ref, sl, i) * cin_f
                 + strided(bb_ref, sl, i) + strided(ab_ref, sl, i) * cin_b)
            put(bf_ref, sl, i, h)
        return carry

    lax.fori_loop(0, seg, fix, 0, unroll=4)

    for s in range(n_seg):
        h = jnp.concatenate([bf_ref[sl, s * pitch:s * pitch + seg, :] for sl in range(slabs)], axis=1)
        gl = gl_ref[s * seg:(s + 1) * seg, :]
        o_ref[s * seg:(s + 1) * seg, :] = (jax.nn.gelu(gl) * h).astype(BF16)


def _lru_kernel(*refs, seg, pitch):
    for _ in _lru_steps(*refs, seg, pitch):
        pass


def _mixer_kernel(qt_ref, k_ref, vt_ref, xl_ref, gl_ref, cw_ref, cb_ref, w_ref, bias_ref, lam_ref,
                  att_ref, lru_ref, xc_ref, af_ref, bf_ref, ab_ref, bb_ref, *, sub, seg, pitch):
    attn = _attn_steps(qt_ref, k_ref, vt_ref, att_ref, sub)
    lru = _lru_steps(xl_ref, gl_ref, cw_ref, cb_ref, w_ref, bias_ref, lam_ref, lru_ref,
                     xc_ref, af_ref, bf_ref, ab_ref, bb_ref, seg, pitch)
    live = [attn, lru]
    while live:
        for gen in list(live):
            if next(gen, _DONE) is _DONE:
                live.remove(gen)


_DONE = object()


def _lru_specs(S, C, col, half):
    return [
        pl.BlockSpec((None, S, C), col),
        pl.BlockSpec((None, S, C), col),
        pl.BlockSpec((CONV_W, C), lambda b, h: (0, half(b, h))),
        pl.BlockSpec((1, C), lambda b, h: (0, half(b, h))),
        pl.BlockSpec((None, C, 4 * C), lambda b, h: (half(b, h), 0, 0)),
        pl.BlockSpec((None, 1, 4 * C), lambda b, h: (half(b, h), 0, 0)),
        pl.BlockSpec((2, C), lambda b, h: (0, half(b, h))),
    ]


def _lru(xl, gl, conv_w, conv_b, w_gate, b_gate, lam):
    B, S, d_lru = xl.shape
    nh, C, _ = w_gate.shape
    seg = S // SUBLANES
    pitch = seg + LRU_PITCH_PAD
    col = lambda b, h: (b, 0, h)
    state = pltpu.VMEM((C // LANES, SUBLANES * pitch, LANES), F32)
    return pl.pallas_call(
        functools.partial(_lru_kernel, seg=seg, pitch=pitch),
        grid=(B, nh),
        in_specs=_lru_specs(S, C, col, lambda b, h: h),
        out_specs=pl.BlockSpec((None, S, C), col),
        out_shape=jax.ShapeDtypeStruct((B, S, d_lru), BF16),
        scratch_shapes=[pltpu.VMEM((S, C), F32), state, state, state, state],
        compiler_params=_params("parallel", "parallel"),
        name="lru",
    )(xl, gl, conv_w, conv_b, w_gate, b_gate, lam)


def _mixer(qt, k, vt, xl, gl, conv_w, conv_b, w_gate, b_gate, lam):
    B, S, d_lru = xl.shape
    nh, C, _ = w_gate.shape
    tq = _tile(S, ATTN_BLOCK)
    if S // tq != nh:
        return _attention(qt, k, vt), _lru(xl, gl, conv_w, conv_b, w_gate, b_gate, lam)
    sub = _tile(tq, ATTN_SUB)
    seg = S // SUBLANES
    pitch = seg + LRU_PITCH_PAD
    col = lambda b, i: (b, 0, i)
    state = pltpu.VMEM((C // LANES, SUBLANES * pitch, LANES), F32)
    return pl.pallas_call(
        functools.partial(_mixer_kernel, sub=sub, seg=seg, pitch=pitch),
        grid=(B, nh),
        in_specs=[
            pl.BlockSpec((None, D_ATT, tq), col),
            pl.BlockSpec((None, S, D_KV), lambda b, i: (b, 0, 0)),
            pl.BlockSpec((None, D_KV, S), lambda b, i: (b, 0, 0)),
        ] + _lru_specs(S, C, col, lambda b, i: i),
        out_specs=[pl.BlockSpec((None, tq, D_ATT), lambda b, i: (b, i, 0)), pl.BlockSpec((None, S, C), col)],
        out_shape=[jax.ShapeDtypeStruct((B, S, D_ATT), BF16), jax.ShapeDtypeStruct((B, S, d_lru), BF16)],
        scratch_shapes=[pltpu.VMEM((S, C), F32), state, state, state, state],
        compiler_params=_params("parallel", "parallel"),
        name="mixer",
    )(qt, k, vt, xl, gl, conv_w, conv_b, w_gate, b_gate, lam)


def _out_proj_kernel(att_ref, lru_ref, x_ref, mod_ref, ga_ref, gr_ref, w_ref, g1_ref, b1_ref, wr_ref,
                     x1_ref, u2_ref, lg_ref):
    d_att = att_ref.shape[-1]
    an = _rms(att_ref[...].astype(F32)) * ga_ref[...]
    rn = _rms(lru_ref[...].astype(F32)) * gr_ref[...]
    mix = (jnp.dot(an.astype(BF16), w_ref[0:d_att, :], preferred_element_type=F32)
           + jnp.dot(rn.astype(BF16), w_ref[d_att:, :], preferred_element_type=F32))
    gate1 = mod_ref[2:3, :]
    shift2 = mod_ref[3:4, :]
    scale2 = mod_ref[4:5, :]
    x1 = _layer_norm(DN_ALPHA * x_ref[...] + gate1 * mix) * g1_ref[...] + b1_ref[...]
    x1_ref[...] = x1
    u2 = _layer_norm(x1) * (1.0 + scale2) + shift2
    _store_packed(u2_ref, u2)
    E = lg_ref.shape[0]
    u_hi = u2.astype(BF16)
    u_lo = (u2 - u_hi.astype(F32)).astype(BF16)
    parts = (jnp.dot(u_hi, wr_ref[...], preferred_element_type=F32)
             + jnp.dot(u_lo, wr_ref[...], preferred_element_type=F32))
    parts_t = parts.T
    lg_ref[...] = parts_t[:E, :] + parts_t[E:, :]


def _out_proj(att, lru, x, mod, ga, gr, w_out_bf, g1, b1, wr_split):
    B, S, D = x.shape
    tm = _tile(S, PROJ_TILE)
    d_att, d_lru = att.shape[-1], lru.shape[-1]
    full = lambda shape: pl.BlockSpec(shape, lambda b, i: tuple(0 for _ in shape))
    row = lambda w: pl.BlockSpec((None, tm, w), lambda b, i: (b, i, 0))
    packed = pl.BlockSpec((ROW_SPLIT, None, tm, D // (2 * ROW_SPLIT)), lambda b, i: (0, b, i, 0))
    return pl.pallas_call(
        _out_proj_kernel,
        grid=(B, S // tm),
        in_specs=[
            row(d_att), row(d_lru), row(D),
            pl.BlockSpec((None, N_MOD, D), lambda b, i: (b, 0, 0)),
            full((1, d_att)), full((1, d_lru)), full((d_att + d_lru, D)),
            full((1, D)), full((1, D)), full((D, 2 * N_EXPERTS)),
        ],
        out_specs=[row(D), packed, pl.BlockSpec((None, N_EXPERTS, tm), lambda b, i: (b, 0, i))],
        out_shape=[
            jax.ShapeDtypeStruct((B, S, D), F32),
            jax.ShapeDtypeStruct((ROW_SPLIT, B, S, D // (2 * ROW_SPLIT)), U32),
            jax.ShapeDtypeStruct((B, N_EXPERTS, S), F32),
        ],
        compiler_params=_params("parallel", "parallel"),
        name="out_proj",
    )(att, lru, x, mod, ga, gr, w_out_bf, g1, b1, wr_split)


def _route_kernel(lg_ref, bias_ref, w_ref, eid_ref, rank_ref, cnt_ref):
    E, T = lg_ref.shape
    s = jax.nn.sigmoid(lg_ref[...])
    choice = s + bias_ref[...]
    neg = -jnp.inf

    c3 = choice.reshape(N_GROUPS, GROUP_SIZE, T)
    mem = lax.broadcasted_iota(jnp.int32, c3.shape, 1)
    m1 = jnp.max(c3, axis=1, keepdims=True)
    first = jnp.min(jnp.where(c3 == m1, mem, GROUP_SIZE), axis=1, keepdims=True)
    m2 = jnp.max(jnp.where(mem == first, neg, c3), axis=1, keepdims=True)
    gs = (m1 + m2).reshape(N_GROUPS, T)

    gi = lax.broadcasted_iota(jnp.int32, gs.shape, 0)
    rank = jnp.zeros(gs.shape, jnp.int32)
    for g2 in range(N_GROUPS):
        other = gs[g2:g2 + 1, :]
        beats = (other > gs) | ((other == gs) & (gi > g2))
        rank = rank + beats.astype(jnp.int32)
    gmask = (rank < TOPK_GROUPS).astype(F32).reshape(N_GROUPS, 1, T)
    emask = jnp.broadcast_to(gmask, (N_GROUPS, GROUP_SIZE, T)).reshape(E, T) > 0.5
    cur = jnp.where(emask, choice, neg)

    ei = lax.broadcasted_iota(jnp.int32, (E, T), 0)
    sel = jnp.zeros((E, T), F32)
    eids, ws = [], []
    for _ in range(TOP_K):
        m = jnp.max(cur, axis=0, keepdims=True)
        first_e = jnp.min(jnp.where(cur == m, ei, E), axis=0, keepdims=True)
        hit = ei == first_e
        eids.append(first_e)
        ws.append(jnp.sum(jnp.where(hit, s, 0.0), axis=0, keepdims=True))
        sel = jnp.where(hit, 1.0, sel)
        cur = jnp.where(hit, neg, cur)

    denom = ws[0]
    for w in ws[1:]:
        denom = denom + w
    wk = [w / denom * ROUTED_SCALE for w in ws]
    w_ref[...] = jnp.concatenate(wk + [jnp.zeros((LANES - TOP_K, T), F32)], axis=0).T

    sel_bf = sel.astype(BF16)
    t_row = lax.broadcasted_iota(jnp.int32, (T, T), 0)
    t_col = lax.broadcasted_iota(jnp.int32, (T, T), 1)
    before = (t_row < t_col).astype(BF16)
    cum = jnp.dot(sel_bf, before, preferred_element_type=F32)
    ranks = [jnp.sum(jnp.where(ei == e, cum, 0.0), axis=0, keepdims=True).astype(jnp.int32) for e in eids]
    pad = [jnp.zeros((SUBLANES - TOP_K, T), jnp.int32)]
    eid_ref[...] = jnp.concatenate(eids + pad, axis=0)
    rank_ref[...] = jnp.concatenate(ranks + pad, axis=0)
    cnt_ref[...] = lax.dot_general(jnp.ones((SUBLANES, T), BF16), sel_bf, (((1,), (1,)), ((), ())),
                                   preferred_element_type=F32)


def _route(logits_t, router_bias):
    B, E, S = logits_t.shape
    tr = _tile(S, ROUTE_TILE)
    nt = S // tr
    tok = lambda b, i: (b, 0, i)
    return pl.pallas_call(
        _route_kernel,
        grid=(B, nt),
        in_specs=[
            pl.BlockSpec((None, E, tr), tok),
            pl.BlockSpec((E, 1), lambda b, i: (0, 0)),
        ],
        out_specs=[
            pl.BlockSpec((None, tr, LANES), lambda b, i: (b, i, 0)),
            pl.BlockSpec((None, SUBLANES, tr), tok),
            pl.BlockSpec((None, SUBLANES, tr), tok),
            pl.BlockSpec((None, None, SUBLANES, E), lambda b, i: (b, i, 0, 0)),
        ],
        out_shape=[
            jax.ShapeDtypeStruct((B, S, LANES), F32),
            jax.ShapeDtypeStruct((B, SUBLANES, S), jnp.int32),
            jax.ShapeDtypeStruct((B, SUBLANES, S), jnp.int32),
            jax.ShapeDtypeStruct((B, nt, SUBLANES, E), F32),
        ],
        compiler_params=_params("parallel", "parallel"),
        name="route",
    )(logits_t, router_bias.reshape(E, 1))


def _slots_kernel(eid_ref, rank_ref, base_ref, pos_ref):
    nt, E, _ = base_ref.shape
    T = eid_ref.shape[-1] // nt
    ei = lax.broadcasted_iota(jnp.int32, (E, T), 0)
    pad = jnp.zeros((SUBLANES - TOP_K, T), jnp.int32)
    for t in range(nt):
        base = base_ref[t]
        eid = eid_ref[:, t * T:(t + 1) * T]
        rows = [jnp.sum(jnp.where(ei == eid[k:k + 1, :], base, 0), axis=0, keepdims=True)
                for k in range(TOP_K)]
        pos_ref[:, t * T:(t + 1) * T] = jnp.concatenate(rows + [pad], axis=0) + rank_ref[:, t * T:(t + 1) * T]


def _slots(eid, rank, base):
    B, _, S = eid.shape
    nt, E = base.shape[1], base.shape[2]
    seq = lambda b: (b, 0, 0)
    return pl.pallas_call(
        _slots_kernel,
        grid=(B,),
        in_specs=[
            pl.BlockSpec((None, SUBLANES, S), seq),
            pl.BlockSpec((None, SUBLANES, S), seq),
            pl.BlockSpec((None, nt, E, 1), lambda b: (b, 0, 0, 0)),
        ],
        out_specs=pl.BlockSpec((None, SUBLANES, S), seq),
        out_shape=jax.ShapeDtypeStruct((B, SUBLANES, S), jnp.int32),
        compiler_params=_params("parallel"),
        name="slots",
    )(eid, rank, base.reshape(B, nt, E, 1))


def _sc_mesh():
    return plsc.VectorSubcoreMesh(core_axis_name="core", subcore_axis_name="subcore")


def _sc_scatter_rows(src, idx2d, n_out):
    R, W = src.shape
    K = idx2d.shape[0]

    @functools.partial(pl.kernel, out_type=jax.ShapeDtypeStruct((n_out, W), src.dtype), mesh=_sc_mesh(),
                       scratch_types=[], name="sc_dispatch")
    def scatter(x_hbm, i_hbm, o_hbm):
        def body(x_vmem, i_vmem):
            pltpu.sync_copy(x_vmem, o_hbm.at[i_vmem.at[0]])

        pltpu.emit_pipeline(
            body, grid=(R // SC_WINDOW, K),
            in_specs=[pl.BlockSpec((SC_WINDOW, W), lambda i, k: (i, 0)),
                      pl.BlockSpec((1, SC_WINDOW), lambda i, k: (k, i))],
            out_specs=[],
            core_axis_name=("core", "subcore"),
            dimension_semantics=(pltpu.PARALLEL, pltpu.ARBITRARY),
        )(x_hbm, i_hbm)

    return scatter(src, idx2d)


def _sc_gather_rows(table, idx):
    N = idx.shape[0]
    W = table.shape[1]

    @functools.partial(pl.kernel, out_type=jax.ShapeDtypeStruct((N, W), table.dtype), mesh=_sc_mesh(),
                       scratch_types=[], name="sc_combine")
    def gather(t_hbm, i_hbm, o_hbm):
        def body(i_vmem, o_vmem):
            pltpu.sync_copy(t_hbm.at[i_vmem.at[0]], o_vmem)

        pltpu.emit_pipeline(
            body, grid=(N // SC_WINDOW,),
            in_specs=[pl.BlockSpec((1, SC_WINDOW), lambda i: (0, i))],
            out_specs=[pl.BlockSpec((SC_WINDOW, W), lambda i: (i, 0))],
            core_axis_name=("core", "subcore"),
            dimension_semantics=(pltpu.PARALLEL,),
        )(i_hbm, o_hbm)

    return gather(table, idx.reshape(1, N))


def _swiglu_hidden(u, w_in):
    gu = jnp.dot(u, w_in, preferred_element_type=F32)
    d = gu.shape[-1] // 2
    g, up = gu[:, :d], gu[:, d:]
    return g * jax.nn.sigmoid(g) * up


def _experts_kernel(blk_ref, used_ref, xs_ref, wi_ref, wo_ref, ys_ref, wi_bf, wo_bf):
    nb = pl.program_id(0)

    @pl.when((nb == 0) | (blk_ref[nb] != blk_ref[jnp.maximum(nb - 1, 0)]))
    def _():
        wi_bf[...] = wi_ref[...].astype(BF16)
        wo_bf[...] = wo_ref[...].astype(BF16)

    @pl.when(nb < used_ref[0])
    def _():
        x = _load_packed(xs_ref).astype(BF16)
        h = _swiglu_hidden(x, wi_bf[...])
        _store_packed(ys_ref, jnp.dot(h.astype(BF16), wo_bf[...], preferred_element_type=F32))


def _experts(xs, blk_expert, n_used, wei, weo, bm):
    _, P, Wd = xs.shape
    E, D, d2 = wei.shape
    de = weo.shape[1]
    rows = pl.BlockSpec((ROW_SPLIT, bm, Wd), lambda nb, blk, used: (0, jnp.minimum(nb, used[0] - 1), 0))
    return pl.pallas_call(
        _experts_kernel,
        grid_spec=pltpu.PrefetchScalarGridSpec(
            num_scalar_prefetch=2,
            grid=(P // bm,),
            in_specs=[
                rows,
                pl.BlockSpec((None, D, d2), lambda nb, blk, used: (blk[nb], 0, 0)),
                pl.BlockSpec((None, de, D), lambda nb, blk, used: (blk[nb], 0, 0)),
            ],
            out_specs=rows,
            scratch_shapes=[pltpu.VMEM((D, d2), BF16), pltpu.VMEM((de, D), BF16)],
        ),
        out_shape=jax.ShapeDtypeStruct(xs.shape, U32),
        compiler_params=_params("arbitrary"),
        name="experts",
    )(blk_expert, n_used, xs, wei, weo)


def _moe_out_kernel(u_ref, x1_ref, yg_ref, w_ref, mod_ref, wsi_ref, wso_ref, g2_ref, b2_ref, o_ref):
    u = _load_packed(u_ref).astype(BF16)
    hs = _swiglu_hidden(u, wsi_ref[...])
    acc = jnp.dot(hs.astype(BF16), wso_ref[...], preferred_element_type=F32)
    w = w_ref[...]
    for k in range(TOP_K):
        acc = acc + w[:, k:k + 1] * _load_packed(yg_ref.at[k])
    gate2 = mod_ref[5:6, :]
    y = DN_ALPHA * x1_ref[...] + gate2 * acc
    o_ref[...] = _layer_norm(y) * g2_ref[...] + b2_ref[...]


def _moe_out(u2w, x1, yg, w_t, mod, wsi, wso, g2, b2):
    B, S, D = x1.shape
    Wd = u2w.shape[-1]
    tm = _tile(S, MOE_OUT_TILE)
    full = lambda shape: pl.BlockSpec(shape, lambda b, i: tuple(0 for _ in shape))
    return pl.pallas_call(
        _moe_out_kernel,
        grid=(B, S // tm),
        in_specs=[
            pl.BlockSpec((ROW_SPLIT, None, tm, Wd), lambda b, i: (0, b, i, 0)),
            pl.BlockSpec((None, tm, D), lambda b, i: (b, i, 0)),
            pl.BlockSpec((TOP_K, ROW_SPLIT, None, tm, Wd), lambda b, i: (0, 0, b, i, 0)),
            pl.BlockSpec((None, tm, LANES), lambda b, i: (b, i, 0)),
            pl.BlockSpec((None, N_MOD, D), lambda b, i: (b, 0, 0)),
            full(wsi.shape), full(wso.shape), full((1, D)), full((1, D)),
        ],
        out_specs=pl.BlockSpec((None, tm, D), lambda b, i: (b, i, 0)),
        out_shape=jax.ShapeDtypeStruct((B, S, D), F32),
        compiler_params=_params("parallel", "parallel"),
        name="moe_out",
    )(u2w, x1, yg, w_t, mod, wsi, wso, g2, b2)


def _routed_experts(u2w, eid, rank, cnt, wei, weo):
    _, B, S, Wd = u2w.shape
    T = B * S
    E = wei.shape[0]
    bm = _tile(S, EXPERT_BLOCK)
    n_blocks = (T * TOP_K) // bm + E
    P = n_blocks * bm

    cnt_te = cnt[:, :, 0, :].astype(jnp.int32).reshape(-1, E)
    total = jnp.sum(cnt_te, axis=0)
    padded = (total + bm - 1) // bm * bm
    region_end = jnp.cumsum(padded)
    region_start = region_end - padded
    base = region_start[None, :] + jnp.cumsum(cnt_te, axis=0) - cnt_te
    blk_start = jnp.arange(n_blocks, dtype=jnp.int32) * bm
    blk_expert = jnp.minimum(jnp.sum(region_end[None, :] <= blk_start[:, None], axis=1), E - 1).astype(jnp.int32)
    n_used = (region_end[-1] // bm).astype(jnp.int32).reshape(1)

    pos = _slots(eid, rank, base.reshape(B, -1, E))
    pos = jnp.transpose(pos[:, :TOP_K, :], (1, 0, 2)).reshape(TOP_K, T)
    halves = jnp.stack([pos + h * P for h in range(ROW_SPLIT)], axis=1)

    xs = _sc_scatter_rows(u2w.reshape(ROW_SPLIT * T, Wd), halves.reshape(TOP_K, ROW_SPLIT * T), ROW_SPLIT * P)
    ys = _experts(xs.reshape(ROW_SPLIT, P, Wd), blk_expert, n_used, wei, weo, bm)
    yg = _sc_gather_rows(ys.reshape(ROW_SPLIT * P, Wd), halves.reshape(-1))
    return yg.reshape(TOP_K, ROW_SPLIT, B, S, Wd)


def _rope_tables(S):
    rows = S // GRID_W
    row_idx = jnp.repeat(jnp.arange(rows, dtype=F32), GRID_W)
    col_idx = jnp.tile(jnp.arange(GRID_W, dtype=F32), rows)
    inv_freq = ROPE_THETA ** (-jnp.arange(ROPE_FREQS, dtype=F32) / ROPE_FREQS)
    ang = jnp.stack([row_idx[:, None] * inv_freq, col_idx[:, None] * inv_freq], axis=1)
    cos, sin = jnp.cos(ang), jnp.sin(ang)
    cos_h = jnp.stack([cos, cos], axis=2).reshape(S, HEAD_DIM)
    sin_h = jnp.stack([-sin, sin], axis=2).reshape(S, HEAD_DIM)
    reps = LANES // HEAD_DIM
    return jnp.tile(cos_h, (1, reps)), jnp.tile(sin_h, (1, reps))


def _gate_weights(lru_wa, lru_ba, lru_wx, lru_bx, C):
    nb, blk, _ = lru_wa.shape[1:]
    d_lru = nb * blk
    per = C // blk

    def dense(w):
        w = w.reshape(d_lru // C, per, blk, blk)
        eye = jnp.eye(per, dtype=w.dtype)
        return jnp.einsum("hpcd,pq->hpcqd", w, eye).reshape(d_lru // C, C, C)

    w = jnp.concatenate([dense(lru_wa[0]), dense(lru_wx[0]), dense(lru_wa[1]), dense(lru_wx[1])], axis=-1)
    halves = lambda v: v.reshape(d_lru // C, 1, C)
    b = jnp.concatenate([halves(lru_ba[0]), halves(lru_bx[0]), halves(lru_ba[1]), halves(lru_bx[1])], axis=-1)
    return w.astype(BF16), b


def _split_bf16(w):
    hi = w.astype(BF16)
    lo = (w - hi.astype(F32)).astype(BF16)
    return jnp.concatenate([hi, lo], axis=1)


def _encoder(x, c, P):
    B, S, D = x.shape
    mod = _ada(c, P["w_ada"], P["b_ada"]).reshape(B, N_MOD, D)
    cos_t, sin_t = _rope_tables(S)
    qt, k, vt, xl, gl = _in_proj(x, mod, P["w_in"], P["gq"], P["gk"], cos_t, sin_t, P["ones_blk"], P["d_lru"])
    att, lru = _mixer(qt, k, vt, xl, gl, P["conv_w"], P["conv_b"], P["w_gate"], P["b_gate"], P["lam"])
    x1, u2w, logits_t = _out_proj(att, lru, x, mod, P["ga"], P["gr"], P["w_out"], P["ln1_g"], P["ln1_b"], P["wr_split"])
    w_t, eid, rank, cnt = _route(logits_t, P["router_bias"])
    yg = _routed_experts(u2w, eid, rank, cnt, P["w_exp_in"], P["w_exp_out"])
    return _moe_out(u2w, x1, yg, w_t, mod, P["w_sh_in"], P["w_sh_out"], P["ln2_g"], P["ln2_b"])


def kernel(x_prompt, x_sample, c_prompt, c_sample, w_ada, b_ada, w_in, q_norm_g, k_norm_g, conv_w, conv_b, lru_wa, lru_ba, lru_wx, lru_bx, lru_lambda, attn_out_g, lru_out_g, w_out, ln1_g, ln1_b, w_router, router_bias, w_exp_in, w_exp_out, w_sh_in, w_sh_out, ln2_g, ln2_b):
    l = 0
    D = x_prompt.shape[-1]
    d_lru = lru_out_g.shape[-1]
    lru_half = 2 * LANES
    w_gate, b_gate = _gate_weights(lru_wa[l], lru_ba[l], lru_wx[l], lru_bx[l], lru_half)
    idx = jnp.arange(2 * LANES)
    ones_blk = (idx[:, None] // HEAD_DIM == idx[None, :] // HEAD_DIM).astype(BF16)
    P = {
        "w_ada": w_ada[l], "b_ada": b_ada[l],
        "w_in": w_in[l].astype(BF16),
        "gq": (jnp.tile(q_norm_g[l], N_Q_HEADS) * (HEAD_DIM ** -0.5 * LOG2_E)).reshape(1, D_ATT),
        "gk": jnp.tile(k_norm_g[l], N_KV_HEADS).reshape(1, D_KV),
        "ones_blk": ones_blk, "d_lru": d_lru,
        "conv_w": conv_w[l], "conv_b": conv_b[l].reshape(1, d_lru),
        "w_gate": w_gate, "b_gate": b_gate, "lam": lru_lambda[l],
        "ga": attn_out_g[l].reshape(1, D_ATT), "gr": lru_out_g[l].reshape(1, d_lru),
        "w_out": w_out[l].astype(BF16),
        "ln1_g": ln1_g[l].reshape(1, D), "ln1_b": ln1_b[l].reshape(1, D),
        "wr_split": _split_bf16(w_router[l]), "router_bias": router_bias[l],
        "w_exp_in": w_exp_in[l], "w_exp_out": w_exp_out[l],
        "w_sh_in": w_sh_in[l].astype(BF16), "w_sh_out": w_sh_out[l].astype(BF16),
        "ln2_g": ln2_g[l].reshape(1, D), "ln2_b": ln2_b[l].reshape(1, D),
    }
    return (_encoder(x_prompt, c_prompt, P), _encoder(x_sample, c_sample, P))
```
